```python
import jax, jax.numpy as jnp
from jax import lax
import numpy as np

D_MODEL = 2048
BATCH = 2
SEQ = 16384
DEPTH = 2

CONV_CH = D_MODEL // 2
CONV_WIDTH = 31
NSA_HEADS = 16
NSA_HEAD_DIM = 64
NSA_KV_GROUPS = 4
NSA_HPG = NSA_HEADS // NSA_KV_GROUPS
NSA_Q = NSA_HEADS * NSA_HEAD_DIM
NSA_KV = NSA_KV_GROUPS * NSA_HEAD_DIM
CMP_STRIDE = 16
CMP_BLOCK = 2 * CMP_STRIDE
CMP_HIDDEN = 256
SLC_BLOCK = 64
SLC_TOPK = 16
WINDOW = 512
Q_BLOCK = 128
FORCE_SCORE = 1.0e3
MIX_WIDTH = CONV_CH + NSA_Q
EVEN_SPLITS = (2 * CONV_CH, NSA_Q, NSA_KV, NSA_KV, NSA_KV, NSA_KV, NSA_KV, NSA_KV, 3 * NSA_HEADS)
EVEN_IN = 2 * CONV_CH + NSA_Q + 6 * NSA_KV + 3 * NSA_HEADS
HG_HEADS = 16
HG_DK = 128
HG_DV = D_MODEL // HG_HEADS
HG_CHUNK = 64
ODD_SPLITS = (HG_HEADS * HG_DK, HG_HEADS * HG_DK, HG_HEADS * HG_DV, HG_HEADS * HG_DV)
ODD_IN = 2 * HG_HEADS * HG_DK + 2 * HG_HEADS * HG_DV
FFN_HIDDEN = ((8 * D_MODEL // 3 + 255) // 256) * 256
N_EVEN = (DEPTH + 1) // 2
N_ODD = DEPTH // 2
EPS = 1e-6
TINY = 1e-30

kernel_name = "hybrid_conv_nsa_hgrn2_trunk"


def _split_cols(u, sizes):
    offs = np.cumsum(np.array(sizes))[:-1].tolist()
    return jnp.split(u, offs, axis=-1)


def rmsnorm(x, w):
    xf = x.astype(jnp.float32)
    y = xf * lax.rsqrt(jnp.mean(xf * xf, axis=-1, keepdims=True) + EPS)
    return (y * w.astype(jnp.float32)).astype(x.dtype)


def alibi_slopes(n_heads):
    return 2.0 ** (-8.0 * jnp.arange(1, n_heads + 1, dtype=jnp.float32) / n_heads)


def masked_softmax(s, mask):
    s = jnp.where(mask, s, -jnp.inf)
    m = jnp.max(s, axis=-1, keepdims=True)
    m = jnp.where(jnp.isfinite(m), m, 0.0)
    p = jnp.exp(s - m)
    return p / jnp.maximum(jnp.sum(p, axis=-1, keepdims=True), TINY)


def conformer_conv(u, conv_w, conv_b, ln_w, ln_b):
    a, g = jnp.split(u, 2, axis=-1)
    h = a * jax.nn.sigmoid(g)
    h = lax.conv_general_dilated(h, conv_w.astype(h.dtype), window_strides=(1,),
                                 padding=[(CONV_WIDTH - 1, 0)],
                                 dimension_numbers=('NWC', 'WIO', 'NWC'),
                                 feature_group_count=CONV_CH) + conv_b
    hf = h.astype(jnp.float32)
    mu = jnp.mean(hf, axis=-1, keepdims=True)
    var = jnp.mean(jnp.square(hf - mu), axis=-1, keepdims=True)
    hn = (hf - mu) * lax.rsqrt(var + EPS) * ln_w + ln_b
    return jax.nn.silu(hn).astype(u.dtype)


def compress_blocks(kv, pos, w1, b1, w2, b2):
    B, S, G, Dh = kv.shape
    halves = kv.reshape(B, S // CMP_STRIDE, CMP_STRIDE, G, Dh)
    blocks = jnp.concatenate([halves[:, :-1], halves[:, 1:]], axis=2)
    blocks = blocks + pos[None, None, :, None, :]
    flat = blocks.transpose(0, 1, 3, 2, 4).reshape(B, S // CMP_STRIDE - 1, G, CMP_BLOCK * Dh)
    return jax.nn.silu(flat @ w1 + b1) @ w2 + b2


def nsa_attention(q, k_cmp, v_cmp, k_slc, v_slc, k_win, v_win, gates):
    f32 = jnp.float32
    B, S = q.shape[:2]
    G, HPG, DH = NSA_KV_GROUPS, NSA_HPG, NSA_HEAD_DIM
    n_cmp = S // CMP_STRIDE - 1
    n_slc = S // SLC_BLOCK
    n_top = min(SLC_TOPK, n_slc)
    slopes = alibi_slopes(NSA_HEADS).reshape(G, HPG)
    q = q.astype(f32) * DH ** -0.5
    k_cmp = k_cmp.astype(f32)
    v_cmp = v_cmp.astype(f32)
    cmp_end = jnp.arange(n_cmp) * CMP_STRIDE + (CMP_BLOCK - 1)
    ci = jnp.arange(n_cmp)[:, None] * CMP_STRIDE
    sj = jnp.arange(n_slc)[None, :] * SLC_BLOCK
    overlap = ((ci < sj + SLC_BLOCK) & (ci + CMP_BLOCK > sj)).astype(f32)

    def to_blocks(t):
        return t.astype(f32).reshape(B, n_slc, SLC_BLOCK, G, DH).transpose(0, 3, 1, 2, 4).reshape(
            B, G, n_slc, SLC_BLOCK * DH)

    ks_blk, vs_blk = to_blocks(k_slc), to_blocks(v_slc)
    gather = jax.vmap(jax.vmap(lambda blk, ix: blk[ix]))
    pad = ((0, 0), (WINDOW, 0), (0, 0), (0, 0))
    kw_pad = jnp.pad(k_win.astype(f32), pad)
    vw_pad = jnp.pad(v_win.astype(f32), pad)
    nk = n_top * SLC_BLOCK

    def block(qb):
        t0 = qb * Q_BLOCK
        t = t0 + jnp.arange(Q_BLOCK)
        qblk = lax.dynamic_slice_in_dim(q, t0, Q_BLOCK, axis=1)
        gblk = lax.dynamic_slice_in_dim(gates, t0, Q_BLOCK, axis=1)
        dist = (t[:, None] - cmp_end[None, :]).astype(f32)
        s = jnp.einsum('bqghd,bngd->bghqn', qblk, k_cmp) - slopes[None, :, :, None, None] * dist
        p_cmp = masked_softmax(s, dist >= 0)
        o_cmp = jnp.einsum('bghqn,bngd->bqghd', p_cmp, v_cmp)
        imp = jnp.einsum('bgqn,nj->bgqj', jnp.sum(p_cmp, axis=2), overlap)
        cur = (t // SLC_BLOCK)[:, None]
        j = jnp.arange(n_slc)[None, :]
        forced = (j == 0) | (j == cur) | (j == cur - 1)
        imp = jnp.where(forced, FORCE_SCORE, jnp.where(j > cur, -1.0, imp))
        _, idx = lax.top_k(imp, n_top)
        flat_idx = idx.reshape(B, G, Q_BLOCK * n_top)
        k_sel = gather(ks_blk, flat_idx).reshape(B, G, Q_BLOCK, nk, DH)
        v_sel = gather(vs_blk, flat_idx).reshape(B, G, Q_BLOCK, nk, DH)
        pos = (idx[..., None] * SLC_BLOCK + jnp.arange(SLC_BLOCK)).reshape(B, G, Q_BLOCK, nk)
        dist = (t[None, None, :, None] - pos).astype(f32)
        s = jnp.einsum('bqghd,bgqmd->bghqm', qblk, k_sel) - slopes[None, :, :, None, None] * dist[:, :, None]
        p = masked_softmax(s, (dist >= 0)[:, :, None])
        o_slc = jnp.einsum('bghqm,bgqmd->bqghd', p, v_sel)
        kwin = lax.dynamic_slice_in_dim(kw_pad, t0, WINDOW + Q_BLOCK, axis=1)
        vwin = lax.dynamic_slice_in_dim(vw_pad, t0, WINDOW + Q_BLOCK, axis=1)
        kpos = t0 - WINDOW + jnp.arange(WINDOW + Q_BLOCK)
        dist = t[:, None] - kpos[None, :]
        mask = (dist >= 0) & (dist < WINDOW) & (kpos[None, :] >= 0)
        s = jnp.einsum('bqghd,bkgd->bghqk', qblk, kwin) - slopes[None, :, :, None, None] * dist.astype(f32)
        p = masked_softmax(s, mask)
        o_win = jnp.einsum('bghqk,bkgd->bqghd', p, vwin)
        o = (gblk[:, :, 0, :, :, None] * o_cmp + gblk[:, :, 1, :, :, None] * o_slc
             + gblk[:, :, 2, :, :, None] * o_win)
        return o.reshape(B, Q_BLOCK, NSA_Q)

    out = lax.map(block, jnp.arange(S // Q_BLOCK))
    return out.transpose(1, 0, 2, 3).reshape(B, S, NSA_Q)


def even_mixer(h, w_in, conv_w, conv_b, ln_w, ln_b, cmp_pos, cmp_w1, cmp_b1, cmp_w2, cmp_b2, w_out):
    B, S, _ = h.shape
    u = h @ w_in
    a_in, q, kc, vc, ks, vs, kw, vw, g = _split_cols(u, EVEN_SPLITS)
    a_out = conformer_conv(a_in, conv_w, conv_b, ln_w, ln_b)

    def kv(t):
        return t.reshape(B, S, NSA_KV_GROUPS, NSA_HEAD_DIM)

    k_cmp = compress_blocks(kv(kc), cmp_pos[0], cmp_w1[0], cmp_b1[0], cmp_w2[0], cmp_b2[0])
    v_cmp = compress_blocks(kv(vc), cmp_pos[1], cmp_w1[1], cmp_b1[1], cmp_w2[1], cmp_b2[1])
    gates = jax.nn.sigmoid(g.astype(jnp.float32)).reshape(B, S, 3, NSA_KV_GROUPS, NSA_HPG)
    b_out = nsa_attention(q.reshape(B, S, NSA_KV_GROUPS, NSA_HPG, NSA_HEAD_DIM), k_cmp, v_cmp,
                          kv(ks), kv(vs), kv(kw), kv(vw), gates).astype(h.dtype)
    return jnp.concatenate([a_out, b_out], axis=-1) @ w_out


def odd_mixer(h, w_in, lb, gnorm_w, w_out):
    f32 = jnp.float32
    B, S, _ = h.shape
    q, f_logit, i, g = _split_cols(h @ w_in, ODD_SPLITS)
    f = lb + (1.0 - lb) * jax.nn.sigmoid(f_logit.astype(f32))
    log_f = jnp.log(jnp.maximum(f, TINY))
    k = 1.0 - f
    n_chunk = S // HG_CHUNK

    def chunks(t, d):
        return t.astype(f32).reshape(B, n_chunk, HG_CHUNK, HG_HEADS, d).transpose(1, 0, 3, 2, 4)

    xs = (chunks(q, HG_DK), chunks(k, HG_DK), chunks(log_f, HG_DK), chunks(i, HG_DV))
    causal = jnp.tril(jnp.ones((HG_CHUNK, HG_CHUNK), dtype=bool))[:, :, None]

    def step(state, inp):
        qc, kc, gc, ic = inp
        Gc = jnp.cumsum(gc, axis=2)
        o_inter = jnp.einsum('bhtk,bhkv->bhtv', qc * jnp.exp(Gc), state)
        decay = jnp.exp(jnp.where(causal, Gc[:, :, :, None, :] - Gc[:, :, None, :, :], -jnp.inf))
        a = jnp.einsum('bhtk,bhsk,bhtsk->bhts', qc, kc, decay)
        o_intra = jnp.einsum('bhts,bhsv->bhtv', a, ic)
        g_last = Gc[:, :, -1, :]
        state = (jnp.exp(g_last)[..., None] * state
                 + jnp.einsum('bhsk,bhsv->bhkv', kc * jnp.exp(g_last[:, :, None, :] - Gc), ic))
        return state, o_inter + o_intra

    s0 = jnp.zeros((B, HG_HEADS, HG_DK, HG_DV), f32)
    _, o = lax.scan(step, s0, xs)
    o = o.transpose(1, 0, 3, 2, 4).reshape(B, S, HG_HEADS, HG_DV)
    o = o * lax.rsqrt(jnp.mean(o * o, axis=-1, keepdims=True) + EPS) * gnorm_w.astype(f32).reshape(HG_HEADS, HG_DV)
    o = (o.reshape(B, S, HG_HEADS * HG_DV) * jax.nn.silu(g.astype(f32))).astype(h.dtype)
    return o @ w_out


def swiglu(h, w_gu, w_down):
    a, b = jnp.split(h @ w_gu, 2, axis=-1)
    return (jax.nn.silu(a) * b) @ w_down


def setup_inputs(seed: int = 0) -> dict:
    key = jax.random.key(seed)
    ks = jax.random.split(key, 20)
    dh = NSA_HEAD_DIM

    def nrm(k, shape, scale):
        return jax.random.normal(k, shape, jnp.float32) * scale

    return {
        "x": nrm(ks[0], (BATCH, SEQ, D_MODEL), 1.0),
        "norm_w": 1.0 + nrm(ks[1], (DEPTH, 2, D_MODEL), 0.02),
        "final_norm_w": 1.0 + nrm(ks[2], (D_MODEL,), 0.02),
        "ev_w_in": nrm(ks[3], (N_EVEN, D_MODEL, EVEN_IN), D_MODEL ** -0.5),
        "ev_conv_w": nrm(ks[4], (N_EVEN, CONV_WIDTH, 1, CONV_CH), CONV_WIDTH ** -0.5),
        "ev_conv_b": nrm(ks[5], (N_EVEN, CONV_CH), 0.02),
        "ev_conv_ln_w": 1.0 + nrm(ks[6], (N_EVEN, CONV_CH), 0.02),
        "ev_conv_ln_b": nrm(ks[7], (N_EVEN, CONV_CH), 0.02),
        "ev_cmp_pos": nrm(ks[8], (N_EVEN, 2, CMP_BLOCK, dh), 0.1),
        "ev_cmp_w1": nrm(ks[9], (N_EVEN, 2, CMP_BLOCK * dh, CMP_HIDDEN), (CMP_BLOCK * dh) ** -0.5),
        "ev_cmp_b1": nrm(ks[10], (N_EVEN, 2, CMP_HIDDEN), 0.02),
        "ev_cmp_w2": nrm(ks[11], (N_EVEN, 2, CMP_HIDDEN, dh), CMP_HIDDEN ** -0.5),
        "ev_cmp_b2": nrm(ks[12], (N_EVEN, 2, dh), 0.02),
        "ev_w_out": nrm(ks[13], (N_EVEN, MIX_WIDTH, D_MODEL), MIX_WIDTH ** -0.5),
        "od_w_in": nrm(ks[14], (N_ODD, D_MODEL, ODD_IN), D_MODEL ** -0.5),
        "od_lb_gamma": nrm(ks[15], (DEPTH, HG_HEADS * HG_DK), 0.5),
        "od_gnorm_w": 1.0 + nrm(ks[16], (N_ODD, HG_HEADS * HG_DV), 0.02),
        "od_w_out": nrm(ks[17], (N_ODD, HG_HEADS * HG_DV, D_MODEL), (HG_HEADS * HG_DV) ** -0.5),
        "ffn_w_gu": nrm(ks[18], (DEPTH, D_MODEL, 2 * FFN_HIDDEN), D_MODEL ** -0.5),
        "ffn_w_down": nrm(ks[19], (DEPTH, FFN_HIDDEN, D_MODEL), FFN_HIDDEN ** -0.5),
    }


def reference(x, norm_w, final_norm_w, ev_w_in, ev_conv_w, ev_conv_b, ev_conv_ln_w, ev_conv_ln_b,
              ev_cmp_pos, ev_cmp_w1, ev_cmp_b1, ev_cmp_w2, ev_cmp_b2, ev_w_out,
              od_w_in, od_lb_gamma, od_gnorm_w, od_w_out, ffn_w_gu, ffn_w_down):
    lb_all = jnp.cumsum(jax.nn.softmax(od_lb_gamma.astype(jnp.float32), axis=0), axis=0)
    lb_all = lb_all - lb_all[0]
    for layer in range(DEPTH):
        h = rmsnorm(x, norm_w[layer, 0])
        if layer % 2 == 0:
            e = layer // 2
            x = x + even_mixer(h, ev_w_in[e], ev_conv_w[e], ev_conv_b[e], ev_conv_ln_w[e], ev_conv_ln_b[e],
                               ev_cmp_pos[e], ev_cmp_w1[e], ev_cmp_b1[e], ev_cmp_w2[e], ev_cmp_b2[e], ev_w_out[e])
        else:
            o = layer // 2
            x = x + odd_mixer(h, od_w_in[o], lb_all[layer], od_gnorm_w[o], od_w_out[o])
        h = rmsnorm(x, norm_w[layer, 1])
        x = x + swiglu(h, ffn_w_gu[layer], ffn_w_down[layer])
    return rmsnorm(x, final_norm_w)
```

```python
import functools

import jax
import jax.numpy as jnp
import numpy as np
from jax import lax
from jax.experimental import pallas as pl
from jax.experimental.pallas import tpu as pltpu

F32 = jnp.float32
BF16 = jnp.bfloat16

EPS = 1e-6
TINY = 1e-30
NEG = -1e30

VMEM_LIMIT_BYTES = 56 * 1024 * 1024

CONV_WIDTH = 31
NSA_HEADS = 16
NSA_HEAD_DIM = 64
NSA_KV_GROUPS = 4
NSA_HPG = NSA_HEADS // NSA_KV_GROUPS
CMP_STRIDE = 16
CMP_BLOCK = 32
SLC_BLOCK = 64
SLC_TOPK = 16
WINDOW = 512
Q_BLOCK = 128
FORCE_SCORE = 1.0e3
HG_HEADS = 16
HG_DK = 128
HG_DV = 128
HG_CHUNK = 64
HG_SUB = 16


def _params(*sem):
    return pltpu.CompilerParams(dimension_semantics=sem, vmem_limit_bytes=VMEM_LIMIT_BYTES)


def _sigmoid(x):
    return 1.0 / (1.0 + jnp.exp(-x))


def _silu(x):
    return x * _sigmoid(x)


def _dot(a, b):
    return jnp.dot(a, b, preferred_element_type=F32)


def _dot_nt(a, b):
    return lax.dot_general(a, b, (((1,), (1,)), ((), ())), preferred_element_type=F32)


def _dot_tn(a, b):
    return lax.dot_general(a, b, (((0,), (0,)), ((), ())), preferred_element_type=F32)


def _norm_matmul_kernel(x_ref, nw_ref, w_ref, o_ref, h_ref):
    @pl.when(pl.program_id(1) == 0)
    def _():
        x = x_ref[...]
        ms = jnp.mean(x * x, axis=-1, keepdims=True)
        h_ref[...] = (x * lax.rsqrt(ms + EPS) * nw_ref[...]).astype(BF16)

    o_ref[...] = _dot(h_ref[...], w_ref[...]).astype(o_ref.dtype)


def norm_matmul(x, nw, w, *, tm, tn, out_dtype=BF16):
    m, k = x.shape
    n = w.shape[1]
    return pl.pallas_call(
        _norm_matmul_kernel,
        grid=(m // tm, n // tn),
        in_specs=[
            pl.BlockSpec((tm, k), lambda i, j: (i, 0)),
            pl.BlockSpec((1, k), lambda i, j: (0, 0)),
            pl.BlockSpec((k, tn), lambda i, j: (0, j)),
        ],
        out_specs=pl.BlockSpec((tm, tn), lambda i, j: (i, j)),
        out_shape=jax.ShapeDtypeStruct((m, n), out_dtype),
        scratch_shapes=[pltpu.VMEM((tm, k), BF16)],
        compiler_params=_params("parallel", "arbitrary"),
        name="norm_matmul",
    )(x, nw.reshape(1, k), w)


def _matmul_res_kernel(a1_ref, a2_ref, w1_ref, w2_ref, r_ref, o_ref):
    acc = _dot(a1_ref[...], w1_ref[...]) + _dot(a2_ref[...], w2_ref[...])
    o_ref[...] = r_ref[...] + acc


def matmul_residual(a1, a2, blk1, blk2, w, res, *, tm, tn):
    m = res.shape[0]
    k, n = w.shape
    kh = k // 2
    return pl.pallas_call(
        _matmul_res_kernel,
        grid=(m // tm, n // tn),
        in_specs=[
            pl.BlockSpec((tm, kh), lambda i, j: (i, blk1)),
            pl.BlockSpec((tm, kh), lambda i, j: (i, blk2)),
            pl.BlockSpec((kh, tn), lambda i, j: (0, j)),
            pl.BlockSpec((kh, tn), lambda i, j: (1, j)),
            pl.BlockSpec((tm, tn), lambda i, j: (i, j)),
        ],
        out_specs=pl.BlockSpec((tm, tn), lambda i, j: (i, j)),
        out_shape=jax.ShapeDtypeStruct((m, n), F32),
        compiler_params=_params("parallel", "arbitrary"),
        name="matmul_residual",
    )(a1, a2, w, w, res)


def _ffn_kernel(x_ref, nw_ref, wg_ref, wu_ref, wd_ref, fw_ref, o_ref, h_ref, acc_ref, *, final_norm):
    j = pl.program_id(1)

    @pl.when(j == 0)
    def _():
        x = x_ref[...]
        ms = jnp.mean(x * x, axis=-1, keepdims=True)
        h_ref[...] = (x * lax.rsqrt(ms + EPS) * nw_ref[...]).astype(BF16)
        acc_ref[...] = x

    h = h_ref[...]
    a = _dot(h, wg_ref[...])
    b = _dot(h, wu_ref[...])
    z = (_silu(a) * b).astype(BF16)
    acc_ref[...] += _dot(z, wd_ref[...])

    @pl.when(j == pl.num_programs(1) - 1)
    def _():
        y = acc_ref[...]
        if final_norm:
            ms = jnp.mean(y * y, axis=-1, keepdims=True)
            y = y * lax.rsqrt(ms + EPS) * fw_ref[...]
        o_ref[...] = y


def ffn_block(x, nw, w_gu, w_down, final_w, *, tm, th, final_norm):
    m, d = x.shape
    hid = w_down.shape[0]
    nh = hid // th
    return pl.pallas_call(
        functools.partial(_ffn_kernel, final_norm=final_norm),
        grid=(m // tm, nh),
        in_specs=[
            pl.BlockSpec((tm, d), lambda i, j: (i, 0)),
            pl.BlockSpec((1, d), lambda i, j: (0, 0)),
            pl.BlockSpec((d, th), lambda i, j: (0, j)),
            pl.BlockSpec((d, th), lambda i, j: (0, j + nh)),
            pl.BlockSpec((th, d), lambda i, j: (j, 0)),
            pl.BlockSpec((1, d), lambda i, j: (0, 0)),
        ],
        out_specs=pl.BlockSpec((tm, d), lambda i, j: (i, 0)),
        out_shape=jax.ShapeDtypeStruct((m, d), F32),
        scratch_shapes=[pltpu.VMEM((tm, d), BF16), pltpu.VMEM((tm, d), F32)],
        compiler_params=_params("parallel", "arbitrary"),
        name="ffn_block",
    )(x, nw.reshape(1, d), w_gu, w_gu, w_down, final_w.reshape(1, d))


CONV_HALO = 32
CONV_ROWS = 16
CONV_COLS = 256


def _conv_kernel(a_ref, g_ref, ah_ref, gh_ref, cw_ref, cb_ref, lw_ref, lb_ref, o_ref, hs_ref, cv_ref, *,
                 tiles_per_seq):
    ts = a_ref.shape[0]
    first = (pl.program_id(0) % tiles_per_seq) == 0
    hprev = ah_ref[...].astype(F32) * _sigmoid(gh_ref[...].astype(F32))
    hs_ref[0:CONV_HALO, :] = jnp.where(first, 0.0, hprev)
    hs_ref[CONV_HALO:CONV_HALO + ts, :] = a_ref[...].astype(F32) * _sigmoid(g_ref[...].astype(F32))
    off = CONV_HALO - (CONV_WIDTH - 1)

    def body(r, carry):
        base = pl.multiple_of(r * CONV_ROWS, CONV_ROWS)
        for c0 in range(0, hs_ref.shape[1], CONV_COLS):
            cols = slice(c0, c0 + CONV_COLS)
            win = hs_ref[pl.ds(base, CONV_ROWS + CONV_HALO), cols]
            acc = jnp.zeros((CONV_ROWS, CONV_COLS), F32) + cb_ref[:, cols]
            for w in range(CONV_WIDTH):
                acc = acc + win[off + w:off + w + CONV_ROWS, :] * cw_ref[w:w + 1, cols]
            cv_ref[:, cols] = acc
        acc = cv_ref[...]
        mu = jnp.mean(acc, axis=-1, keepdims=True)
        d = acc - mu
        var = jnp.mean(d * d, axis=-1, keepdims=True)
        hn = d * lax.rsqrt(var + EPS) * lw_ref[...] + lb_ref[...]
        o_ref[pl.ds(base, CONV_ROWS), :] = _silu(hn).astype(o_ref.dtype)
        return carry

    lax.fori_loop(0, ts // CONV_ROWS, body, 0)


def conformer_conv(u, conv_w, conv_b, ln_w, ln_b, *, seq, ts):
    t = u.shape[0]
    c = conv_w.shape[1]
    hb = ts // CONV_HALO
    return pl.pallas_call(
        functools.partial(_conv_kernel, tiles_per_seq=seq // ts),
        grid=(t // ts,),
        in_specs=[
            pl.BlockSpec((ts, c), lambda i: (i, 0)),
            pl.BlockSpec((ts, c), lambda i: (i, 1)),
            pl.BlockSpec((CONV_HALO, c), lambda i: (jnp.maximum(i * hb - 1, 0), 0)),
            pl.BlockSpec((CONV_HALO, c), lambda i: (jnp.maximum(i * hb - 1, 0), 1)),
            pl.BlockSpec((CONV_WIDTH, c), lambda i: (0, 0)),
            pl.BlockSpec((1, c), lambda i: (0, 0)),
            pl.BlockSpec((1, c), lambda i: (0, 0)),
            pl.BlockSpec((1, c), lambda i: (0, 0)),
        ],
        out_specs=pl.BlockSpec((ts, c), lambda i: (i, 0)),
        out_shape=jax.ShapeDtypeStruct((t, c), BF16),
        scratch_shapes=[pltpu.VMEM((CONV_HALO + ts, c), F32), pltpu.VMEM((CONV_ROWS, c), F32)],
        compiler_params=_params("parallel"),
        name="conformer_conv",
    )(u, u, u, u, conv_w, conv_b.reshape(1, c), ln_w.reshape(1, c), ln_b.reshape(1, c))


def _compress_kernel(x_ref, pos_ref, w1_ref, b1_ref, w2_ref, b2_ref, o_ref, sh_ref):
    n = x_ref.shape[0]
    hw = x_ref.shape[1]
    x = x_ref[...].astype(F32)
    xa = (x + pos_ref[:, 0:hw]).astype(BF16)
    xb = (x + pos_ref[:, hw:2 * hw]).astype(BF16)
    p1 = _dot(xa, w1_ref[0:hw, :])
    sh_ref[0:n, :] = _dot(xb, w1_ref[hw:2 * hw, :])
    sh_ref[n:n + 8, :] = jnp.zeros((8, sh_ref.shape[1]), F32)
    hid = p1 + sh_ref[1:n + 1, :] + b1_ref[...]
    o_ref[...] = _dot(_silu(hid).astype(BF16), w2_ref[...]) + b2_ref[...]


def compress_tokens(xh, pos, w1, b1, w2, b2):
    _, bg, n, hw = xh.shape
    hid = w1.shape[-1]
    dh = w2.shape[-1]
    return pl.pallas_call(
        _compress_kernel,
        grid=(2, bg),
        in_specs=[
            pl.BlockSpec((None, None, n, hw), lambda s, i: (s, i, 0, 0)),
            pl.BlockSpec((None, 1, 2 * hw), lambda s, i: (s, 0, 0)),
            pl.BlockSpec((None, 2 * hw, hid), lambda s, i: (s, 0, 0)),
            pl.BlockSpec((None, 1, hid), lambda s, i: (s, 0, 0)),
            pl.BlockSpec((None, hid, dh), lambda s, i: (s, 0, 0)),
            pl.BlockSpec((None, 1, dh), lambda s, i: (s, 0, 0)),
        ],
        out_specs=pl.BlockSpec((None, None, n, dh), lambda s, i: (s, i, 0, 0)),
        out_shape=jax.ShapeDtypeStruct((2, bg, n, dh), F32),
        scratch_shapes=[pltpu.VMEM((n + 8, hid), F32)],
        compiler_params=_params("parallel", "parallel"),
        name="compress_tokens",
    )(xh, pos.reshape(2, 1, 2 * hw), w1.astype(BF16), b1.reshape(2, 1, hid), w2.astype(BF16),
      b2.reshape(2, 1, dh))


def _split_bf16(x):
    hi = x.astype(BF16)
    lo = (x - hi.astype(F32)).astype(BF16)
    return hi, lo


def _hgrn_kernel(q_ref, f_ref, i_ref, g_ref, gamma_ref, gn_ref, o_ref, st_ref, *, layer):
    c, sub = HG_CHUNK, HG_SUB
    nsub = c // sub
    tt = q_ref.shape[0]

    @pl.when(pl.program_id(2) == 0)
    def _():
        st_ref[...] = jnp.zeros_like(st_ref)

    gamma = gamma_ref[...]
    e = jnp.exp(gamma - jnp.max(gamma, axis=0, keepdims=True))
    sm = e / jnp.sum(e, axis=0, keepdims=True)
    lb = jnp.sum(sm[0:layer + 1, :], axis=0, keepdims=True) - sm[0:1, :]
    row = lax.broadcasted_iota(jnp.int32, (c, c), 0)
    col = lax.broadcasted_iota(jnp.int32, (c, c), 1)
    tril = jnp.where(row >= col, 1.0, 0.0).astype(BF16)
    trow = lax.broadcasted_iota(jnp.int32, (1, sub, 1), 1)

    def chunk(ci, carry):
        r0 = pl.multiple_of(ci * c, c)
        q = q_ref[pl.ds(r0, c), :].astype(F32)
        fl = f_ref[pl.ds(r0, c), :].astype(F32)
        iv = i_ref[pl.ds(r0, c), :].astype(F32)
        gg = g_ref[pl.ds(r0, c), :].astype(F32)
        f = lb + (1.0 - lb) * _sigmoid(fl)
        logf = jnp.log(jnp.maximum(f, TINY))
        k = 1.0 - f
        hi, lo = _split_bf16(logf)
        gc = _dot(tril, hi) + _dot(tril, lo)
        glast = gc[c - 1:c, :]
        st = st_ref[...]
        ivb = iv.astype(BF16)
        o = _dot_nt((q * jnp.exp(gc)).astype(BF16), st.astype(BF16))
        kdec = k * jnp.exp(glast - gc)
        st_ref[...] = jnp.exp(glast) * st + _dot_tn(ivb, kdec.astype(BF16))

        parts = [jnp.zeros((sub, o.shape[1]), F32)]
        for si in range(1, nsub):
            lo_r, hi_r = si * sub, (si + 1) * sub
            ref = gc[lo_r - 1:lo_r, :]
            qhat = q[lo_r:hi_r, :] * jnp.exp(gc[lo_r:hi_r, :] - ref)
            khat = k[0:lo_r, :] * jnp.exp(ref - gc[0:lo_r, :])
            a = _dot_nt(qhat.astype(BF16), khat.astype(BF16))
            parts.append(_dot(a.astype(BF16), ivb[0:lo_r, :]))
        o = o + jnp.concatenate(parts, axis=0)

        q3 = q.reshape(nsub, sub, -1)
        k3 = k.reshape(nsub, sub, -1)
        g3 = gc.reshape(nsub, sub, -1)
        i3 = iv.reshape(nsub, sub, -1)
        od = jnp.zeros_like(q3)
        for s in range(sub):
            e = jnp.where(trow >= s, g3 - g3[:, s:s + 1, :], NEG)
            a = jnp.sum(q3 * k3[:, s:s + 1, :] * jnp.exp(e), axis=-1, keepdims=True)
            od = od + a * i3[:, s:s + 1, :]
        o = o + od.reshape(c, -1)

        o = o * lax.rsqrt(jnp.mean(o * o, axis=-1, keepdims=True) + EPS) * gn_ref[...]
        o_ref[pl.ds(r0, c), :] = (o * _silu(gg)).astype(o_ref.dtype)
        return carry

    lax.fori_loop(0, tt // c, chunk, 0)


def hgrn2(u, lb_gamma, gnorm_w, *, layer, batch, seq, tt):
    h, dk, dv = HG_HEADS, HG_DK, HG_DV
    nt = seq // tt
    depth = lb_gamma.shape[0]

    def col(off):
        return pl.BlockSpec((tt, dk), lambda b, hh, t: (b * nt + t, off + hh))

    return pl.pallas_call(
        functools.partial(_hgrn_kernel, layer=layer),
        grid=(batch, h, nt),
        in_specs=[
            col(0), col(h), col(2 * h), col(3 * h),
            pl.BlockSpec((depth, dk), lambda b, hh, t: (0, hh)),
            pl.BlockSpec((1, dv), lambda b, hh, t: (0, hh)),
        ],
        out_specs=pl.BlockSpec((tt, dv), lambda b, hh, t: (b * nt + t, hh)),
        out_shape=jax.ShapeDtypeStruct((batch * seq, h * dv), BF16),
        scratch_shapes=[pltpu.VMEM((dv, dk), F32)],
        compiler_params=_params("parallel", "parallel", "arbitrary"),
        name="hgrn2",
    )(u, u, u, u, lb_gamma, gnorm_w.reshape(1, h * dv))


QL = NSA_HPG * Q_BLOCK
FLAG_BITS = 16


def _tile_positions(qb):
    lane = lax.broadcasted_iota(jnp.int32, (1, QL), 1)
    return qb * Q_BLOCK + (lane & (Q_BLOCK - 1))


def _nsa_select_kernel(q_ref, kc_ref, vct_ref, slope_ref, oc_ref, sel_ref, flag_ref, p_ref):
    nc = kc_ref.shape[0]
    ns = sel_ref.shape[0]
    qb = pl.program_id(2)
    q = (q_ref[...].astype(F32) * NSA_HEAD_DIM ** -0.5).astype(BF16)
    t = _tile_positions(qb)
    slope = slope_ref[...]

    s = _dot(kc_ref[...], q)
    n = lax.broadcasted_iota(jnp.int32, (nc, 1), 0)
    dist = (t - (n * CMP_STRIDE + (CMP_BLOCK - 1))).astype(F32)
    valid = dist >= 0
    s = jnp.where(valid, s - slope * dist, NEG)
    m = jnp.max(s, axis=0, keepdims=True)
    p = jnp.where(valid, jnp.exp(s - m), 0.0)
    inv = 1.0 / jnp.maximum(jnp.sum(p, axis=0, keepdims=True), TINY)
    oc_ref[...] = _dot(vct_ref[...], p.astype(BF16)) * inv
    pn = p * inv
    psum = pn[:, 0:Q_BLOCK]
    for h in range(1, NSA_HPG):
        psum = psum + pn[:, h * Q_BLOCK:(h + 1) * Q_BLOCK]

    p_ref[0:8, :] = jnp.zeros((8, Q_BLOCK), F32)
    p_ref[8:8 + nc, :] = psum
    p_ref[8 + nc:16 + nc, :] = jnp.zeros((8, Q_BLOCK), F32)
    ratio = SLC_BLOCK // CMP_STRIDE
    imp = p_ref[pl.ds(7, ns, stride=ratio), :]
    for r in range(1, ratio + 1):
        imp = imp + p_ref[pl.ds(7 + r, ns, stride=ratio), :]

    j = lax.broadcasted_iota(jnp.int32, (ns, 1), 0)
    cur = t[:, 0:Q_BLOCK] // SLC_BLOCK
    forced = (j == 0) | (j == cur) | (j == cur - 1)
    imp = jnp.where(forced, FORCE_SCORE, jnp.where(j > cur, -1.0, imp))
    sel = jnp.zeros((ns, Q_BLOCK), F32)
    for _ in range(min(SLC_TOPK, ns)):
        mx = jnp.max(imp, axis=0, keepdims=True)
        idx = jnp.min(jnp.where(imp == mx, j, ns), axis=0, keepdims=True)
        hit = j == idx
        sel = jnp.where(hit, 1.0, sel)
        imp = jnp.where(hit, NEG, imp)
    sel = jnp.where(j <= cur, sel, 0.0)
    sel_ref[...] = sel

    cnt = _dot_nt(jnp.ones((8, Q_BLOCK), BF16), sel.astype(BF16))
    flags = jnp.where(cnt > 0.0, 1.0, 0.0).astype(BF16)
    jj = lax.broadcasted_iota(jnp.int32, (ns, 128), 0)
    ww = lax.broadcasted_iota(jnp.int32, (ns, 128), 1)
    bit = jnp.left_shift(1, jj & (FLAG_BITS - 1)).astype(F32)
    pack = jnp.where(ww == jj // FLAG_BITS, bit, 0.0).astype(BF16)
    flag_ref[...] = _dot(flags, pack)


def nsa_select(qt, kc, vct, slopes):
    b, g, nqb, dh, ql = qt.shape
    nc = kc.shape[2]
    ns = nc * CMP_STRIDE // SLC_BLOCK
    tile = lambda *shape: pl.BlockSpec((None, None, None) + shape, lambda bi, gi, qi: (bi, gi, qi, 0, 0))
    return pl.pallas_call(
        _nsa_select_kernel,
        grid=(b, g, nqb),
        in_specs=[
            tile(dh, ql),
            pl.BlockSpec((None, None, nc, dh), lambda bi, gi, qi: (bi, gi, 0, 0)),
            pl.BlockSpec((None, None, dh, nc), lambda bi, gi, qi: (bi, gi, 0, 0)),
            pl.BlockSpec((None, 1, ql), lambda bi, gi, qi: (gi, 0, 0)),
        ],
        out_specs=[tile(dh, ql), tile(ns, Q_BLOCK), tile(8, 128)],
        out_shape=[
            jax.ShapeDtypeStruct((b, g, nqb, dh, ql), F32),
            jax.ShapeDtypeStruct((b, g, nqb, ns, Q_BLOCK), F32),
            jax.ShapeDtypeStruct((b, g, nqb, 8, 128), F32),
        ],
        scratch_shapes=[pltpu.VMEM((nc + 16, Q_BLOCK), F32)],
        compiler_params=_params("parallel", "parallel", "parallel"),
        name="nsa_select",
    )(qt, kc, vct, slopes)


def _attend_block(k_ref, vt_ref, j, q, t, slope, extra_valid, m_ref, l_ref, acc_ref):
    s = _dot(k_ref[j], q)
    kpos = j * SLC_BLOCK + lax.broadcasted_iota(jnp.int32, (SLC_BLOCK, 1), 0)
    dist = t - kpos
    valid = extra_valid(dist)
    s = jnp.where(valid, s - slope * dist.astype(F32), NEG)
    m_old = m_ref[...]
    m_new = jnp.maximum(m_old, jnp.max(s, axis=0, keepdims=True))
    alpha = jnp.exp(m_old - m_new)
    p = jnp.where(valid, jnp.exp(s - m_new), 0.0)
    m_ref[...] = m_new
    l_ref[...] = alpha * l_ref[...] + jnp.sum(p, axis=0, keepdims=True)
    acc_ref[...] = alpha * acc_ref[...] + _dot(vt_ref[j], p.astype(BF16))


def _nsa_attend_kernel(fw_ref, q_ref, ks_ref, vst_ref, kw_ref, vwt_ref, sel_ref, oc_ref, gate_ref, slope_ref,
                       o_ref, m_ref, l_ref, acc_ref, *, words_per_tile):
    bi, gi, qb = pl.program_id(0), pl.program_id(1), pl.program_id(2)
    tile_id = (bi * pl.num_programs(1) + gi) * pl.num_programs(2) + qb
    q = (q_ref[...].astype(F32) * NSA_HEAD_DIM ** -0.5).astype(BF16)
    t = _tile_positions(qb)
    slope = slope_ref[...]
    last_blk = (qb + 1) * (Q_BLOCK // SLC_BLOCK)

    def reset():
        m_ref[...] = jnp.full_like(m_ref, NEG)
        l_ref[...] = jnp.zeros_like(l_ref)
        acc_ref[...] = jnp.zeros_like(acc_ref)

    def result():
        return acc_ref[...] * (1.0 / jnp.maximum(l_ref[...], TINY))

    reset()

    def slc_body(j, carry):
        word = fw_ref[tile_id * words_per_tile + j // FLAG_BITS]

        @pl.when(((word >> (j % FLAG_BITS)) & 1) == 1)
        def _():
            row = sel_ref[pl.ds(j, 1), :]
            chosen = jnp.concatenate([row] * NSA_HPG, axis=1) > 0.0
            _attend_block(ks_ref, vst_ref, j, q, t, slope, lambda d: (d >= 0) & chosen, m_ref, l_ref, acc_ref)

        return carry

    lax.fori_loop(0, last_blk, slc_body, 0)
    o_slc = result()

    reset()

    def win_body(j, carry):
        _attend_block(kw_ref, vwt_ref, j, q, t, slope, lambda d: (d >= 0) & (d < WINDOW), m_ref, l_ref, acc_ref)
        return carry

    lax.fori_loop(jnp.maximum(last_blk - (WINDOW + Q_BLOCK) // SLC_BLOCK, 0), last_blk, win_body, 0)
    o_win = result()

    gate = _sigmoid(gate_ref[...])
    o_ref[...] = (gate[0:1, :] * oc_ref[...] + gate[1:2, :] * o_slc + gate[2:3, :] * o_win).astype(o_ref.dtype)


def nsa_attend(flag_words, qt, ks, vst, kw, vwt, sel, oc, gates, slopes):
    b, g, nqb, dh, ql = qt.shape
    ns = ks.shape[2]
    tile = lambda *shape: pl.BlockSpec((None, None, None) + shape, lambda bi, gi, qi, fw: (bi, gi, qi, 0, 0))
    seq = lambda *shape: pl.BlockSpec((None, None) + shape, lambda bi, gi, qi, fw: (bi, gi, 0, 0, 0))
    grid_spec = pltpu.PrefetchScalarGridSpec(
        num_scalar_prefetch=1,
        grid=(b, g, nqb),
        in_specs=[
            tile(dh, ql),
            seq(ns, SLC_BLOCK, dh), seq(ns, dh, SLC_BLOCK), seq(ns, SLC_BLOCK, dh), seq(ns, dh, SLC_BLOCK),
            tile(ns, Q_BLOCK), tile(dh, ql), tile(3, ql),
            pl.BlockSpec((None, 1, ql), lambda bi, gi, qi, fw: (gi, 0, 0)),
        ],
        out_specs=tile(dh, ql),
        scratch_shapes=[pltpu.VMEM((1, ql), F32), pltpu.VMEM((1, ql), F32), pltpu.VMEM((dh, ql), F32)],
    )
    return pl.pallas_call(
        functools.partial(_nsa_attend_kernel, words_per_tile=ns // FLAG_BITS),
        grid_spec=grid_spec,
        out_shape=jax.ShapeDtypeStruct((b, g, nqb, dh, ql), BF16),
        compiler_params=_params("parallel", "parallel", "arbitrary"),
        name="nsa_attend",
    )(flag_words, qt, ks, vst, kw, vwt, sel, oc, gates, slopes)


def nsa_mixer(q, kc, vc, ks, vs, kw, vw, gate_logits, cmp_pos, cmp_w1, cmp_b1, cmp_w2, cmp_b2, *, batch, seq):
    g, hpg, dh = NSA_KV_GROUPS, NSA_HPG, NSA_HEAD_DIM
    nqb, ns, nc = seq // Q_BLOCK, seq // SLC_BLOCK, seq // CMP_STRIDE

    qt = q.reshape(batch, nqb, Q_BLOCK, g, hpg, dh).transpose(0, 3, 1, 5, 4, 2).reshape(batch, g, nqb, dh, QL)
    gates = gate_logits.astype(F32).reshape(batch, nqb, Q_BLOCK, 3, g, hpg).transpose(0, 4, 1, 3, 5, 2)
    gates = gates.reshape(batch, g, nqb, 3, QL)
    slopes = 2.0 ** (-8.0 * jnp.arange(1, NSA_HEADS + 1, dtype=F32) / NSA_HEADS)
    slopes = jnp.repeat(slopes.reshape(g, 1, hpg), Q_BLOCK, axis=2)

    def halves(x):
        x = x.reshape(batch, nc, CMP_STRIDE, g, dh).transpose(0, 3, 1, 2, 4)
        return x.reshape(batch * g, nc, CMP_STRIDE * dh)

    def key_blocks(x):
        return x.reshape(batch, ns, SLC_BLOCK, g, dh).transpose(0, 3, 1, 2, 4)

    def value_blocks(x):
        return x.reshape(batch, ns, SLC_BLOCK, g, dh).transpose(0, 3, 1, 4, 2)

    cmp = compress_tokens(jnp.stack([halves(kc), halves(vc)]), cmp_pos, cmp_w1, cmp_b1, cmp_w2, cmp_b2)
    cmp = cmp.astype(BF16).reshape(2, batch, g, nc, dh)
    k_cmp, v_cmp_t = cmp[0], cmp[1].transpose(0, 1, 3, 2)

    oc, sel, flags = nsa_select(qt, k_cmp, v_cmp_t, slopes)
    flag_words = flags[:, :, :, 0, :ns // FLAG_BITS].astype(jnp.int32).reshape(-1)
    ot = nsa_attend(flag_words, qt, key_blocks(ks), value_blocks(vs), key_blocks(kw), value_blocks(vw),
                    sel, oc, gates, slopes)
    ot = ot.reshape(batch, g, nqb, dh, hpg, Q_BLOCK).transpose(0, 2, 5, 1, 4, 3)
    return ot.reshape(batch * seq, g * hpg * dh)


ROW_TILE = 1024
COL_TILE = 1024
FFN_ROW_TILE = 512
FFN_HID_TILE = 512
CONV_SEQ_TILE = 512
HGRN_SEQ_TILE = 512


def _even_layer(x, nw, w_in, conv_w, conv_b, ln_w, ln_b, cmp_pos, cmp_w1, cmp_b1, cmp_w2, cmp_b2, w_out, *,
                batch, seq):
    c = conv_w.shape[-1]
    nq = NSA_HEADS * NSA_HEAD_DIM
    nkv = NSA_KV_GROUPS * NSA_HEAD_DIM
    n_in = w_in.shape[1]
    n_pad = -(-n_in // COL_TILE) * COL_TILE
    w_in = jnp.pad(w_in, ((0, 0), (0, n_pad - n_in))).astype(BF16)
    u = norm_matmul(x, nw, w_in, tm=ROW_TILE, tn=COL_TILE)
    a_out = conformer_conv(u, conv_w.reshape(CONV_WIDTH, c), conv_b, ln_w, ln_b, seq=seq, ts=CONV_SEQ_TILE)
    off = 2 * c
    q = u[:, off:off + nq]
    off += nq
    kvs = [u[:, off + i * nkv:off + (i + 1) * nkv] for i in range(6)]
    off += 6 * nkv
    gate_logits = u[:, off:off + 3 * NSA_HEADS]
    b_out = nsa_mixer(q, *kvs, gate_logits, cmp_pos, cmp_w1, cmp_b1, cmp_w2, cmp_b2, batch=batch, seq=seq)
    return matmul_residual(a_out, b_out, 0, 0, w_out.astype(BF16), x, tm=ROW_TILE, tn=COL_TILE)


def _odd_layer(x, nw, w_in, lb_gamma, gnorm_w, w_out, *, layer, batch, seq):
    u = norm_matmul(x, nw, w_in.astype(BF16), tm=ROW_TILE, tn=COL_TILE)
    o = hgrn2(u, lb_gamma, gnorm_w, layer=layer, batch=batch, seq=seq, tt=HGRN_SEQ_TILE)
    return matmul_residual(o, o, 0, 1, w_out.astype(BF16), x, tm=ROW_TILE, tn=COL_TILE)


def kernel(x, norm_w, final_norm_w, ev_w_in, ev_conv_w, ev_conv_b, ev_conv_ln_w, ev_conv_ln_b, ev_cmp_pos,
           ev_cmp_w1, ev_cmp_b1, ev_cmp_w2, ev_cmp_b2, ev_w_out, od_w_in, od_lb_gamma, od_gnorm_w, od_w_out,
           ffn_w_gu, ffn_w_down):
    batch, seq, d = x.shape
    depth = norm_w.shape[0]
    xs = x.reshape(batch * seq, d)
    for layer in range(depth):
        i = layer // 2
        if layer % 2 == 0:
            xs = _even_layer(xs, norm_w[layer, 0], ev_w_in[i], ev_conv_w[i], ev_conv_b[i], ev_conv_ln_w[i],
                             ev_conv_ln_b[i], ev_cmp_pos[i], ev_cmp_w1[i], ev_cmp_b1[i], ev_cmp_w2[i],
                             ev_cmp_b2[i], ev_w_out[i], batch=batch, seq=seq)
        else:
            xs = _odd_layer(xs, norm_w[layer, 0], od_w_in[i], od_lb_gamma.astype(F32), od_gnorm_w[i],
                            od_w_out[i], layer=layer, batch=batch, seq=seq)
        xs = ffn_block(xs, norm_w[layer, 1], ffn_w_gu[layer].astype(BF16), ffn_w_down[layer].astype(BF16),
                       final_norm_w, tm=FFN_ROW_TILE, th=FFN_HID_TILE, final_norm=layer == depth - 1)
    return xs.reshape(batch, seq, d)
```

```python
import functools

import jax
import jax.numpy as jnp
import numpy as np
from jax import lax
from jax.experimental import pallas as pl
from jax.experimental.pallas import tpu as pltpu

F32 = jnp.float32
BF16 = jnp.bfloat16

EPS = 1e-6
TINY = 1e-30
NEG = -1e30

VMEM_LIMIT_BYTES = 56 * 1024 * 1024

CONV_WIDTH = 31
NSA_HEADS = 16
NSA_HEAD_DIM = 64
NSA_KV_GROUPS = 4
NSA_HPG = NSA_HEADS // NSA_KV_GROUPS
CMP_STRIDE = 16
CMP_BLOCK = 32
SLC_BLOCK = 64
SLC_TOPK = 16
WINDOW = 512
Q_BLOCK = 128
FORCE_SCORE = 1.0e3
HG_HEADS = 16
HG_DK = 128
HG_DV = 128
HG_CHUNK = 64
HG_SUB = 16
HG_UNROLL = 4


def _params(*sem):
    return pltpu.CompilerParams(dimension_semantics=sem, vmem_limit_bytes=VMEM_LIMIT_BYTES)


def _sigmoid(x):
    return 1.0 / (1.0 + jnp.exp(-x))


def _silu(x):
    return x * _sigmoid(x)


def _dot(a, b):
    return jnp.dot(a, b, preferred_element_type=F32)


def _dot_nt(a, b):
    return lax.dot_general(a, b, (((1,), (1,)), ((), ())), preferred_element_type=F32)


def _dot_tn(a, b):
    return lax.dot_general(a, b, (((0,), (0,)), ((), ())), preferred_element_type=F32)


def _norm_matmul_kernel(x_ref, nw_ref, w_ref, o_ref, h_ref):
    @pl.when(pl.program_id(1) == 0)
    def _():
        x = x_ref[...]
        ms = jnp.mean(x * x, axis=-1, keepdims=True)
        h_ref[...] = (x * lax.rsqrt(ms + EPS) * nw_ref[...]).astype(BF16)

    o_ref[...] = _dot(h_ref[...], w_ref[...]).astype(o_ref.dtype)


def norm_matmul(x, nw, w, *, tm, tn, out_dtype=BF16):
    m, k = x.shape
    n = w.shape[1]
    return pl.pallas_call(
        _norm_matmul_kernel,
        grid=(m // tm, n // tn),
        in_specs=[
            pl.BlockSpec((tm, k), lambda i, j: (i, 0)),
            pl.BlockSpec((1, k), lambda i, j: (0, 0)),
            pl.BlockSpec((k, tn), lambda i, j: (0, j)),
        ],
        out_specs=pl.BlockSpec((tm, tn), lambda i, j: (i, j)),
        out_shape=jax.ShapeDtypeStruct((m, n), out_dtype),
        scratch_shapes=[pltpu.VMEM((tm, k), BF16)],
        compiler_params=_params("parallel", "arbitrary"),
        name="norm_matmul",
    )(x, nw.reshape(1, k), w)


def _matmul_res_kernel(a1_ref, a2_ref, w1_ref, w2_ref, r_ref, o_ref):
    acc = _dot(a1_ref[...], w1_ref[...]) + _dot(a2_ref[...], w2_ref[...])
    o_ref[...] = r_ref[...] + acc


def matmul_residual(a1, a2, blk1, blk2, w, res, *, tm, tn):
    m = res.shape[0]
    k, n = w.shape
    kh = k // 2
    return pl.pallas_call(
        _matmul_res_kernel,
        grid=(m // tm, n // tn),
        in_specs=[
            pl.BlockSpec((tm, kh), lambda i, j: (i, blk1)),
            pl.BlockSpec((tm, kh), lambda i, j: (i, blk2)),
            pl.BlockSpec((kh, tn), lambda i, j: (0, j)),
            pl.BlockSpec((kh, tn), lambda i, j: (1, j)),
            pl.BlockSpec((tm, tn), lambda i, j: (i, j)),
        ],
        out_specs=pl.BlockSpec((tm, tn), lambda i, j: (i, j)),
        out_shape=jax.ShapeDtypeStruct((m, n), F32),
        compiler_params=_params("parallel", "arbitrary"),
        name="matmul_residual",
    )(a1, a2, w, w, res)


def _ffn_kernel(x_ref, nw_ref, wg_ref, wu_ref, wd_ref, fw_ref, o_ref, h_ref, acc_ref, *, final_norm):
    j = pl.program_id(1)

    @pl.when(j == 0)
    def _():
        x = x_ref[...]
        ms = jnp.mean(x * x, axis=-1, keepdims=True)
        h_ref[...] = (x * lax.rsqrt(ms + EPS) * nw_ref[...]).astype(BF16)
        acc_ref[...] = x

    h = h_ref[...]
    a = _dot(h, wg_ref[...])
    b = _dot(h, wu_ref[...])
    z = (_silu(a) * b).astype(BF16)
    acc_ref[...] += _dot(z, wd_ref[...])

    @pl.when(j == pl.num_programs(1) - 1)
    def _():
        y = acc_ref[...]
        if final_norm:
            ms = jnp.mean(y * y, axis=-1, keepdims=True)
            y = y * lax.rsqrt(ms + EPS) * fw_ref[...]
        o_ref[...] = y


def ffn_block(x, nw, w_gu, w_down, final_w, *, tm, th, final_norm):
    m, d = x.shape
    hid = w_down.shape[0]
    nh = hid // th
    return pl.pallas_call(
        functools.partial(_ffn_kernel, final_norm=final_norm),
        grid=(m // tm, nh),
        in_specs=[
            pl.BlockSpec((tm, d), lambda i, j: (i, 0)),
            pl.BlockSpec((1, d), lambda i, j: (0, 0)),
            pl.BlockSpec((d, th), lambda i, j: (0, j)),
            pl.BlockSpec((d, th), lambda i, j: (0, j + nh)),
            pl.BlockSpec((th, d), lambda i, j: (j, 0)),
            pl.BlockSpec((1, d), lambda i, j: (0, 0)),
        ],
        out_specs=pl.BlockSpec((tm, d), lambda i, j: (i, 0)),
        out_shape=jax.ShapeDtypeStruct((m, d), F32),
        scratch_shapes=[pltpu.VMEM((tm, d), BF16), pltpu.VMEM((tm, d), F32)],
        compiler_params=_params("parallel", "arbitrary"),
        name="ffn_block",
    )(x, nw.reshape(1, d), w_gu, w_gu, w_down, final_w.reshape(1, d))


CONV_HALO = 32
CONV_ROWS = 16
CONV_COLS = 256


def _conv_kernel(a_ref, g_ref, ah_ref, gh_ref, cw_ref, cb_ref, lw_ref, lb_ref, o_ref, hs_ref, cv_ref, *,
                 tiles_per_seq):
    ts = a_ref.shape[0]
    first = (pl.program_id(0) % tiles_per_seq) == 0
    hprev = ah_ref[...].astype(F32) * _sigmoid(gh_ref[...].astype(F32))
    hs_ref[0:CONV_HALO, :] = jnp.where(first, 0.0, hprev)
    hs_ref[CONV_HALO:CONV_HALO + ts, :] = a_ref[...].astype(F32) * _sigmoid(g_ref[...].astype(F32))
    off = CONV_HALO - (CONV_WIDTH - 1)

    def body(r, carry):
        base = pl.multiple_of(r * CONV_ROWS, CONV_ROWS)
        for c0 in range(0, hs_ref.shape[1], CONV_COLS):
            cols = slice(c0, c0 + CONV_COLS)
            win = hs_ref[pl.ds(base, CONV_ROWS + CONV_HALO), cols]
            acc = jnp.zeros((CONV_ROWS, CONV_COLS), F32) + cb_ref[:, cols]
            for w in range(CONV_WIDTH):
                acc = acc + win[off + w:off + w + CONV_ROWS, :] * cw_ref[w:w + 1, cols]
            cv_ref[:, cols] = acc
        acc = cv_ref[...]
        mu = jnp.mean(acc, axis=-1, keepdims=True)
        d = acc - mu
        var = jnp.mean(d * d, axis=-1, keepdims=True)
        hn = d * lax.rsqrt(var + EPS) * lw_ref[...] + lb_ref[...]
        o_ref[pl.ds(base, CONV_ROWS), :] = _silu(hn).astype(o_ref.dtype)
        return carry

    lax.fori_loop(0, ts // CONV_ROWS, body, 0)


def conformer_conv(u, conv_w, conv_b, ln_w, ln_b, *, seq, ts):
    t = u.shape[0]
    c = conv_w.shape[1]
    hb = ts // CONV_HALO
    return pl.pallas_call(
        functools.partial(_conv_kernel, tiles_per_seq=seq // ts),
        grid=(t // ts,),
        in_specs=[
            pl.BlockSpec((ts, c), lambda i: (i, 0)),
            pl.BlockSpec((ts, c), lambda i: (i, 1)),
            pl.BlockSpec((CONV_HALO, c), lambda i: (jnp.maximum(i * hb - 1, 0), 0)),
            pl.BlockSpec((CONV_HALO, c), lambda i: (jnp.maximum(i * hb - 1, 0), 1)),
            pl.BlockSpec((CONV_WIDTH, c), lambda i: (0, 0)),
            pl.BlockSpec((1, c), lambda i: (0, 0)),
            pl.BlockSpec((1, c), lambda i: (0, 0)),
            pl.BlockSpec((1, c), lambda i: (0, 0)),
        ],
        out_specs=pl.BlockSpec((ts, c), lambda i: (i, 0)),
        out_shape=jax.ShapeDtypeStruct((t, c), BF16),
        scratch_shapes=[pltpu.VMEM((CONV_HALO + ts, c), F32), pltpu.VMEM((CONV_ROWS, c), F32)],
        compiler_params=_params("parallel"),
        name="conformer_conv",
    )(u, u, u, u, conv_w, conv_b.reshape(1, c), ln_w.reshape(1, c), ln_b.reshape(1, c))


def _compress_kernel(x_ref, pos_ref, w1_ref, b1_ref, w2_ref, b2_ref, o_ref, sh_ref):
    n = x_ref.shape[0]
    hw = x_ref.shape[1]
    x = x_ref[...].astype(F32)
    xa = (x + pos_ref[:, 0:hw]).astype(BF16)
    xb = (x + pos_ref[:, hw:2 * hw]).astype(BF16)
    p1 = _dot(xa, w1_ref[0:hw, :])
    sh_ref[0:n, :] = _dot(xb, w1_ref[hw:2 * hw, :])
    sh_ref[n:n + 8, :] = jnp.zeros((8, sh_ref.shape[1]), F32)
    hid = p1 + sh_ref[1:n + 1, :] + b1_ref[...]
    o_ref[...] = _dot(_silu(hid).astype(BF16), w2_ref[...]) + b2_ref[...]


def compress_tokens(xh, pos, w1, b1, w2, b2):
    _, bg, n, hw = xh.shape
    hid = w1.shape[-1]
    dh = w2.shape[-1]
    return pl.pallas_call(
        _compress_kernel,
        grid=(2, bg),
        in_specs=[
            pl.BlockSpec((None, None, n, hw), lambda s, i: (s, i, 0, 0)),
            pl.BlockSpec((None, 1, 2 * hw), lambda s, i: (s, 0, 0)),
            pl.BlockSpec((None, 2 * hw, hid), lambda s, i: (s, 0, 0)),
            pl.BlockSpec((None, 1, hid), lambda s, i: (s, 0, 0)),
            pl.BlockSpec((None, hid, dh), lambda s, i: (s, 0, 0)),
            pl.BlockSpec((None, 1, dh), lambda s, i: (s, 0, 0)),
        ],
        out_specs=pl.BlockSpec((None, None, n, dh), lambda s, i: (s, i, 0, 0)),
        out_shape=jax.ShapeDtypeStruct((2, bg, n, dh), F32),
        scratch_shapes=[pltpu.VMEM((n + 8, hid), F32)],
        compiler_params=_params("parallel", "parallel"),
        name="compress_tokens",
    )(xh, pos.reshape(2, 1, 2 * hw), w1.astype(BF16), b1.reshape(2, 1, hid), w2.astype(BF16),
      b2.reshape(2, 1, dh))


def _split_bf16(x):
    hi = x.astype(BF16)
    lo = (x - hi.astype(F32)).astype(BF16)
    return hi, lo


def _hgrn_kernel(q_ref, f_ref, i_ref, g_ref, gamma_ref, gn_ref, o_ref, st_ref, *, layer):
    c, sub = HG_CHUNK, HG_SUB
    nsub = c // sub
    tt = q_ref.shape[0]

    @pl.when(pl.program_id(2) == 0)
    def _():
        st_ref[...] = jnp.zeros_like(st_ref)

    gamma = gamma_ref[...]
    e = jnp.exp(gamma - jnp.max(gamma, axis=0, keepdims=True))
    sm = e / jnp.sum(e, axis=0, keepdims=True)
    lb = jnp.sum(sm[0:layer + 1, :], axis=0, keepdims=True) - sm[0:1, :]
    row = lax.broadcasted_iota(jnp.int32, (c, c), 0)
    col = lax.broadcasted_iota(jnp.int32, (c, c), 1)
    tril = jnp.where(row >= col, 1.0, 0.0).astype(BF16)
    trow = lax.broadcasted_iota(jnp.int32, (1, sub, 1), 1)

    def chunk(ci, carry):
        r0 = pl.multiple_of(ci * c, c)
        q = q_ref[pl.ds(r0, c), :].astype(F32)
        fl = f_ref[pl.ds(r0, c), :].astype(F32)
        iv = i_ref[pl.ds(r0, c), :].astype(F32)
        gg = g_ref[pl.ds(r0, c), :].astype(F32)
        f = lb + (1.0 - lb) * _sigmoid(fl)
        logf = jnp.log(jnp.maximum(f, TINY))
        k = 1.0 - f
        hi, lo = _split_bf16(logf)
        gc = _dot(tril, hi) + _dot(tril, lo)
        glast = gc[c - 1:c, :]
        st = st_ref[...]
        ivb = iv.astype(BF16)
        o = _dot_nt((q * jnp.exp(gc)).astype(BF16), st.astype(BF16))
        kdec = k * jnp.exp(glast - gc)
        st_ref[...] = jnp.exp(glast) * st + _dot_tn(ivb, kdec.astype(BF16))

        parts = [jnp.zeros((sub, o.shape[1]), F32)]
        for si in range(1, nsub):
            lo_r, hi_r = si * sub, (si + 1) * sub
            ref = gc[lo_r - 1:lo_r, :]
            qhat = q[lo_r:hi_r, :] * jnp.exp(gc[lo_r:hi_r, :] - ref)
            khat = k[0:lo_r, :] * jnp.exp(ref - gc[0:lo_r, :])
            a = _dot_nt(qhat.astype(BF16), khat.astype(BF16))
            parts.append(_dot(a.astype(BF16), ivb[0:lo_r, :]))
        o = o + jnp.concatenate(parts, axis=0)

        q3 = q.reshape(nsub, sub, -1)
        k3 = k.reshape(nsub, sub, -1)
        g3 = gc.reshape(nsub, sub, -1)
        i3 = iv.reshape(nsub, sub, -1)
        od = jnp.zeros_like(q3)
        for s in range(sub):
            e = jnp.where(trow >= s, g3 - g3[:, s:s + 1, :], NEG)
            a = jnp.sum(q3 * k3[:, s:s + 1, :] * jnp.exp(e), axis=-1, keepdims=True)
            od = od + a * i3[:, s:s + 1, :]
        o = o + od.reshape(c, -1)

        o = o * lax.rsqrt(jnp.mean(o * o, axis=-1, keepdims=True) + EPS) * gn_ref[...]
        o_ref[pl.ds(r0, c), :] = (o * _silu(gg)).astype(o_ref.dtype)
        return carry

    lax.fori_loop(0, tt // c, chunk, 0, unroll=HG_UNROLL)


def hgrn2(u, lb_gamma, gnorm_w, *, layer, batch, seq, tt):
    h, dk, dv = HG_HEADS, HG_DK, HG_DV
    nt = seq // tt
    depth = lb_gamma.shape[0]

    def col(off):
        return pl.BlockSpec((tt, dk), lambda b, hh, t: (b * nt + t, off + hh))

    return pl.pallas_call(
        functools.partial(_hgrn_kernel, layer=layer),
        grid=(batch, h, nt),
        in_specs=[
            col(0), col(h), col(2 * h), col(3 * h),
            pl.BlockSpec((depth, dk), lambda b, hh, t: (0, hh)),
            pl.BlockSpec((1, dv), lambda b, hh, t: (0, hh)),
        ],
        out_specs=pl.BlockSpec((tt, dv), lambda b, hh, t: (b * nt + t, hh)),
        out_shape=jax.ShapeDtypeStruct((batch * seq, h * dv), BF16),
        scratch_shapes=[pltpu.VMEM((dv, dk), F32)],
        compiler_params=_params("parallel", "parallel", "arbitrary"),
        name="hgrn2",
    )(u, u, u, u, lb_gamma, gnorm_w.reshape(1, h * dv))


QL = NSA_HPG * Q_BLOCK
FLAG_BITS = 16


def _tile_positions(qb):
    lane = lax.broadcasted_iota(jnp.int32, (1, QL), 1)
    return qb * Q_BLOCK + (lane & (Q_BLOCK - 1))


def _nsa_select_kernel(q_ref, kc_ref, vct_ref, slope_ref, oc_ref, sel_ref, flag_ref, p_ref):
    nc = kc_ref.shape[0]
    ns = sel_ref.shape[0]
    qb = pl.program_id(2)
    q = (q_ref[...].astype(F32) * NSA_HEAD_DIM ** -0.5).astype(BF16)
    t = _tile_positions(qb)
    slope = slope_ref[...]

    s = _dot(kc_ref[...], q)
    n = lax.broadcasted_iota(jnp.int32, (nc, 1), 0)
    dist = (t - (n * CMP_STRIDE + (CMP_BLOCK - 1))).astype(F32)
    valid = dist >= 0
    s = jnp.where(valid, s - slope * dist, NEG)
    m = jnp.max(s, axis=0, keepdims=True)
    p = jnp.where(valid, jnp.exp(s - m), 0.0)
    inv = 1.0 / jnp.maximum(jnp.sum(p, axis=0, keepdims=True), TINY)
    oc_ref[...] = _dot(vct_ref[...], p.astype(BF16)) * inv
    pn = p * inv
    psum = pn[:, 0:Q_BLOCK]
    for h in range(1, NSA_HPG):
        psum = psum + pn[:, h * Q_BLOCK:(h + 1) * Q_BLOCK]

    p_ref[0:8, :] = jnp.zeros((8, Q_BLOCK), F32)
    p_ref[8:8 + nc, :] = psum
    p_ref[8 + nc:16 + nc, :] = jnp.zeros((8, Q_BLOCK), F32)
    ratio = SLC_BLOCK // CMP_STRIDE
    imp = p_ref[pl.ds(7, ns, stride=ratio), :]
    for r in range(1, ratio + 1):
        imp = imp + p_ref[pl.ds(7 + r, ns, stride=ratio), :]

    j = lax.broadcasted_iota(jnp.int32, (ns, 1), 0)
    cur = t[:, 0:Q_BLOCK] // SLC_BLOCK
    forced = (j == 0) | (j == cur) | (j == cur - 1)
    imp = jnp.where(forced, FORCE_SCORE, jnp.where(j > cur, -1.0, imp))
    sel = jnp.zeros((ns, Q_BLOCK), F32)
    for _ in range(min(SLC_TOPK, ns)):
        mx = jnp.max(imp, axis=0, keepdims=True)
        idx = jnp.min(jnp.where(imp == mx, j, ns), axis=0, keepdims=True)
        hit = j == idx
        sel = jnp.where(hit, 1.0, sel)
        imp = jnp.where(hit, NEG, imp)
    sel = jnp.where(j <= cur, sel, 0.0)
    sel_ref[...] = sel

    cnt = _dot_nt(jnp.ones((8, Q_BLOCK), BF16), sel.astype(BF16))
    flags = jnp.where(cnt > 0.0, 1.0, 0.0).astype(BF16)
    jj = lax.broadcasted_iota(jnp.int32, (ns, 128), 0)
    ww = lax.broadcasted_iota(jnp.int32, (ns, 128), 1)
    bit = jnp.left_shift(1, jj & (FLAG_BITS - 1)).astype(F32)
    pack = jnp.where(ww == jj // FLAG_BITS, bit, 0.0).astype(BF16)
    flag_ref[...] = _dot(flags, pack)


def nsa_select(qt, kc, vct, slopes):
    b, g, nqb, dh, ql = qt.shape
    nc = kc.shape[2]
    ns = nc * CMP_STRIDE // SLC_BLOCK
    tile = lambda *shape: pl.BlockSpec((None, None, None) + shape, lambda bi, gi, qi: (bi, gi, qi, 0, 0))
    return pl.pallas_call(
        _nsa_select_kernel,
        grid=(b, g, nqb),
        in_specs=[
            tile(dh, ql),
            pl.BlockSpec((None, None, nc, dh), lambda bi, gi, qi: (bi, gi, 0, 0)),
            pl.BlockSpec((None, None, dh, nc), lambda bi, gi, qi: (bi, gi, 0, 0)),
            pl.BlockSpec((None, 1, ql), lambda bi, gi, qi: (gi, 0, 0)),
        ],
        out_specs=[tile(dh, ql), tile(ns, Q_BLOCK), tile(8, 128)],
        out_shape=[
            jax.ShapeDtypeStruct((b, g, nqb, dh, ql), F32),
            jax.ShapeDtypeStruct((b, g, nqb, ns, Q_BLOCK), F32),
            jax.ShapeDtypeStruct((b, g, nqb, 8, 128), F32),
        ],
        scratch_shapes=[pltpu.VMEM((nc + 16, Q_BLOCK), F32)],
        compiler_params=_params("parallel", "parallel", "parallel"),
        name="nsa_select",
    )(qt, kc, vct, slopes)


ATT_GROUP = 8
DIAG_BLOCKS = Q_BLOCK // SLC_BLOCK
WIN_BLOCKS = WINDOW // SLC_BLOCK
KEY_LANES = 128
SLOPE_PARTS = 3
LOG2E = 1.4426950408889634


def _attend_group(k_src, v_src, blocks, masks, sel_rows, qa_ref, t, slope2, blk0,
                  kcat_ref, vcat_ref, s_ref, p_ref, m_ref, l_ref, acc_ref):
    nb = len(blocks)
    kb = SLC_BLOCK
    for b, j in enumerate(blocks):
        jc = jnp.maximum(j, 0)
        kcat_ref[b * kb:(b + 1) * kb, :] = k_src[jc]
        vcat_ref[b * kb:(b + 1) * kb, :] = v_src[jc]
    s_ref[0:nb * kb, :] = _dot(kcat_ref[0:nb * kb, :], qa_ref[...])

    rows = []
    mc = None
    for b, j in enumerate(blocks):
        brow = slope2 * ((j - blk0) * kb).astype(F32)
        if sel_rows[b] is not None:
            brow = jnp.where(sel_rows[b] > 0.0, brow, NEG)
        brow = jnp.where(j >= 0, brow, NEG)
        rows.append(brow)
        s = s_ref[b * kb:(b + 1) * kb, :]
        if masks[b] is not None:
            kpos = j * kb + lax.broadcasted_iota(jnp.int32, (kb, 1), 0)
            valid = (t >= kpos) if masks[b] == "causal" else (t - kpos < WINDOW)
            s = jnp.where(valid, s, NEG)
            s_ref[b * kb:(b + 1) * kb, :] = s
        cm = jnp.max(s.reshape(kb // 8, 8, QL), axis=0) + brow
        mc = cm if mc is None else jnp.maximum(mc, cm)
    m_old = m_ref[...]
    m_new = jnp.maximum(m_old, jnp.max(mc, axis=0, keepdims=True))
    alpha = jnp.exp2(m_old - m_new)
    m_ref[...] = m_new

    lsum = None
    for b in range(nb):
        p = jnp.exp2(s_ref[b * kb:(b + 1) * kb, :] - (m_new - rows[b]))
        ls = jnp.sum(p.reshape(kb // 8, 8, QL), axis=0)
        lsum = ls if lsum is None else lsum + ls
        p_ref[b * kb:(b + 1) * kb, :] = p.astype(BF16)
    l_ref[...] = alpha * l_ref[...] + lsum
    acc_ref[...] = alpha * acc_ref[...] + _dot_tn(vcat_ref[0:nb * kb, :], p_ref[0:nb * kb, :])


def _nsa_attend_kernel(fw_ref, q_ref, sa_ref, ks_ref, vs_ref, kw_ref, vw_ref, sel_ref, oc_ref, gate_ref, slope_ref,
                       o_ref, qa_ref, kcat_ref, vcat_ref, s_ref, p_ref, m_ref, l_ref, acc_ref, list_ref, *,
                       words_per_tile):
    bi, gi, qb = pl.program_id(0), pl.program_id(1), pl.program_id(2)
    tile_id = (bi * pl.num_programs(1) + gi) * pl.num_programs(2) + qb
    dh = q_ref.shape[0]
    qa_ref[0:dh, :] = (q_ref[...].astype(F32) * (NSA_HEAD_DIM ** -0.5 * LOG2E)).astype(BF16)
    qa_ref[dh:, :] = sa_ref[...]
    t = _tile_positions(qb)
    slope2 = slope_ref[...] * LOG2E
    blk0 = qb * DIAG_BLOCKS
    diag = [blk0 + i for i in range(DIAG_BLOCKS)]
    scratch = (kcat_ref, vcat_ref, s_ref, p_ref, m_ref, l_ref, acc_ref)

    def sel_row(j):
        row = sel_ref[pl.ds(jnp.maximum(j, 0), 1), :]
        return jnp.concatenate([row] * NSA_HPG, axis=1)

    def reset():
        m_ref[...] = jnp.full_like(m_ref, NEG)
        l_ref[...] = jnp.zeros_like(l_ref)
        acc_ref[...] = jnp.zeros_like(acc_ref)

    def result():
        l = jnp.sum(l_ref[...], axis=0, keepdims=True)
        return acc_ref[...] * (1.0 / jnp.maximum(l, TINY))

    def scan_flags(w, n):
        base = w * FLAG_BITS
        word = fw_ref[tile_id * words_per_tile + w] & ((1 << jnp.minimum(blk0 - base, FLAG_BITS)) - 1)
        for i in range(FLAG_BITS):
            list_ref[n] = base + i
            n = n + ((word >> i) & 1)
        return n

    n_sel = lax.fori_loop(0, (blk0 + FLAG_BITS - 1) // FLAG_BITS, scan_flags, 0)
    for i in range(ATT_GROUP):
        list_ref[n_sel + i] = -1

    reset()
    _attend_group(ks_ref, vs_ref, diag, ["causal"] * DIAG_BLOCKS, [sel_row(j) for j in diag],
                  qa_ref, t, slope2, blk0, *scratch)

    def slc_body(i, carry):
        blocks = [list_ref[i * ATT_GROUP + b] for b in range(ATT_GROUP)]
        _attend_group(ks_ref, vs_ref, blocks, [None] * ATT_GROUP, [sel_row(j) for j in blocks],
                      qa_ref, t, slope2, blk0, *scratch)
        return carry

    lax.fori_loop(0, (n_sel + ATT_GROUP - 1) // ATT_GROUP, slc_body, 0)
    o_slc = result()

    reset()
    past = [blk0 - WIN_BLOCKS + i for i in range(WIN_BLOCKS)]
    masks = ["window"] * DIAG_BLOCKS + [None] * (WIN_BLOCKS - DIAG_BLOCKS) + ["causal"] * DIAG_BLOCKS
    _attend_group(kw_ref, vw_ref, past + diag, masks, [None] * len(masks), qa_ref, t, slope2, blk0, *scratch)
    o_win = result()

    gate = _sigmoid(gate_ref[...])
    o_ref[...] = (gate[0:1, :] * oc_ref[...] + gate[1:2, :] * o_slc + gate[2:3, :] * o_win).astype(o_ref.dtype)


def nsa_attend(flag_words, qt, slope_rows, ks, vs, kw, vw, sel, oc, gates, slopes):
    b, g, nqb, dh, ql = qt.shape
    ns = ks.shape[2]
    nkeys = max(ATT_GROUP, WIN_BLOCKS + DIAG_BLOCKS) * SLC_BLOCK
    tile = lambda *shape: pl.BlockSpec((None, None, None) + shape, lambda bi, gi, qi, fw: (bi, gi, qi, 0, 0))
    seq = lambda *shape: pl.BlockSpec((None, None) + shape, lambda bi, gi, qi, fw: (bi, gi, 0, 0, 0))
    grid_spec = pltpu.PrefetchScalarGridSpec(
        num_scalar_prefetch=1,
        grid=(b, g, nqb),
        in_specs=[
            tile(dh, ql),
            pl.BlockSpec((None, KEY_LANES - dh, ql), lambda bi, gi, qi, fw: (gi, 0, 0)),
            seq(ns, SLC_BLOCK, KEY_LANES), seq(ns, SLC_BLOCK, dh),
            seq(ns, SLC_BLOCK, KEY_LANES), seq(ns, SLC_BLOCK, dh),
            tile(ns, Q_BLOCK), tile(dh, ql), tile(3, ql),
            pl.BlockSpec((None, 1, ql), lambda bi, gi, qi, fw: (gi, 0, 0)),
        ],
        out_specs=tile(dh, ql),
        scratch_shapes=[
            pltpu.VMEM((KEY_LANES, ql), BF16),
            pltpu.VMEM((nkeys, KEY_LANES), BF16), pltpu.VMEM((nkeys, dh), BF16),
            pltpu.VMEM((nkeys, ql), F32), pltpu.VMEM((nkeys, ql), BF16),
            pltpu.VMEM((1, ql), F32), pltpu.VMEM((8, ql), F32), pltpu.VMEM((dh, ql), F32),
            pltpu.SMEM((ns + ATT_GROUP,), jnp.int32),
        ],
    )
    return pl.pallas_call(
        functools.partial(_nsa_attend_kernel, words_per_tile=ns // FLAG_BITS),
        grid_spec=grid_spec,
        out_shape=jax.ShapeDtypeStruct((b, g, nqb, dh, ql), BF16),
        compiler_params=_params("parallel", "parallel", "arbitrary"),
        name="nsa_attend",
    )(flag_words, qt, slope_rows, ks, vs, kw, vw, sel, oc, gates, slopes)


def nsa_mixer(q, kc, vc, ks, vs, kw, vw, gate_logits, cmp_pos, cmp_w1, cmp_b1, cmp_w2, cmp_b2, *, batch, seq):
    g, hpg, dh = NSA_KV_GROUPS, NSA_HPG, NSA_HEAD_DIM
    nqb, ns, nc = seq // Q_BLOCK, seq // SLC_BLOCK, seq // CMP_STRIDE

    qt = q.reshape(batch, nqb, Q_BLOCK, g, hpg, dh).transpose(0, 3, 1, 5, 4, 2).reshape(batch, g, nqb, dh, QL)
    gates = gate_logits.astype(F32).reshape(batch, nqb, Q_BLOCK, 3, g, hpg).transpose(0, 4, 1, 3, 5, 2)
    gates = gates.reshape(batch, g, nqb, 3, QL)
    slopes = 2.0 ** (-8.0 * jnp.arange(1, NSA_HEADS + 1, dtype=F32) / NSA_HEADS)
    slopes = jnp.repeat(slopes.reshape(g, 1, hpg), Q_BLOCK, axis=2)

    def halves(x):
        x = x.reshape(batch, nc, CMP_STRIDE, g, dh).transpose(0, 3, 1, 2, 4)
        return x.reshape(batch * g, nc, CMP_STRIDE * dh)

    def value_blocks(x):
        return x.reshape(batch, ns, SLC_BLOCK, g, dh).transpose(0, 3, 1, 2, 4)

    def key_blocks(x):
        r = jnp.arange(SLC_BLOCK, dtype=F32).astype(BF16).reshape(1, 1, 1, SLC_BLOCK, 1)
        r = jnp.broadcast_to(r, (batch, g, ns, SLC_BLOCK, SLOPE_PARTS))
        pad = jnp.zeros((batch, g, ns, SLC_BLOCK, KEY_LANES - dh - SLOPE_PARTS), BF16)
        return jnp.concatenate([value_blocks(x), r, pad], axis=-1)

    s2 = slopes * LOG2E
    parts = []
    for _ in range(SLOPE_PARTS):
        part = s2.astype(BF16)
        parts.append(part)
        s2 = s2 - part.astype(F32)
    slope_rows = jnp.concatenate(parts + [jnp.zeros((g, KEY_LANES - dh - SLOPE_PARTS, QL), BF16)], axis=1)

    cmp = compress_tokens(jnp.stack([halves(kc), halves(vc)]), cmp_pos, cmp_w1, cmp_b1, cmp_w2, cmp_b2)
    cmp = cmp.astype(BF16).reshape(2, batch, g, nc, dh)
    k_cmp, v_cmp_t = cmp[0], cmp[1].transpose(0, 1, 3, 2)

    oc, sel, flags = nsa_select(qt, k_cmp, v_cmp_t, slopes)
    flag_words = flags[:, :, :, 0, :ns // FLAG_BITS].astype(jnp.int32).reshape(-1)
    ot = nsa_attend(flag_words, qt, slope_rows, key_blocks(ks), value_blocks(vs), key_blocks(kw),
                    value_blocks(vw), sel, oc, gates, slopes)
    ot = ot.reshape(batch, g, nqb, dh, hpg, Q_BLOCK).transpose(0, 2, 5, 1, 4, 3)
    return ot.reshape(batch * seq, g * hpg * dh)


ROW_TILE = 1024
COL_TILE = 1024
FFN_ROW_TILE = 512
FFN_HID_TILE = 512
CONV_SEQ_TILE = 512
HGRN_SEQ_TILE = 512


def _even_layer(x, nw, w_in, conv_w, conv_b, ln_w, ln_b, cmp_pos, cmp_w1, cmp_b1, cmp_w2, cmp_b2, w_out, *,
                batch, seq):
    c = conv_w.shape[-1]
    nq = NSA_HEADS * NSA_HEAD_DIM
    nkv = NSA_KV_GROUPS * NSA_HEAD_DIM
    n_in = w_in.shape[1]
    n_pad = -(-n_in // COL_TILE) * COL_TILE
    w_in = jnp.pad(w_in, ((0, 0), (0, n_pad - n_in))).astype(BF16)
    u = norm_matmul(x, nw, w_in, tm=ROW_TILE, tn=COL_TILE)
    a_out = conformer_conv(u, conv_w.reshape(CONV_WIDTH, c), conv_b, ln_w, ln_b, seq=seq, ts=CONV_SEQ_TILE)
    off = 2 * c
    q = u[:, off:off + nq]
    off += nq
    kvs = [u[:, off + i * nkv:off + (i + 1) * nkv] for i in range(6)]
    off += 6 * nkv
    gate_logits = u[:, off:off + 3 * NSA_HEADS]
    b_out = nsa_mixer(q, *kvs, gate_logits, cmp_pos, cmp_w1, cmp_b1, cmp_w2, cmp_b2, batch=batch, seq=seq)
    return matmul_residual(a_out, b_out, 0, 0, w_out.astype(BF16), x, tm=ROW_TILE, tn=COL_TILE)


def _odd_layer(x, nw, w_in, lb_gamma, gnorm_w, w_out, *, layer, batch, seq):
    u = norm_matmul(x, nw, w_in.astype(BF16), tm=ROW_TILE, tn=COL_TILE)
    o = hgrn2(u, lb_gamma, gnorm_w, layer=layer, batch=batch, seq=seq, tt=HGRN_SEQ_TILE)
    return matmul_residual(o, o, 0, 1, w_out.astype(BF16), x, tm=ROW_TILE, tn=COL_TILE)


def kernel(x, norm_w, final_norm_w, ev_w_in, ev_conv_w, ev_conv_b, ev_conv_ln_w, ev_conv_ln_b, ev_cmp_pos,
           ev_cmp_w1, ev_cmp_b1, ev_cmp_w2, ev_cmp_b2, ev_w_out, od_w_in, od_lb_gamma, od_gnorm_w, od_w_out,
           ffn_w_gu, ffn_w_down):
    batch, seq, d = x.shape
    depth = norm_w.shape[0]
    xs = x.reshape(batch * seq, d)
    for layer in range(depth):
        i = layer // 2
        if layer % 2 == 0:
            xs = _even_layer(xs, norm_w[layer, 0], ev_w_in[i], ev_conv_w[i], ev_conv_b[i], ev_conv_ln_w[i],
                             ev_conv_ln_b[i], ev_cmp_pos[i], ev_cmp_w1[i], ev_cmp_b1[i], ev_cmp_w2[i],
                             ev_cmp_b2[i], ev_w_out[i], batch=batch, seq=seq)
        else:
            xs = _odd_layer(xs, norm_w[layer, 0], od_w_in[i], od_lb_gamma.astype(F32), od_gnorm_w[i],
                            od_w_out[i], layer=layer, batch=batch, seq=seq)
        xs = ffn_block(xs, norm_w[layer, 1], ffn_w_gu[layer].astype(BF16), ffn_w_down[layer].astype(BF16),
                       final_norm_w, tm=FFN_ROW_TILE, th=FFN_HID_TILE, final_norm=layer == depth - 1)
    return xs.reshape(batch, seq, d)
```

```python
import functools

import jax
import jax.numpy as jnp
import numpy as np
from jax import lax
from jax.experimental import pallas as pl
from jax.experimental.pallas import tpu as pltpu

F32 = jnp.float32
BF16 = jnp.bfloat16

EPS = 1e-6
TINY = 1e-30
NEG = -1e30

VMEM_LIMIT_BYTES = 56 * 1024 * 1024

CONV_WIDTH = 31
NSA_HEADS = 16
NSA_HEAD_DIM = 64
NSA_KV_GROUPS = 4
NSA_HPG = NSA_HEADS // NSA_KV_GROUPS
CMP_STRIDE = 16
CMP_BLOCK = 32
SLC_BLOCK = 64
SLC_TOPK = 16
WINDOW = 512
Q_BLOCK = 128
FORCE_SCORE = 1.0e3
HG_HEADS = 16
HG_DK = 128
HG_DV = 128
HG_CHUNK = 64
HG_SUB = 16
HG_GROUP = 2


def _params(*sem):
    return pltpu.CompilerParams(dimension_semantics=sem, vmem_limit_bytes=VMEM_LIMIT_BYTES)


def _sigmoid(x):
    return 1.0 / (1.0 + jnp.exp(-x))


def _silu(x):
    return x * _sigmoid(x)


def _dot(a, b):
    return jnp.dot(a, b, preferred_element_type=F32)


def _dot_nt(a, b):
    return lax.dot_general(a, b, (((1,), (1,)), ((), ())), preferred_element_type=F32)


def _dot_tn(a, b):
    return lax.dot_general(a, b, (((0,), (0,)), ((), ())), preferred_element_type=F32)


def _norm_matmul_kernel(x_ref, nw_ref, w_ref, o_ref, h_ref):
    @pl.when(pl.program_id(1) == 0)
    def _():
        x = x_ref[...]
        ms = jnp.mean(x * x, axis=-1, keepdims=True)
        h_ref[...] = (x * lax.rsqrt(ms + EPS) * nw_ref[...]).astype(BF16)

    o_ref[...] = _dot(h_ref[...], w_ref[...]).astype(o_ref.dtype)


def norm_matmul(x, nw, w, *, tm, tn, out_dtype=BF16):
    m, k = x.shape
    n = w.shape[1]
    return pl.pallas_call(
        _norm_matmul_kernel,
        grid=(m // tm, n // tn),
        in_specs=[
            pl.BlockSpec((tm, k), lambda i, j: (i, 0)),
            pl.BlockSpec((1, k), lambda i, j: (0, 0)),
            pl.BlockSpec((k, tn), lambda i, j: (0, j)),
        ],
        out_specs=pl.BlockSpec((tm, tn), lambda i, j: (i, j)),
        out_shape=jax.ShapeDtypeStruct((m, n), out_dtype),
        scratch_shapes=[pltpu.VMEM((tm, k), BF16)],
        compiler_params=_params("parallel", "arbitrary"),
        name="norm_matmul",
    )(x, nw.reshape(1, k), w)


def _matmul_res_kernel(a1_ref, a2_ref, w1_ref, w2_ref, r_ref, o_ref):
    acc = _dot(a1_ref[...], w1_ref[...]) + _dot(a2_ref[...], w2_ref[...])
    o_ref[...] = r_ref[...] + acc


def matmul_residual(a1, a2, blk1, blk2, w, res, *, tm, tn):
    m = res.shape[0]
    k, n = w.shape
    kh = k // 2
    return pl.pallas_call(
        _matmul_res_kernel,
        grid=(m // tm, n // tn),
        in_specs=[
            pl.BlockSpec((tm, kh), lambda i, j: (i, blk1)),
            pl.BlockSpec((tm, kh), lambda i, j: (i, blk2)),
            pl.BlockSpec((kh, tn), lambda i, j: (0, j)),
            pl.BlockSpec((kh, tn), lambda i, j: (1, j)),
            pl.BlockSpec((tm, tn), lambda i, j: (i, j)),
        ],
        out_specs=pl.BlockSpec((tm, tn), lambda i, j: (i, j)),
        out_shape=jax.ShapeDtypeStruct((m, n), F32),
        compiler_params=_params("parallel", "arbitrary"),
        name="matmul_residual",
    )(a1, a2, w, w, res)


def _ffn_kernel(x_ref, nw_ref, wg_ref, wu_ref, wd_ref, fw_ref, o_ref, h_ref, acc_ref, *, final_norm):
    j = pl.program_id(1)

    @pl.when(j == 0)
    def _():
        x = x_ref[...]
        ms = jnp.mean(x * x, axis=-1, keepdims=True)
        h_ref[...] = (x * lax.rsqrt(ms + EPS) * nw_ref[...]).astype(BF16)
        acc_ref[...] = x

    h = h_ref[...]
    a = _dot(h, wg_ref[...])
    b = _dot(h, wu_ref[...])
    z = (_silu(a) * b).astype(BF16)
    acc_ref[...] += _dot(z, wd_ref[...])

    @pl.when(j == pl.num_programs(1) - 1)
    def _():
        y = acc_ref[...]
        if final_norm:
            ms = jnp.mean(y * y, axis=-1, keepdims=True)
            y = y * lax.rsqrt(ms + EPS) * fw_ref[...]
        o_ref[...] = y


def ffn_block(x, nw, w_gu, w_down, final_w, *, tm, th, final_norm):
    m, d = x.shape
    hid = w_down.shape[0]
    nh = hid // th
    return pl.pallas_call(
        functools.partial(_ffn_kernel, final_norm=final_norm),
        grid=(m // tm, nh),
        in_specs=[
            pl.BlockSpec((tm, d), lambda i, j: (i, 0)),
            pl.BlockSpec((1, d), lambda i, j: (0, 0)),
            pl.BlockSpec((d, th), lambda i, j: (0, j)),
            pl.BlockSpec((d, th), lambda i, j: (0, j + nh)),
            pl.BlockSpec((th, d), lambda i, j: (j, 0)),
            pl.BlockSpec((1, d), lambda i, j: (0, 0)),
        ],
        out_specs=pl.BlockSpec((tm, d), lambda i, j: (i, 0)),
        out_shape=jax.ShapeDtypeStruct((m, d), F32),
        scratch_shapes=[pltpu.VMEM((tm, d), BF16), pltpu.VMEM((tm, d), F32)],
        compiler_params=_params("parallel", "arbitrary"),
        name="ffn_block",
    )(x, nw.reshape(1, d), w_gu, w_gu, w_down, final_w.reshape(1, d))


CONV_HALO = 32
CONV_ROWS = 16


def _conv_kernel(a_ref, g_ref, ah_ref, gh_ref, cw_ref, cb_ref, lw_ref, lb_ref, o_ref, hs_ref, sh_ref, cv_ref, *,
                 tiles_per_seq):
    ts = a_ref.shape[0]
    first = (pl.program_id(0) % tiles_per_seq) == 0
    hprev = ah_ref[...].astype(F32) * _sigmoid(gh_ref[...].astype(F32))
    hs_ref[0:CONV_HALO, :] = jnp.where(first, 0.0, hprev)
    hs_ref[CONV_HALO:CONV_HALO + ts, :] = a_ref[...].astype(F32) * _sigmoid(g_ref[...].astype(F32))
    off = CONV_HALO - (CONV_WIDTH - 1)
    n_shifted = sh_ref.shape[1]
    for r in range(1, 8):
        sh_ref[r - 1] = hs_ref[r:r + n_shifted, :]

    def body(i, carry):
        base = pl.multiple_of(i * CONV_ROWS, CONV_ROWS)
        accs = [jnp.zeros((CONV_ROWS // 8, 8, hs_ref.shape[1]), F32) + cb_ref[...],
                jnp.zeros((CONV_ROWS // 8, 8, hs_ref.shape[1]), F32)]
        for w in range(CONV_WIDTH):
            r, a = (off + w) % 8, (off + w) // 8 * 8
            src = hs_ref if r == 0 else sh_ref.at[r - 1]
            rows = src[pl.ds(base + a, CONV_ROWS), :].reshape(CONV_ROWS // 8, 8, -1)
            accs[w % 2] = accs[w % 2] + rows * cw_ref[w * 8:(w + 1) * 8, :]
        cv_ref[pl.ds(base, CONV_ROWS), :] = (accs[0] + accs[1]).reshape(CONV_ROWS, -1)
        return carry

    lax.fori_loop(0, ts // CONV_ROWS, body, 0)

    y = cv_ref[...]
    mu = jnp.mean(y, axis=-1, keepdims=True)
    d = y - mu
    var = jnp.mean(d * d, axis=-1, keepdims=True)
    hn = d * lax.rsqrt(var + EPS) * lw_ref[...] + lb_ref[...]
    o_ref[...] = _silu(hn).astype(o_ref.dtype)


def conformer_conv(u, conv_w, conv_b, ln_w, ln_b, *, seq, ts):
    t = u.shape[0]
    c = conv_w.shape[1]
    hb = ts // CONV_HALO
    return pl.pallas_call(
        functools.partial(_conv_kernel, tiles_per_seq=seq // ts),
        grid=(t // ts,),
        in_specs=[
            pl.BlockSpec((ts, c), lambda i: (i, 0)),
            pl.BlockSpec((ts, c), lambda i: (i, 1)),
            pl.BlockSpec((CONV_HALO, c), lambda i: (jnp.maximum(i * hb - 1, 0), 0)),
            pl.BlockSpec((CONV_HALO, c), lambda i: (jnp.maximum(i * hb - 1, 0), 1)),
            pl.BlockSpec((CONV_WIDTH * 8, c), lambda i: (0, 0)),
            pl.BlockSpec((1, c), lambda i: (0, 0)),
            pl.BlockSpec((1, c), lambda i: (0, 0)),
            pl.BlockSpec((1, c), lambda i: (0, 0)),
        ],
        out_specs=pl.BlockSpec((ts, c), lambda i: (i, 0)),
        out_shape=jax.ShapeDtypeStruct((t, c), BF16),
        scratch_shapes=[pltpu.VMEM((CONV_HALO + ts, c), F32), pltpu.VMEM((7, CONV_HALO + ts - 8, c), F32),
                        pltpu.VMEM((ts, c), F32)],
        compiler_params=_params("parallel"),
        name="conformer_conv",
    )(u, u, u, u, jnp.repeat(conv_w, 8, axis=0), conv_b.reshape(1, c), ln_w.reshape(1, c), ln_b.reshape(1, c))


def _compress_kernel(x_ref, pos_ref, w1_ref, b1_ref, w2_ref, b2_ref, o_ref, sh_ref):
    n = x_ref.shape[0]
    hw = x_ref.shape[1]
    x = x_ref[...].astype(F32)
    xa = (x + pos_ref[:, 0:hw]).astype(BF16)
    xb = (x + pos_ref[:, hw:2 * hw]).astype(BF16)
    p1 = _dot(xa, w1_ref[0:hw, :])
    sh_ref[0:n, :] = _dot(xb, w1_ref[hw:2 * hw, :])
    sh_ref[n:n + 8, :] = jnp.zeros((8, sh_ref.shape[1]), F32)
    hid = p1 + sh_ref[1:n + 1, :] + b1_ref[...]
    o_ref[...] = _dot(_silu(hid).astype(BF16), w2_ref[...]) + b2_ref[...]


def compress_tokens(xh, pos, w1, b1, w2, b2):
    _, bg, n, hw = xh.shape
    hid = w1.shape[-1]
    dh = w2.shape[-1]
    return pl.pallas_call(
        _compress_kernel,
        grid=(2, bg),
        in_specs=[
            pl.BlockSpec((None, None, n, hw), lambda s, i: (s, i, 0, 0)),
            pl.BlockSpec((None, 1, 2 * hw), lambda s, i: (s, 0, 0)),
            pl.BlockSpec((None, 2 * hw, hid), lambda s, i: (s, 0, 0)),
            pl.BlockSpec((None, 1, hid), lambda s, i: (s, 0, 0)),
            pl.BlockSpec((None, hid, dh), lambda s, i: (s, 0, 0)),
            pl.BlockSpec((None, 1, dh), lambda s, i: (s, 0, 0)),
        ],
        out_specs=pl.BlockSpec((None, None, n, dh), lambda s, i: (s, i, 0, 0)),
        out_shape=jax.ShapeDtypeStruct((2, bg, n, dh), F32),
        scratch_shapes=[pltpu.VMEM((n + 8, hid), F32)],
        compiler_params=_params("parallel", "parallel"),
        name="compress_tokens",
    )(xh, pos.reshape(2, 1, 2 * hw), w1.astype(BF16), b1.reshape(2, 1, hid), w2.astype(BF16),
      b2.reshape(2, 1, dh))


def _split_bf16(x):
    hi = x.astype(BF16)
    lo = (x - hi.astype(F32)).astype(BF16)
    return hi, lo


def _hgrn_kernel(q_ref, f_ref, i_ref, g_ref, gamma_ref, gn_ref, tril_ref, sum_ref, o_ref,
                 st_ref, gc_ref, k_ref, hk_ref, qg_ref, u_ref, dcat_ref, ps_ref, opart_ref, *, layer):
    c, sub = HG_CHUNK, HG_SUB
    nsub = c // sub
    tt, dk = q_ref.shape
    n_chunks = tt // c

    @pl.when(pl.program_id(2) == 0)
    def _():
        st_ref[...] = jnp.zeros_like(st_ref)

    gamma = gamma_ref[...]
    e = jnp.exp(gamma - jnp.max(gamma, axis=0, keepdims=True))
    sm = e / jnp.sum(e, axis=0, keepdims=True)
    lb = jnp.sum(sm[0:layer + 1, :], axis=0, keepdims=True) - sm[0:1, :]

    def gates(rows):
        f = lb + (1.0 - lb) * _sigmoid(f_ref[rows, :].astype(F32))
        k = 1.0 - f
        hi, lo = _split_bf16(jnp.log(jnp.maximum(f, TINY)))
        gc = _dot(tril_ref[...], hi) + _dot(tril_ref[...], lo)
        gc_ref[rows, :] = gc
        k_ref[rows, :] = k
        hk_ref[rows, :] = gc - jnp.log(k)
        qg_ref[rows, :] = (q_ref[rows, :].astype(F32) * jnp.exp(gc)).astype(BF16)

    r8 = lax.broadcasted_iota(jnp.int32, (1, 8, 1), 1)
    srow = lax.broadcasted_iota(jnp.int32, (c, 1), 0)
    tcol = lax.broadcasted_iota(jnp.int32, (c, c), 1)

    def half_major(x):
        return [jnp.concatenate([x[(2 * i + h) * 8:(2 * i + h + 1) * 8, :] for i in range(nsub)], axis=0)
                .reshape(nsub, 8, -1) for h in range(2)]

    def state_free_part(ci):
        rows = slice(ci * c, (ci + 1) * c)
        q = q_ref[rows, :].astype(F32)
        gc = gc_ref[rows, :]
        k = k_ref[rows, :]
        ivb = i_ref[rows, :]
        glast = gc[c - 1:c, :]
        u_ref[ci] = _dot_tn((k * jnp.exp(glast - gc)).astype(BF16), ivb)

        qhat = [jnp.zeros((sub, dk), F32)]
        khat = []
        for si in range(1, nsub):
            lo_r, hi_r = si * sub, (si + 1) * sub
            ref = gc[lo_r - 1:lo_r, :]
            qhat.append(q[lo_r:hi_r, :] * jnp.exp(gc[lo_r:hi_r, :] - ref))
            khat.append(k * jnp.exp(jnp.where(srow < lo_r, ref - gc, NEG)))
        at = _dot_nt(jnp.concatenate(khat, axis=0).astype(BF16), jnp.concatenate(qhat, axis=0).astype(BF16))
        at_off = jnp.zeros((c, c), F32)
        for si in range(1, nsub):
            at_off = jnp.where(tcol // sub == si, at[(si - 1) * c:si * c, :], at_off)
        opart_ref[rows, :] = _dot_tn(at_off.astype(BF16), ivb)

        lo_rows = slice(ci * c, ci * c + c // 2)
        hi_rows = slice(ci * c + c // 2, (ci + 1) * c)
        q0, q1 = half_major(q)
        g0, g1 = half_major(gc)
        h0, h1 = half_major(hk_ref[rows, :])
        for s in range(sub // 2):
            key = h0[:, s:s + 1, :]
            d0 = q0 * jnp.exp(jnp.where(r8 >= s, g0 - key, NEG))
            d1 = q1 * jnp.exp(g1 - key)
            dcat_ref[lo_rows, s * dk:(s + 1) * dk] = d0.reshape(c // 2, dk).astype(BF16)
            dcat_ref[hi_rows, s * dk:(s + 1) * dk] = d1.reshape(c // 2, dk).astype(BF16)
        dcat_ref[lo_rows, (sub // 2) * dk:] = jnp.zeros((c // 2, (sub // 2) * dk), BF16)
        for s in range(sub // 2):
            key = h1[:, s:s + 1, :]
            d1 = q1 * jnp.exp(jnp.where(r8 >= s, g1 - key, NEG))
            col = (sub // 2 + s) * dk
            dcat_ref[hi_rows, col:col + dk] = d1.reshape(c // 2, dk).astype(BF16)

    group = tril_ref.shape[0]
    for g0 in range(0, tt, group):
        grows = slice(g0, g0 + group)
        gates(grows)
        for ci in range(g0 // c, (g0 + group) // c):
            state_free_part(ci)
        ps_ref[grows, :] = _dot(dcat_ref[grows, :], sum_ref[...])

    glast = [gc_ref[(ci + 1) * c - 1:(ci + 1) * c, :] for ci in range(n_chunks)]
    glast += [jnp.zeros((1, dk), F32)] * (-n_chunks % 8)
    decay_cols = jnp.exp(jnp.concatenate(glast, axis=0)).T
    st = st_ref[...]
    for ci in range(n_chunks):
        rows = slice(ci * c, (ci + 1) * c)
        opart_ref[rows, :] += _dot(qg_ref[rows, :], st.astype(BF16))
        st = decay_cols[:, ci:ci + 1] * st + u_ref[ci]
    st_ref[...] = st

    own_block = (lax.broadcasted_iota(jnp.int32, (c, dk), 1) // sub
                 == (lax.broadcasted_iota(jnp.int32, (c, dk), 0) // 8) % nsub)
    for ci in range(n_chunks):
        rows = slice(ci * c, (ci + 1) * c)
        ivb = i_ref[rows, :]
        a = jnp.where(own_block, ps_ref[rows, :], 0.0)
        od = _dot(a.astype(BF16), jnp.concatenate([ivb, jnp.zeros((dk - c, ivb.shape[1]), BF16)], axis=0))
        o = opart_ref[rows, :] + jnp.concatenate([od[(h * nsub + i) * 8:(h * nsub + i + 1) * 8, :]
                                                  for i in range(nsub) for h in range(2)], axis=0)
        o = o * lax.rsqrt(jnp.mean(o * o, axis=-1, keepdims=True) + EPS) * gn_ref[...]
        o_ref[rows, :] = (o * _silu(g_ref[rows, :].astype(F32))).astype(o_ref.dtype)


def hgrn2(u, lb_gamma, gnorm_w, *, layer, batch, seq, tt):
    h, dk, dv = HG_HEADS, HG_DK, HG_DV
    nt = seq // tt
    depth = lb_gamma.shape[0]

    def col(off):
        return pl.BlockSpec((tt, dk), lambda b, hh, t: (b * nt + t, off + hh))

    r = jnp.arange(HG_GROUP * HG_CHUNK)
    tril = ((r[:, None] >= r[None, :]) & (r[:, None] // HG_CHUNK == r[None, :] // HG_CHUNK)).astype(BF16)
    key_offset = jnp.arange(HG_SUB * dk) // dk
    summer = (key_offset[:, None] == jnp.arange(dk)[None, :] % HG_SUB).astype(BF16)
    return pl.pallas_call(
        functools.partial(_hgrn_kernel, layer=layer),
        grid=(batch, h, nt),
        in_specs=[
            col(0), col(h), col(2 * h), col(3 * h),
            pl.BlockSpec((depth, dk), lambda b, hh, t: (0, hh)),
            pl.BlockSpec((1, dv), lambda b, hh, t: (0, hh)),
            pl.BlockSpec((HG_GROUP * HG_CHUNK, HG_GROUP * HG_CHUNK), lambda b, hh, t: (0, 0)),
            pl.BlockSpec((HG_SUB * dk, dk), lambda b, hh, t: (0, 0)),
        ],
        out_specs=pl.BlockSpec((tt, dv), lambda b, hh, t: (b * nt + t, hh)),
        out_shape=jax.ShapeDtypeStruct((batch * seq, h * dv), BF16),
        scratch_shapes=[
            pltpu.VMEM((dk, dv), F32),
            pltpu.VMEM((tt, dk), F32), pltpu.VMEM((tt, dk), F32), pltpu.VMEM((tt, dk), F32),
            pltpu.VMEM((tt, dk), BF16), pltpu.VMEM((tt // HG_CHUNK, dk, dv), F32),
            pltpu.VMEM((tt, HG_SUB * dk), BF16), pltpu.VMEM((tt, dk), F32), pltpu.VMEM((tt, dv), F32),
        ],
        compiler_params=_params("parallel", "parallel", "arbitrary"),
        name="hgrn2",
    )(u, u, u, u, lb_gamma, gnorm_w.reshape(1, h * dv), tril, summer)


QL = NSA_HPG * Q_BLOCK
FLAG_BITS = 16


def _tile_positions(qb):
    lane = lax.broadcasted_iota(jnp.int32, (1, QL), 1)
    return qb * Q_BLOCK + (lane & (Q_BLOCK - 1))


CMP_CHUNK = 128
SEL_CLASSES = 4
KEY_LANES = 128
SLOPE_PARTS = 3
LOG2E = 1.4426950408889634


def _load_query(qa_ref, q_ref, sa_ref):
    dh = q_ref.shape[0]
    qa_ref[0:dh, :] = (q_ref[...].astype(F32) * (NSA_HEAD_DIM ** -0.5 * LOG2E)).astype(BF16)
    qa_ref[dh:, :] = sa_ref[...]


def _nsa_select_body(nch, q_ref, sa_ref, kc_ref, vct_ref, slope_ref, oc_ref, sel_ref, flag_ref,
                     qa_ref, s_ref, pb_ref, p_ref):
    ck = CMP_CHUNK
    nk = nch * ck
    ns = nk * CMP_STRIDE // SLC_BLOCK
    qb = pl.program_id(2)
    _load_query(qa_ref, q_ref, sa_ref)
    t = _tile_positions(qb)
    slope2 = slope_ref[...] * LOG2E

    rows = []
    mc = None
    for ch in range(nch):
        rs = slice(ch * ck, (ch + 1) * ck)
        s = _dot(kc_ref[rs, :], qa_ref[...])
        end = (ch * ck + lax.broadcasted_iota(jnp.int32, (ck, 1), 0)) * CMP_STRIDE + (CMP_BLOCK - 1)
        s = jnp.where(t >= end, s, NEG)
        s_ref[rs, :] = s
        brow = slope2 * (ch * ck * CMP_STRIDE + (CMP_BLOCK - 1) - qb * Q_BLOCK).astype(F32)
        rows.append(brow)
        cm = jnp.max(s.reshape(ck // 8, 8, QL), axis=0) + brow
        mc = cm if mc is None else jnp.maximum(mc, cm)
    m = jnp.max(mc, axis=0, keepdims=True)

    lsum = None
    for ch in range(nch):
        rs = slice(ch * ck, (ch + 1) * ck)
        p = jnp.exp2(s_ref[rs, :] - (m - rows[ch]))
        ls = jnp.sum(p.reshape(ck // 8, 8, QL), axis=0)
        lsum = ls if lsum is None else lsum + ls
        s_ref[rs, :] = p
        pb_ref[rs, :] = p.astype(BF16)
    l = jnp.sum(lsum, axis=0, keepdims=True)
    inv = jnp.where(t >= CMP_BLOCK - 1, 1.0 / jnp.maximum(l, TINY), 0.0)
    oc_ref[...] = _dot(vct_ref[:, 0:nk], pb_ref[0:nk, :]) * inv

    p_ref[0:8, :] = jnp.zeros((8, Q_BLOCK), F32)
    for ch in range(nch):
        pn = s_ref[ch * ck:(ch + 1) * ck, :] * inv
        psum = pn[:, 0:Q_BLOCK]
        for h in range(1, NSA_HPG):
            psum = psum + pn[:, h * Q_BLOCK:(h + 1) * Q_BLOCK]
        p_ref[8 + ch * ck:8 + (ch + 1) * ck, :] = psum
    p_ref[8 + nk:16 + nk, :] = jnp.zeros((8, Q_BLOCK), F32)

    ratio = SLC_BLOCK // CMP_STRIDE
    imp = p_ref[pl.ds(7, ns, stride=ratio), :]
    for r in range(1, ratio + 1):
        imp = imp + p_ref[pl.ds(7 + r, ns, stride=ratio), :]

    j = lax.broadcasted_iota(jnp.int32, (ns, 1), 0)
    cur = t[:, 0:Q_BLOCK] // SLC_BLOCK
    forced = (j == 0) | (j == cur) | (j == cur - 1)
    imp = jnp.where(forced, FORCE_SCORE, jnp.where(j > cur, -1.0, imp))
    for _ in range(min(SLC_TOPK, ns)):
        mx = jnp.max(imp, axis=0, keepdims=True)
        idx = jnp.min(jnp.where(imp == mx, j, ns), axis=0, keepdims=True)
        imp = jnp.where(j == idx, NEG, imp)
    sel = jnp.where((imp == NEG) & (j <= cur), 1.0, 0.0)
    sel_ref[0:ns, :] = sel
    if ns < sel_ref.shape[0]:
        sel_ref[ns:, :] = jnp.zeros((sel_ref.shape[0] - ns, Q_BLOCK), F32)

    cnt = _dot_nt(jnp.ones((8, Q_BLOCK), BF16), sel.astype(BF16))
    flags = jnp.where(cnt > 0.0, 1.0, 0.0).astype(BF16)
    jj = lax.broadcasted_iota(jnp.int32, (ns, 128), 0)
    ww = lax.broadcasted_iota(jnp.int32, (ns, 128), 1)
    bit = jnp.left_shift(1, jj & (FLAG_BITS - 1)).astype(F32)
    pack = jnp.where(ww == jj // FLAG_BITS, bit, 0.0).astype(BF16)
    flag_ref[...] = _dot(flags, pack)


def _nsa_select_kernel(q_ref, sa_ref, kc_ref, *rest):
    n_chunks = kc_ref.shape[0] // CMP_CHUNK
    n_cls = min(SEL_CLASSES, n_chunks)
    cls = pl.program_id(2) * n_cls // pl.num_programs(2)
    for c in range(n_cls):
        @pl.when(cls == c)
        def _(c=c):
            _nsa_select_body((c + 1) * n_chunks // n_cls, q_ref, sa_ref, kc_ref, *rest)


def nsa_select(qt, slope_rows, kc, vct, slopes):
    b, g, nqb, dh, ql = qt.shape
    nc = kc.shape[2]
    ns = nc * CMP_STRIDE // SLC_BLOCK
    tile = lambda *shape: pl.BlockSpec((None, None, None) + shape, lambda bi, gi, qi: (bi, gi, qi, 0, 0))
    return pl.pallas_call(
        _nsa_select_kernel,
        grid=(b, g, nqb),
        in_specs=[
            tile(dh, ql),
            pl.BlockSpec((None, KEY_LANES - dh, ql), lambda bi, gi, qi: (gi, 0, 0)),
            pl.BlockSpec((None, None, nc, KEY_LANES), lambda bi, gi, qi: (bi, gi, 0, 0)),
            pl.BlockSpec((None, None, dh, nc), lambda bi, gi, qi: (bi, gi, 0, 0)),
            pl.BlockSpec((None, 1, ql), lambda bi, gi, qi: (gi, 0, 0)),
        ],
        out_specs=[tile(dh, ql), tile(ns, Q_BLOCK), tile(8, 128)],
        out_shape=[
            jax.ShapeDtypeStruct((b, g, nqb, dh, ql), F32),
            jax.ShapeDtypeStruct((b, g, nqb, ns, Q_BLOCK), F32),
            jax.ShapeDtypeStruct((b, g, nqb, 8, 128), F32),
        ],
        scratch_shapes=[
            pltpu.VMEM((KEY_LANES, ql), BF16), pltpu.VMEM((nc, ql), F32), pltpu.VMEM((nc, ql), BF16),
            pltpu.VMEM((nc + 16, Q_BLOCK), F32),
        ],
        compiler_params=_params("parallel", "parallel", "parallel"),
        name="nsa_select",
    )(qt, slope_rows, kc, vct, slopes)


ATT_GROUP = 8
DIAG_BLOCKS = Q_BLOCK // SLC_BLOCK
WIN_BLOCKS = WINDOW // SLC_BLOCK


def _attend_group(k_src, v_src, blocks, masks, sel_rows, qa_ref, t, slope2, blk0,
                  kcat_ref, vcat_ref, s_ref, p_ref, m_ref, l_ref, acc_ref):
    nb = len(blocks)
    kb = SLC_BLOCK
    for b, j in enumerate(blocks):
        jc = jnp.maximum(j, 0)
        kcat_ref[b * kb:(b + 1) * kb, :] = k_src[jc]
        vcat_ref[b * kb:(b + 1) * kb, :] = v_src[jc]
    s_ref[0:nb * kb, :] = _dot(kcat_ref[0:nb * kb, :], qa_ref[...])

    rows = []
    mc = None
    for b, j in enumerate(blocks):
        brow = slope2 * ((j - blk0) * kb).astype(F32)
        if sel_rows[b] is not None:
            brow = jnp.where(sel_rows[b] > 0.0, brow, NEG)
        brow = jnp.where(j >= 0, brow, NEG)
        rows.append(brow)
        s = s_ref[b * kb:(b + 1) * kb, :]
        if masks[b] is not None:
            kpos = j * kb + lax.broadcasted_iota(jnp.int32, (kb, 1), 0)
            valid = (t >= kpos) if masks[b] == "causal" else (t - kpos < WINDOW)
            s = jnp.where(valid, s, NEG)
            s_ref[b * kb:(b + 1) * kb, :] = s
        cm = jnp.max(s.reshape(kb // 8, 8, QL), axis=0) + brow
        mc = cm if mc is None else jnp.maximum(mc, cm)
    m_old = m_ref[...]
    m_new = jnp.maximum(m_old, jnp.max(mc, axis=0, keepdims=True))
    alpha = jnp.exp2(m_old - m_new)
    m_ref[...] = m_new

    lsum = None
    for b in range(nb):
        p = jnp.exp2(s_ref[b * kb:(b + 1) * kb, :] - (m_new - rows[b]))
        ls = jnp.sum(p.reshape(kb // 8, 8, QL), axis=0)
        lsum = ls if lsum is None else lsum + ls
        p_ref[b * kb:(b + 1) * kb, :] = p.astype(BF16)
    l_ref[...] = alpha * l_ref[...] + lsum
    acc_ref[...] = alpha * acc_ref[...] + _dot_tn(vcat_ref[0:nb * kb, :], p_ref[0:nb * kb, :])


def _nsa_attend_kernel(fw_ref, q_ref, sa_ref, ks_ref, vs_ref, kw_ref, vw_ref, sel_ref, oc_ref, gate_ref, slope_ref,
                       o_ref, qa_ref, kcat_ref, vcat_ref, s_ref, p_ref, m_ref, l_ref, acc_ref, list_ref, *,
                       words_per_tile):
    bi, gi, qb = pl.program_id(0), pl.program_id(1), pl.program_id(2)
    tile_id = (bi * pl.num_programs(1) + gi) * pl.num_programs(2) + qb
    _load_query(qa_ref, q_ref, sa_ref)
    t = _tile_positions(qb)
    slope2 = slope_ref[...] * LOG2E
    blk0 = qb * DIAG_BLOCKS
    diag = [blk0 + i for i in range(DIAG_BLOCKS)]
    scratch = (kcat_ref, vcat_ref, s_ref, p_ref, m_ref, l_ref, acc_ref)

    def sel_row(j):
        row = sel_ref[pl.ds(jnp.maximum(j, 0), 1), :]
        return jnp.concatenate([row] * NSA_HPG, axis=1)

    def reset():
        m_ref[...] = jnp.full_like(m_ref, NEG)
        l_ref[...] = jnp.zeros_like(l_ref)
        acc_ref[...] = jnp.zeros_like(acc_ref)

    def result():
        l = jnp.sum(l_ref[...], axis=0, keepdims=True)
        return acc_ref[...] * (1.0 / jnp.maximum(l, TINY))

    def scan_flags(w, n):
        base = w * FLAG_BITS
        word = fw_ref[tile_id * words_per_tile + w] & ((1 << jnp.minimum(blk0 - base, FLAG_BITS)) - 1)
        for i in range(FLAG_BITS):
            list_ref[n] = base + i
            n = n + ((word >> i) & 1)
        return n

    n_sel = lax.fori_loop(0, (blk0 + FLAG_BITS - 1) // FLAG_BITS, scan_flags, 0)
    for i in range(ATT_GROUP):
        list_ref[n_sel + i] = -1

    reset()
    _attend_group(ks_ref, vs_ref, diag, ["causal"] * DIAG_BLOCKS, [sel_row(j) for j in diag],
                  qa_ref, t, slope2, blk0, *scratch)

    def slc_body(i, carry):
        blocks = [list_ref[i * ATT_GROUP + b] for b in range(ATT_GROUP)]
        _attend_group(ks_ref, vs_ref, blocks, [None] * ATT_GROUP, [sel_row(j) for j in blocks],
                      qa_ref, t, slope2, blk0, *scratch)
        return carry

    lax.fori_loop(0, (n_sel + ATT_GROUP - 1) // ATT_GROUP, slc_body, 0)
    o_slc = result()

    reset()
    past = [blk0 - WIN_BLOCKS + i for i in range(WIN_BLOCKS)]
    masks = ["window"] * DIAG_BLOCKS + [None] * (WIN_BLOCKS - DIAG_BLOCKS) + ["causal"] * DIAG_BLOCKS
    _attend_group(kw_ref, vw_ref, past + diag, masks, [None] * len(masks), qa_ref, t, slope2, blk0, *scratch)
    o_win = result()

    gate = _sigmoid(gate_ref[...])
    o_ref[...] = (gate[0:1, :] * oc_ref[...] + gate[1:2, :] * o_slc + gate[2:3, :] * o_win).astype(o_ref.dtype)


def nsa_attend(flag_words, qt, slope_rows, ks, vs, kw, vw, sel, oc, gates, slopes):
    b, g, nqb, dh, ql = qt.shape
    ns = ks.shape[2]
    nkeys = max(ATT_GROUP, WIN_BLOCKS + DIAG_BLOCKS) * SLC_BLOCK
    tile = lambda *shape: pl.BlockSpec((None, None, None) + shape, lambda bi, gi, qi, fw: (bi, gi, qi, 0, 0))
    seq = lambda *shape: pl.BlockSpec((None, None) + shape, lambda bi, gi, qi, fw: (bi, gi, 0, 0, 0))
    grid_spec = pltpu.PrefetchScalarGridSpec(
        num_scalar_prefetch=1,
        grid=(b, g, nqb),
        in_specs=[
            tile(dh, ql),
            pl.BlockSpec((None, KEY_LANES - dh, ql), lambda bi, gi, qi, fw: (gi, 0, 0)),
            seq(ns, SLC_BLOCK, KEY_LANES), seq(ns, SLC_BLOCK, dh),
            seq(ns, SLC_BLOCK, KEY_LANES), seq(ns, SLC_BLOCK, dh),
            tile(ns, Q_BLOCK), tile(dh, ql), tile(3, ql),
            pl.BlockSpec((None, 1, ql), lambda bi, gi, qi, fw: (gi, 0, 0)),
        ],
        out_specs=tile(dh, ql),
        scratch_shapes=[
            pltpu.VMEM((KEY_LANES, ql), BF16),
            pltpu.VMEM((nkeys, KEY_LANES), BF16), pltpu.VMEM((nkeys, dh), BF16),
            pltpu.VMEM((nkeys, ql), F32), pltpu.VMEM((nkeys, ql), BF16),
            pltpu.VMEM((1, ql), F32), pltpu.VMEM((8, ql), F32), pltpu.VMEM((dh, ql), F32),
            pltpu.SMEM((ns + ATT_GROUP,), jnp.int32),
        ],
    )
    return pl.pallas_call(
        functools.partial(_nsa_attend_kernel, words_per_tile=ns // FLAG_BITS),
        grid_spec=grid_spec,
        out_shape=jax.ShapeDtypeStruct((b, g, nqb, dh, ql), BF16),
        compiler_params=_params("parallel", "parallel", "arbitrary"),
        name="nsa_attend",
    )(flag_words, qt, slope_rows, ks, vs, kw, vw, sel, oc, gates, slopes)


def nsa_mixer(q, kc, vc, ks, vs, kw, vw, gate_logits, cmp_pos, cmp_w1, cmp_b1, cmp_w2, cmp_b2, *, batch, seq):
    g, hpg, dh = NSA_KV_GROUPS, NSA_HPG, NSA_HEAD_DIM
    nqb, ns, nc = seq // Q_BLOCK, seq // SLC_BLOCK, seq // CMP_STRIDE

    qt = q.reshape(batch, nqb, Q_BLOCK, g, hpg, dh).transpose(0, 3, 1, 5, 4, 2).reshape(batch, g, nqb, dh, QL)
    gates = gate_logits.astype(F32).reshape(batch, nqb, Q_BLOCK, 3, g, hpg).transpose(0, 4, 1, 3, 5, 2)
    gates = gates.reshape(batch, g, nqb, 3, QL)
    slopes = 2.0 ** (-8.0 * jnp.arange(1, NSA_HEADS + 1, dtype=F32) / NSA_HEADS)
    slopes = jnp.repeat(slopes.reshape(g, 1, hpg), Q_BLOCK, axis=2)

    def halves(x):
        x = x.reshape(batch, nc, CMP_STRIDE, g, dh).transpose(0, 3, 1, 2, 4)
        return x.reshape(batch * g, nc, CMP_STRIDE * dh)

    def value_blocks(x):
        return x.reshape(batch, ns, SLC_BLOCK, g, dh).transpose(0, 3, 1, 2, 4)

    def key_blocks(x):
        r = jnp.arange(SLC_BLOCK, dtype=F32).astype(BF16).reshape(1, 1, 1, SLC_BLOCK, 1)
        r = jnp.broadcast_to(r, (batch, g, ns, SLC_BLOCK, SLOPE_PARTS))
        pad = jnp.zeros((batch, g, ns, SLC_BLOCK, KEY_LANES - dh - SLOPE_PARTS), BF16)
        return jnp.concatenate([value_blocks(x), r, pad], axis=-1)

    s2 = slopes * LOG2E
    parts = []
    for _ in range(SLOPE_PARTS):
        part = s2.astype(BF16)
        parts.append(part)
        s2 = s2 - part.astype(F32)
    slope_rows = jnp.concatenate(parts + [jnp.zeros((g, KEY_LANES - dh - SLOPE_PARTS, QL), BF16)], axis=1)

    cmp = compress_tokens(jnp.stack([halves(kc), halves(vc)]), cmp_pos, cmp_w1, cmp_b1, cmp_w2, cmp_b2)
    cmp = cmp.astype(BF16).reshape(2, batch, g, nc, dh)
    r = (jnp.arange(nc) % CMP_CHUNK * CMP_STRIDE).astype(BF16).reshape(1, 1, nc, 1)
    k_cmp = jnp.concatenate([cmp[0], jnp.broadcast_to(r, (batch, g, nc, SLOPE_PARTS)),
                             jnp.zeros((batch, g, nc, KEY_LANES - dh - SLOPE_PARTS), BF16)], axis=-1)
    v_cmp_t = cmp[1].transpose(0, 1, 3, 2)

    oc, sel, flags = nsa_select(qt, slope_rows, k_cmp, v_cmp_t, slopes)
    flag_words = flags[:, :, :, 0, :ns // FLAG_BITS].astype(jnp.int32).reshape(-1)
    ot = nsa_attend(flag_words, qt, slope_rows, key_blocks(ks), value_blocks(vs), key_blocks(kw),
                    value_blocks(vw), sel, oc, gates, slopes)
    ot = ot.reshape(batch, g, nqb, dh, hpg, Q_BLOCK).transpose(0, 2, 5, 1, 4, 3)
    return ot.reshape(batch * seq, g * hpg * dh)


ROW_TILE = 1024
COL_TILE = 1024
FFN_ROW_TILE = 512
FFN_HID_TILE = 512
CONV_SEQ_TILE = 512
HGRN_SEQ_TILE = 512


def _even_layer(x, nw, w_in, conv_w, conv_b, ln_w, ln_b, cmp_pos, cmp_w1, cmp_b1, cmp_w2, cmp_b2, w_out, *,
                batch, seq):
    c = conv_w.shape[-1]
    nq = NSA_HEADS * NSA_HEAD_DIM
    nkv = NSA_KV_GROUPS * NSA_HEAD_DIM
    n_in = w_in.shape[1]
    n_pad = -(-n_in // COL_TILE) * COL_TILE
    w_in = jnp.pad(w_in, ((0, 0), (0, n_pad - n_in))).astype(BF16)
    u = norm_matmul(x, nw, w_in, tm=ROW_TILE, tn=COL_TILE)
    a_out = conformer_conv(u, conv_w.reshape(CONV_WIDTH, c), conv_b, ln_w, ln_b, seq=seq, ts=CONV_SEQ_TILE)
    off = 2 * c
    q = u[:, off:off + nq]
    off += nq
    kvs = [u[:, off + i * nkv:off + (i + 1) * nkv] for i in range(6)]
    off += 6 * nkv
    gate_logits = u[:, off:off + 3 * NSA_HEADS]
    b_out = nsa_mixer(q, *kvs, gate_logits, cmp_pos, cmp_w1, cmp_b1, cmp_w2, cmp_b2, batch=batch, seq=seq)
    return matmul_residual(a_out, b_out, 0, 0, w_out.astype(BF16), x, tm=ROW_TILE, tn=COL_TILE)


def _odd_layer(x, nw, w_in, lb_gamma, gnorm_w, w_out, *, layer, batch, seq):
    u = norm_matmul(x, nw, w_in.astype(BF16), tm=ROW_TILE, tn=COL_TILE)
    o = hgrn2(u, lb_gamma, gnorm_w, layer=layer, batch=batch, seq=seq, tt=HGRN_SEQ_TILE)
    return matmul_residual(o, o, 0, 1, w_out.astype(BF16), x, tm=ROW_TILE, tn=COL_TILE)


def kernel(x, norm_w, final_norm_w, ev_w_in, ev_conv_w, ev_conv_b, ev_conv_ln_w, ev_conv_ln_b, ev_cmp_pos,
           ev_cmp_w1, ev_cmp_b1, ev_cmp_w2, ev_cmp_b2, ev_w_out, od_w_in, od_lb_gamma, od_gnorm_w, od_w_out,
           ffn_w_gu, ffn_w_down):
    batch, seq, d = x.shape
    depth = norm_w.shape[0]
    xs = x.reshape(batch * seq, d)
    for layer in range(depth):
        i = layer // 2
        if layer % 2 == 0:
            xs = _even_layer(xs, norm_w[layer, 0], ev_w_in[i], ev_conv_w[i], ev_conv_b[i], ev_conv_ln_w[i],
                             ev_conv_ln_b[i], ev_cmp_pos[i], ev_cmp_w1[i], ev_cmp_b1[i], ev_cmp_w2[i],
                             ev_cmp_b2[i], ev_w_out[i], batch=batch, seq=seq)
        else:
            xs = _odd_layer(xs, norm_w[layer, 0], od_w_in[i], od_lb_gamma.astype(F32), od_gnorm_w[i],
                            od_w_out[i], layer=layer, batch=batch, seq=seq)
        xs = ffn_block(xs, norm_w[layer, 1], ffn_w_gu[layer].astype(BF16), ffn_w_down[layer].astype(BF16),
                       final_norm_w, tm=FFN_ROW_TILE, th=FFN_HID_TILE, final_norm=layer == depth - 1)
    return xs.reshape(batch, seq, d)
```

```python
import functools

import jax
import jax.numpy as jnp
import numpy as np
from jax import lax
from jax.experimental import pallas as pl
from jax.experimental.pallas import tpu as pltpu

F32 = jnp.float32
BF16 = jnp.bfloat16

EPS = 1e-6
TINY = 1e-30
NEG = -1e30

VMEM_LIMIT_BYTES = 56 * 1024 * 1024

CONV_WIDTH = 31
NSA_HEADS = 16
NSA_HEAD_DIM = 64
NSA_KV_GROUPS = 4
NSA_HPG = NSA_HEADS // NSA_KV_GROUPS
CMP_STRIDE = 16
CMP_BLOCK = 32
SLC_BLOCK = 64
SLC_TOPK = 16
WINDOW = 512
Q_BLOCK = 128
FORCE_SCORE = 1.0e3
HG_HEADS = 16
HG_DK = 128
HG_DV = 128
HG_CHUNK = 64
HG_SUB = 16
HG_GROUP = 2


def _params(*sem):
    return pltpu.CompilerParams(dimension_semantics=sem, vmem_limit_bytes=VMEM_LIMIT_BYTES)


def _sigmoid(x):
    return 1.0 / (1.0 + jnp.exp(-x))


def _silu(x):
    return x * _sigmoid(x)


def _dot(a, b):
    return jnp.dot(a, b, preferred_element_type=F32)


def _dot_nt(a, b):
    return lax.dot_general(a, b, (((1,), (1,)), ((), ())), preferred_element_type=F32)


def _dot_tn(a, b):
    return lax.dot_general(a, b, (((0,), (0,)), ((), ())), preferred_element_type=F32)


def _norm_matmul_kernel(x_ref, nw_ref, w_ref, o_ref, h_ref):
    @pl.when(pl.program_id(1) == 0)
    def _():
        x = x_ref[...]
        ms = jnp.mean(x * x, axis=-1, keepdims=True)
        h_ref[...] = (x * lax.rsqrt(ms + EPS) * nw_ref[...]).astype(BF16)

    o_ref[...] = _dot(h_ref[...], w_ref[...]).astype(o_ref.dtype)


def norm_matmul(x, nw, w, *, tm, tn, out_dtype=BF16):
    m, k = x.shape
    n = w.shape[1]
    return pl.pallas_call(
        _norm_matmul_kernel,
        grid=(m // tm, n // tn),
        in_specs=[
            pl.BlockSpec((tm, k), lambda i, j: (i, 0)),
            pl.BlockSpec((1, k), lambda i, j: (0, 0)),
            pl.BlockSpec((k, tn), lambda i, j: (0, j)),
        ],
        out_specs=pl.BlockSpec((tm, tn), lambda i, j: (i, j)),
        out_shape=jax.ShapeDtypeStruct((m, n), out_dtype),
        scratch_shapes=[pltpu.VMEM((tm, k), BF16)],
        compiler_params=_params("parallel", "arbitrary"),
        name="norm_matmul",
    )(x, nw.reshape(1, k), w)


def _matmul_res_kernel(a1_ref, a2_ref, w1_ref, w2_ref, r_ref, o_ref):
    acc = _dot(a1_ref[...], w1_ref[...]) + _dot(a2_ref[...], w2_ref[...])
    o_ref[...] = r_ref[...] + acc


def matmul_residual(a1, a2, blk1, blk2, w, res, *, tm, tn):
    m = res.shape[0]
    k, n = w.shape
    kh = k // 2
    return pl.pallas_call(
        _matmul_res_kernel,
        grid=(m // tm, n // tn),
        in_specs=[
            pl.BlockSpec((tm, kh), lambda i, j: (i, blk1)),
            pl.BlockSpec((tm, kh), lambda i, j: (i, blk2)),
            pl.BlockSpec((kh, tn), lambda i, j: (0, j)),
            pl.BlockSpec((kh, tn), lambda i, j: (1, j)),
            pl.BlockSpec((tm, tn), lambda i, j: (i, j)),
        ],
        out_specs=pl.BlockSpec((tm, tn), lambda i, j: (i, j)),
        out_shape=jax.ShapeDtypeStruct((m, n), F32),
        compiler_params=_params("parallel", "arbitrary"),
        name="matmul_residual",
    )(a1, a2, w, w, res)


def _ffn_kernel(x_ref, nw_ref, wg_ref, wu_ref, wd_ref, fw_ref, o_ref, h_ref, *, final_norm):
    j = pl.program_id(1)

    @pl.when(j == 0)
    def _():
        x = x_ref[...]
        ms = jnp.mean(x * x, axis=-1, keepdims=True)
        h_ref[...] = (x * lax.rsqrt(ms + EPS) * nw_ref[...]).astype(BF16)
        o_ref[...] = x

    h = h_ref[...]
    a = _dot(h, wg_ref[...])
    b = _dot(h, wu_ref[...])
    z = (_silu(a) * b).astype(BF16)
    o_ref[...] += _dot(z, wd_ref[...])

    if final_norm:
        @pl.when(j == pl.num_programs(1) - 1)
        def _():
            y = o_ref[...]
            ms = jnp.mean(y * y, axis=-1, keepdims=True)
            o_ref[...] = y * lax.rsqrt(ms + EPS) * fw_ref[...]


def ffn_block(x, nw, w_gu, w_down, final_w, *, tm, th, final_norm):
    m, d = x.shape
    hid = w_down.shape[0]
    nh = hid // th
    return pl.pallas_call(
        functools.partial(_ffn_kernel, final_norm=final_norm),
        grid=(m // tm, nh),
        in_specs=[
            pl.BlockSpec((tm, d), lambda i, j: (i, 0)),
            pl.BlockSpec((1, d), lambda i, j: (0, 0)),
            pl.BlockSpec((d, th), lambda i, j: (0, j)),
            pl.BlockSpec((d, th), lambda i, j: (0, j + nh)),
            pl.BlockSpec((th, d), lambda i, j: (j, 0)),
            pl.BlockSpec((1, d), lambda i, j: (0, 0)),
        ],
        out_specs=pl.BlockSpec((tm, d), lambda i, j: (i, 0)),
        out_shape=jax.ShapeDtypeStruct((m, d), F32),
        scratch_shapes=[pltpu.VMEM((tm, d), BF16)],
        compiler_params=_params("parallel", "arbitrary"),
        name="ffn_block",
    )(x, nw.reshape(1, d), w_gu, w_gu, w_down, final_w.reshape(1, d))


CONV_HALO = 32
CONV_ROWS = 16


def _conv_kernel(a_ref, g_ref, ah_ref, gh_ref, cw_ref, cb_ref, lw_ref, lb_ref, o_ref, hs_ref, sh_ref, cv_ref, *,
                 tiles_per_seq):
    ts = a_ref.shape[0]
    first = (pl.program_id(0) % tiles_per_seq) == 0
    hprev = ah_ref[...].astype(F32) * _sigmoid(gh_ref[...].astype(F32))
    hs_ref[0:CONV_HALO, :] = jnp.where(first, 0.0, hprev)
    hs_ref[CONV_HALO:CONV_HALO + ts, :] = a_ref[...].astype(F32) * _sigmoid(g_ref[...].astype(F32))
    off = CONV_HALO - (CONV_WIDTH - 1)
    n_shifted = sh_ref.shape[1]
    for r in range(1, 8):
        sh_ref[r - 1] = hs_ref[r:r + n_shifted, :]

    def body(i, carry):
        base = pl.multiple_of(i * CONV_ROWS, CONV_ROWS)
        accs = [jnp.zeros((CONV_ROWS // 8, 8, hs_ref.shape[1]), F32) + cb_ref[...],
                jnp.zeros((CONV_ROWS // 8, 8, hs_ref.shape[1]), F32)]
        for w in range(CONV_WIDTH):
            r, a = (off + w) % 8, (off + w) // 8 * 8
            src = hs_ref if r == 0 else sh_ref.at[r - 1]
            rows = src[pl.ds(base + a, CONV_ROWS), :].reshape(CONV_ROWS // 8, 8, -1)
            accs[w % 2] = accs[w % 2] + rows * cw_ref[w * 8:(w + 1) * 8, :]
        cv_ref[pl.ds(base, CONV_ROWS), :] = (accs[0] + accs[1]).reshape(CONV_ROWS, -1)
        return carry

    lax.fori_loop(0, ts // CONV_ROWS, body, 0)

    y = cv_ref[...]
    mu = jnp.mean(y, axis=-1, keepdims=True)
    d = y - mu
    var = jnp.mean(d * d, axis=-1, keepdims=True)
    hn = d * lax.rsqrt(var + EPS) * lw_ref[...] + lb_ref[...]
    o_ref[...] = _silu(hn).astype(o_ref.dtype)


def conformer_conv(u, conv_w, conv_b, ln_w, ln_b, *, seq, ts):
    t = u.shape[0]
    c = conv_w.shape[1]
    hb = ts // CONV_HALO
    return pl.pallas_call(
        functools.partial(_conv_kernel, tiles_per_seq=seq // ts),
        grid=(t // ts,),
        in_specs=[
            pl.BlockSpec((ts, c), lambda i: (i, 0)),
            pl.BlockSpec((ts, c), lambda i: (i, 1)),
            pl.BlockSpec((CONV_HALO, c), lambda i: (jnp.maximum(i * hb - 1, 0), 0)),
            pl.BlockSpec((CONV_HALO, c), lambda i: (jnp.maximum(i * hb - 1, 0), 1)),
            pl.BlockSpec((CONV_WIDTH * 8, c), lambda i: (0, 0)),
            pl.BlockSpec((1, c), lambda i: (0, 0)),
            pl.BlockSpec((1, c), lambda i: (0, 0)),
            pl.BlockSpec((1, c), lambda i: (0, 0)),
        ],
        out_specs=pl.BlockSpec((ts, c), lambda i: (i, 0)),
        out_shape=jax.ShapeDtypeStruct((t, c), BF16),
        scratch_shapes=[pltpu.VMEM((CONV_HALO + ts, c), F32), pltpu.VMEM((7, CONV_HALO + ts - 8, c), F32),
                        pltpu.VMEM((ts, c), F32)],
        compiler_params=_params("parallel"),
        name="conformer_conv",
    )(u, u, u, u, jnp.repeat(conv_w, 8, axis=0), conv_b.reshape(1, c), ln_w.reshape(1, c), ln_b.reshape(1, c))


def _compress_kernel(x_ref, pos_ref, w1_ref, b1_ref, w2_ref, b2_ref, o_ref, sh_ref):
    n = x_ref.shape[0]
    hw = x_ref.shape[1]
    x = x_ref[...].astype(F32)
    xa = (x + pos_ref[:, 0:hw]).astype(BF16)
    xb = (x + pos_ref[:, hw:2 * hw]).astype(BF16)
    p1 = _dot(xa, w1_ref[0:hw, :])
    sh_ref[0:n, :] = _dot(xb, w1_ref[hw:2 * hw, :])
    sh_ref[n:n + 8, :] = jnp.zeros((8, sh_ref.shape[1]), F32)
    hid = p1 + sh_ref[1:n + 1, :] + b1_ref[...]
    o_ref[...] = _dot(_silu(hid).astype(BF16), w2_ref[...]) + b2_ref[...]


def compress_tokens(xh, pos, w1, b1, w2, b2):
    _, bg, n, hw = xh.shape
    hid = w1.shape[-1]
    dh = w2.shape[-1]
    return pl.pallas_call(
        _compress_kernel,
        grid=(2, bg),
        in_specs=[
            pl.BlockSpec((None, None, n, hw), lambda s, i: (s, i, 0, 0)),
            pl.BlockSpec((None, 1, 2 * hw), lambda s, i: (s, 0, 0)),
            pl.BlockSpec((None, 2 * hw, hid), lambda s, i: (s, 0, 0)),
            pl.BlockSpec((None, 1, hid), lambda s, i: (s, 0, 0)),
            pl.BlockSpec((None, hid, dh), lambda s, i: (s, 0, 0)),
            pl.BlockSpec((None, 1, dh), lambda s, i: (s, 0, 0)),
        ],
        out_specs=pl.BlockSpec((None, None, n, dh), lambda s, i: (s, i, 0, 0)),
        out_shape=jax.ShapeDtypeStruct((2, bg, n, dh), F32),
        scratch_shapes=[pltpu.VMEM((n + 8, hid), F32)],
        compiler_params=_params("parallel", "parallel"),
        name="compress_tokens",
    )(xh, pos.reshape(2, 1, 2 * hw), w1.astype(BF16), b1.reshape(2, 1, hid), w2.astype(BF16),
      b2.reshape(2, 1, dh))


def _split_bf16(x):
    hi = x.astype(BF16)
    lo = (x - hi.astype(F32)).astype(BF16)
    return hi, lo


def _hgrn_kernel(q_ref, f_ref, i_ref, g_ref, gamma_ref, gn_ref, tril_ref, sum_ref, o_ref,
                 st_ref, gc_ref, k_ref, hk_ref, qg_ref, u_ref, dcat_ref, ps_ref, opart_ref, *, layer):
    c, sub = HG_CHUNK, HG_SUB
    nsub = c // sub
    tt, dk = q_ref.shape
    n_chunks = tt // c

    @pl.when(pl.program_id(2) == 0)
    def _():
        st_ref[...] = jnp.zeros_like(st_ref)

    gamma = gamma_ref[...]
    e = jnp.exp(gamma - jnp.max(gamma, axis=0, keepdims=True))
    sm = e / jnp.sum(e, axis=0, keepdims=True)
    lb = jnp.sum(sm[0:layer + 1, :], axis=0, keepdims=True) - sm[0:1, :]

    def gates(rows):
        f = lb + (1.0 - lb) * _sigmoid(f_ref[rows, :].astype(F32))
        k = 1.0 - f
        hi, lo = _split_bf16(jnp.log(jnp.maximum(f, TINY)))
        gc = _dot(tril_ref[...], hi) + _dot(tril_ref[...], lo)
        gc_ref[rows, :] = gc
        k_ref[rows, :] = k
        hk_ref[rows, :] = gc - jnp.log(k)
        qg_ref[rows, :] = (q_ref[rows, :].astype(F32) * jnp.exp(gc)).astype(BF16)

    r8 = lax.broadcasted_iota(jnp.int32, (1, 8, 1), 1)
    srow = lax.broadcasted_iota(jnp.int32, (c, 1), 0)
    tcol = lax.broadcasted_iota(jnp.int32, (c, c), 1)

    def half_major(x):
        return [jnp.concatenate([x[(2 * i + h) * 8:(2 * i + h + 1) * 8, :] for i in range(nsub)], axis=0)
                .reshape(nsub, 8, -1) for h in range(2)]

    def state_free_part(ci):
        rows = slice(ci * c, (ci + 1) * c)
        q = q_ref[rows, :].astype(F32)
        gc = gc_ref[rows, :]
        k = k_ref[rows, :]
        ivb = i_ref[rows, :]
        glast = gc[c - 1:c, :]
        u_ref[ci] = _dot_tn((k * jnp.exp(glast - gc)).astype(BF16), ivb)

        qhat = [jnp.zeros((sub, dk), F32)]
        khat = []
        for si in range(1, nsub):
            lo_r, hi_r = si * sub, (si + 1) * sub
            ref = gc[lo_r - 1:lo_r, :]
            qhat.append(q[lo_r:hi_r, :] * jnp.exp(gc[lo_r:hi_r, :] - ref))
            khat.append(k * jnp.exp(jnp.where(srow < lo_r, ref - gc, NEG)))
        at = _dot_nt(jnp.concatenate(khat, axis=0).astype(BF16), jnp.concatenate(qhat, axis=0).astype(BF16))
        at_off = jnp.zeros((c, c), F32)
        for si in range(1, nsub):
            at_off = jnp.where(tcol // sub == si, at[(si - 1) * c:si * c, :], at_off)
        opart_ref[rows, :] = _dot_tn(at_off.astype(BF16), ivb)

        lo_rows = slice(ci * c, ci * c + c // 2)
        hi_rows = slice(ci * c + c // 2, (ci + 1) * c)
        q0, q1 = half_major(q)
        g0, g1 = half_major(gc)
        h0, h1 = half_major(hk_ref[rows, :])
        for s in range(sub // 2):
            key = h0[:, s:s + 1, :]
            d0 = q0 * jnp.exp(jnp.where(r8 >= s, g0 - key, NEG))
            d1 = q1 * jnp.exp(g1 - key)
            dcat_ref[lo_rows, s * dk:(s + 1) * dk] = d0.reshape(c // 2, dk).astype(BF16)
            dcat_ref[hi_rows, s * dk:(s + 1) * dk] = d1.reshape(c // 2, dk).astype(BF16)
        dcat_ref[lo_rows, (sub // 2) * dk:] = jnp.zeros((c // 2, (sub // 2) * dk), BF16)
        for s in range(sub // 2):
            key = h1[:, s:s + 1, :]
            d1 = q1 * jnp.exp(jnp.where(r8 >= s, g1 - key, NEG))
            col = (sub // 2 + s) * dk
            dcat_ref[hi_rows, col:col + dk] = d1.reshape(c // 2, dk).astype(BF16)

    group = tril_ref.shape[0]
    for g0 in range(0, tt, group):
        grows = slice(g0, g0 + group)
        gates(grows)
        for ci in range(g0 // c, (g0 + group) // c):
            state_free_part(ci)
        ps_ref[grows, :] = _dot(dcat_ref[grows, :], sum_ref[...])

    glast = [gc_ref[(ci + 1) * c - 1:(ci + 1) * c, :] for ci in range(n_chunks)]
    glast += [jnp.zeros((1, dk), F32)] * (-n_chunks % 8)
    decay_cols = jnp.exp(jnp.concatenate(glast, axis=0)).T
    st = st_ref[...]
    for ci in range(n_chunks):
        rows = slice(ci * c, (ci + 1) * c)
        opart_ref[rows, :] += _dot(qg_ref[rows, :], st.astype(BF16))
        st = decay_cols[:, ci:ci + 1] * st + u_ref[ci]
    st_ref[...] = st

    own_block = (lax.broadcasted_iota(jnp.int32, (c, dk), 1) // sub
                 == (lax.broadcasted_iota(jnp.int32, (c, dk), 0) // 8) % nsub)
    for ci in range(n_chunks):
        rows = slice(ci * c, (ci + 1) * c)
        ivb = i_ref[rows, :]
        a = jnp.where(own_block, ps_ref[rows, :], 0.0)
        od = _dot(a.astype(BF16), jnp.concatenate([ivb, jnp.zeros((dk - c, ivb.shape[1]), BF16)], axis=0))
        o = opart_ref[rows, :] + jnp.concatenate([od[(h * nsub + i) * 8:(h * nsub + i + 1) * 8, :]
                                                  for i in range(nsub) for h in range(2)], axis=0)
        o = o * lax.rsqrt(jnp.mean(o * o, axis=-1, keepdims=True) + EPS) * gn_ref[...]
        o_ref[rows, :] = (o * _silu(g_ref[rows, :].astype(F32))).astype(o_ref.dtype)


def hgrn2(u, lb_gamma, gnorm_w, *, layer, batch, seq, tt):
    h, dk, dv = HG_HEADS, HG_DK, HG_DV
    nt = seq // tt
    depth = lb_gamma.shape[0]

    def col(off):
        return pl.BlockSpec((tt, dk), lambda b, hh, t: (b * nt + t, off + hh))

    r = jnp.arange(HG_GROUP * HG_CHUNK)
    tril = ((r[:, None] >= r[None, :]) & (r[:, None] // HG_CHUNK == r[None, :] // HG_CHUNK)).astype(BF16)
    key_offset = jnp.arange(HG_SUB * dk) // dk
    summer = (key_offset[:, None] == jnp.arange(dk)[None, :] % HG_SUB).astype(BF16)
    return pl.pallas_call(
        functools.partial(_hgrn_kernel, layer=layer),
        grid=(batch, h, nt),
        in_specs=[
            col(0), col(h), col(2 * h), col(3 * h),
            pl.BlockSpec((depth, dk), lambda b, hh, t: (0, hh)),
            pl.BlockSpec((1, dv), lambda b, hh, t: (0, hh)),
            pl.BlockSpec((HG_GROUP * HG_CHUNK, HG_GROUP * HG_CHUNK), lambda b, hh, t: (0, 0)),
            pl.BlockSpec((HG_SUB * dk, dk), lambda b, hh, t: (0, 0)),
        ],
        out_specs=pl.BlockSpec((tt, dv), lambda b, hh, t: (b * nt + t, hh)),
        out_shape=jax.ShapeDtypeStruct((batch * seq, h * dv), BF16),
        scratch_shapes=[
            pltpu.VMEM((dk, dv), F32),
            pltpu.VMEM((tt, dk), F32), pltpu.VMEM((tt, dk), F32), pltpu.VMEM((tt, dk), F32),
            pltpu.VMEM((tt, dk), BF16), pltpu.VMEM((tt // HG_CHUNK, dk, dv), F32),
            pltpu.VMEM((tt, HG_SUB * dk), BF16), pltpu.VMEM((tt, dk), F32), pltpu.VMEM((tt, dv), F32),
        ],
        compiler_params=_params("parallel", "parallel", "arbitrary"),
        name="hgrn2",
    )(u, u, u, u, lb_gamma, gnorm_w.reshape(1, h * dv), tril, summer)


QL = NSA_HPG * Q_BLOCK
FLAG_BITS = 16


def _tile_positions(qb):
    lane = lax.broadcasted_iota(jnp.int32, (1, QL), 1)
    return qb * Q_BLOCK + (lane & (Q_BLOCK - 1))


CMP_CHUNK = 128
SEL_CLASSES = 4
KEY_LANES = 128
SLOPE_PARTS = 3
LOG2E = 1.4426950408889634


def _load_query(qa_ref, q_ref, sa_ref):
    dh = q_ref.shape[0]
    qa_ref[0:dh, :] = (q_ref[...].astype(F32) * (NSA_HEAD_DIM ** -0.5 * LOG2E)).astype(BF16)
    qa_ref[dh:, :] = sa_ref[...]


def _nsa_select_body(nch, q_ref, sa_ref, kc_ref, vct_ref, slope_ref, oc_ref, sel_ref, flag_ref,
                     qa_ref, s_ref, pb_ref, p_ref):
    ck = CMP_CHUNK
    nk = nch * ck
    ns = nk * CMP_STRIDE // SLC_BLOCK
    qb = pl.program_id(2)
    _load_query(qa_ref, q_ref, sa_ref)
    t = _tile_positions(qb)
    slope2 = slope_ref[...] * LOG2E

    rows = []
    mc = None
    for ch in range(nch):
        rs = slice(ch * ck, (ch + 1) * ck)
        s = _dot(kc_ref[rs, :], qa_ref[...])
        end = (ch * ck + lax.broadcasted_iota(jnp.int32, (ck, 1), 0)) * CMP_STRIDE + (CMP_BLOCK - 1)
        s = jnp.where(t >= end, s, NEG)
        s_ref[rs, :] = s
        brow = slope2 * (ch * ck * CMP_STRIDE + (CMP_BLOCK - 1) - qb * Q_BLOCK).astype(F32)
        rows.append(brow)
        cm = jnp.max(s.reshape(ck // 8, 8, QL), axis=0) + brow
        mc = cm if mc is None else jnp.maximum(mc, cm)
    m = jnp.max(mc, axis=0, keepdims=True)

    lsum = None
    for ch in range(nch):
        rs = slice(ch * ck, (ch + 1) * ck)
        p = jnp.exp2(s_ref[rs, :] - (m - rows[ch]))
        ls = jnp.sum(p.reshape(ck // 8, 8, QL), axis=0)
        lsum = ls if lsum is None else lsum + ls
        s_ref[rs, :] = p
        pb_ref[rs, :] = p.astype(BF16)
    l = jnp.sum(lsum, axis=0, keepdims=True)
    inv = jnp.where(t >= CMP_BLOCK - 1, 1.0 / jnp.maximum(l, TINY), 0.0)
    oc_ref[...] = _dot(vct_ref[:, 0:nk], pb_ref[0:nk, :]) * inv

    p_ref[0:8, :] = jnp.zeros((8, Q_BLOCK), F32)
    for ch in range(nch):
        pn = s_ref[ch * ck:(ch + 1) * ck, :] * inv
        psum = pn[:, 0:Q_BLOCK]
        for h in range(1, NSA_HPG):
            psum = psum + pn[:, h * Q_BLOCK:(h + 1) * Q_BLOCK]
        p_ref[8 + ch * ck:8 + (ch + 1) * ck, :] = psum
    p_ref[8 + nk:16 + nk, :] = jnp.zeros((8, Q_BLOCK), F32)

    ratio = SLC_BLOCK // CMP_STRIDE
    imp = p_ref[pl.ds(7, ns, stride=ratio), :]
    for r in range(1, ratio + 1):
        imp = imp + p_ref[pl.ds(7 + r, ns, stride=ratio), :]

    j = lax.broadcasted_iota(jnp.int32, (ns, 1), 0)
    cur = t[:, 0:Q_BLOCK] // SLC_BLOCK
    forced = (j == 0) | (j == cur) | (j == cur - 1)
    imp = jnp.where(forced, FORCE_SCORE, jnp.where(j > cur, -1.0, imp))
    for _ in range(min(SLC_TOPK, ns)):
        mx = jnp.max(imp, axis=0, keepdims=True)
        idx = jnp.min(jnp.where(imp == mx, j, ns), axis=0, keepdims=True)
        imp = jnp.where(j == idx, NEG, imp)
    sel = jnp.where((imp == NEG) & (j <= cur), 1.0, 0.0)
    sel_ref[0:ns, :] = sel
    if ns < sel_ref.shape[0]:
        sel_ref[ns:, :] = jnp.zeros((sel_ref.shape[0] - ns, Q_BLOCK), F32)

    cnt = _dot_nt(jnp.ones((8, Q_BLOCK), BF16), sel.astype(BF16))
    flags = jnp.where(cnt > 0.0, 1.0, 0.0).astype(BF16)
    jj = lax.broadcasted_iota(jnp.int32, (ns, 128), 0)
    ww = lax.broadcasted_iota(jnp.int32, (ns, 128), 1)
    bit = jnp.left_shift(1, jj & (FLAG_BITS - 1)).astype(F32)
    pack = jnp.where(ww == jj // FLAG_BITS, bit, 0.0).astype(BF16)
    flag_ref[...] = _dot(flags, pack)


def _nsa_select_kernel(q_ref, sa_ref, kc_ref, *rest):
    n_chunks = kc_ref.shape[0] // CMP_CHUNK
    n_cls = min(SEL_CLASSES, n_chunks)
    cls = pl.program_id(2) * n_cls // pl.num_programs(2)
    for c in range(n_cls):
        @pl.when(cls == c)
        def _(c=c):
            _nsa_select_body((c + 1) * n_chunks // n_cls, q_ref, sa_ref, kc_ref, *rest)


def nsa_select(qt, slope_rows, kc, vct, slopes):
    b, g, nqb, dh, ql = qt.shape
    nc = kc.shape[2]
    ns = nc * CMP_STRIDE // SLC_BLOCK
    tile = lambda *shape: pl.BlockSpec((None, None, None) + shape, lambda bi, gi, qi: (bi, gi, qi, 0, 0))
    return pl.pallas_call(
        _nsa_select_kernel,
        grid=(b, g, nqb),
        in_specs=[
            tile(dh, ql),
            pl.BlockSpec((None, KEY_LANES - dh, ql), lambda bi, gi, qi: (gi, 0, 0)),
            pl.BlockSpec((None, None, nc, KEY_LANES), lambda bi, gi, qi: (bi, gi, 0, 0)),
            pl.BlockSpec((None, None, dh, nc), lambda bi, gi, qi: (bi, gi, 0, 0)),
            pl.BlockSpec((None, 1, ql), lambda bi, gi, qi: (gi, 0, 0)),
        ],
        out_specs=[tile(dh, ql), tile(ns, Q_BLOCK), tile(8, 128)],
        out_shape=[
            jax.ShapeDtypeStruct((b, g, nqb, dh, ql), F32),
            jax.ShapeDtypeStruct((b, g, nqb, ns, Q_BLOCK), F32),
            jax.ShapeDtypeStruct((b, g, nqb, 8, 128), F32),
        ],
        scratch_shapes=[
            pltpu.VMEM((KEY_LANES, ql), BF16), pltpu.VMEM((nc, ql), F32), pltpu.VMEM((nc, ql), BF16),
            pltpu.VMEM((nc + 16, Q_BLOCK), F32),
        ],
        compiler_params=_params("parallel", "parallel", "parallel"),
        name="nsa_select",
    )(qt, slope_rows, kc, vct, slopes)


ATT_GROUP = 8
LIST_PAD = 3 * ATT_GROUP
DIAG_BLOCKS = Q_BLOCK // SLC_BLOCK
WIN_BLOCKS = WINDOW // SLC_BLOCK


def _score_blocks(k_src, blocks, qa_ref, kcat_ref, s_ref):
    kb = SLC_BLOCK
    for b, j in enumerate(blocks):
        kcat_ref[b * kb:(b + 1) * kb, :] = k_src[jnp.maximum(j, 0)]
    n = len(blocks) * kb
    s_ref[0:n, :] = _dot(kcat_ref[0:n, :], qa_ref[...])


def _softmax_step(v_src, blocks, masks, sel_rows, t, slope2, blk0, s_ref, vcat_ref, p_ref, m_ref, l_ref, acc_ref):
    nb = len(blocks)
    kb = SLC_BLOCK
    for b, j in enumerate(blocks):
        vcat_ref[b * kb:(b + 1) * kb, :] = v_src[jnp.maximum(j, 0)]
    rows = []
    mc = None
    for b, j in enumerate(blocks):
        brow = slope2 * ((j - blk0) * kb).astype(F32)
        if sel_rows[b] is not None:
            brow = jnp.where(sel_rows[b] > 0.0, brow, NEG)
        brow = jnp.where(j >= 0, brow, NEG)
        rows.append(brow)
        s = s_ref[b * kb:(b + 1) * kb, :]
        if masks[b] is not None:
            kpos = j * kb + lax.broadcasted_iota(jnp.int32, (kb, 1), 0)
            valid = (t >= kpos) if masks[b] == "causal" else (t - kpos < WINDOW)
            s = jnp.where(valid, s, NEG)
            s_ref[b * kb:(b + 1) * kb, :] = s
        cm = jnp.max(s.reshape(kb // 8, 8, QL), axis=0) + brow
        mc = cm if mc is None else jnp.maximum(mc, cm)
    m_old = m_ref[...]
    m_new = jnp.maximum(m_old, jnp.max(mc, axis=0, keepdims=True))
    alpha = jnp.exp2(m_old - m_new)
    m_ref[...] = m_new

    lsum = None
    for b in range(nb):
        p = jnp.exp2(s_ref[b * kb:(b + 1) * kb, :] - (m_new - rows[b]))
        ls = jnp.sum(p.reshape(kb // 8, 8, QL), axis=0)
        lsum = ls if lsum is None else lsum + ls
        p_ref[b * kb:(b + 1) * kb, :] = p.astype(BF16)
    l_ref[...] = alpha * l_ref[...] + lsum
    acc_ref[...] = alpha * acc_ref[...] + _dot_tn(vcat_ref[0:nb * kb, :], p_ref[0:nb * kb, :])


def _nsa_attend_kernel(fw_ref, q_ref, sa_ref, ks_ref, vs_ref, kw_ref, vw_ref, sel_ref, oc_ref, gate_ref, slope_ref,
                       o_ref, qa_ref, kcat_ref, s_ref, vcat_ref, p_ref, m_ref, l_ref, acc_ref,
                       kcat_w_ref, s_w_ref, vcat_w_ref, p_w_ref, m_w_ref, l_w_ref, acc_w_ref, list_ref, *,
                       words_per_tile):
    bi, gi, qb = pl.program_id(0), pl.program_id(1), pl.program_id(2)
    tile_id = (bi * pl.num_programs(1) + gi) * pl.num_programs(2) + qb
    _load_query(qa_ref, q_ref, sa_ref)
    t = _tile_positions(qb)
    slope2 = slope_ref[...] * LOG2E
    blk0 = qb * DIAG_BLOCKS
    diag = [blk0 + i for i in range(DIAG_BLOCKS)]
    past = [blk0 - WIN_BLOCKS + i for i in range(WIN_BLOCKS)]

    def sel_row(j):
        row = sel_ref[pl.ds(jnp.maximum(j, 0), 1), :]
        return jnp.concatenate([row] * NSA_HPG, axis=1)

    def reset(m, l, acc):
        m[...] = jnp.full_like(m, NEG)
        l[...] = jnp.zeros_like(l)
        acc[...] = jnp.zeros_like(acc)

    def result(l, acc):
        return acc[...] * (1.0 / jnp.maximum(jnp.sum(l[...], axis=0, keepdims=True), TINY))

    def scan_flags(w, n):
        base = w * FLAG_BITS
        word = fw_ref[tile_id * words_per_tile + w] & ((1 << jnp.minimum(blk0 - base, FLAG_BITS)) - 1)
        for i in range(FLAG_BITS):
            list_ref[n] = base + i
            n = n + ((word >> i) & 1)
        return n

    n_sel = lax.fori_loop(0, (blk0 + FLAG_BITS - 1) // FLAG_BITS, scan_flags, 0)
    for i in range(LIST_PAD):
        list_ref[n_sel + i] = -1

    def listed(i):
        return [list_ref[i * ATT_GROUP + b] for b in range(ATT_GROUP)]

    _score_blocks(kw_ref, past + diag, qa_ref, kcat_w_ref, s_w_ref)
    _score_blocks(ks_ref, diag, qa_ref, kcat_ref.at[0], s_ref.at[0])
    _score_blocks(ks_ref, listed(0), qa_ref, kcat_ref.at[1], s_ref.at[1])

    slc_state = (m_ref, l_ref, acc_ref)
    reset(*slc_state)
    _softmax_step(vs_ref, diag, ["causal"] * DIAG_BLOCKS, [sel_row(j) for j in diag], t, slope2, blk0,
                  s_ref.at[0], vcat_ref, p_ref, *slc_state)

    win_state = (m_w_ref, l_w_ref, acc_w_ref)
    reset(*win_state)
    masks = ["window"] * DIAG_BLOCKS + [None] * (WIN_BLOCKS - DIAG_BLOCKS) + ["causal"] * DIAG_BLOCKS
    _softmax_step(vw_ref, past + diag, masks, [None] * len(masks), t, slope2, blk0,
                  s_w_ref, vcat_w_ref, p_w_ref, *win_state)

    n_steps = (n_sel + ATT_GROUP - 1) // ATT_GROUP

    def slc_step(i, slot):
        _score_blocks(ks_ref, listed(i + 1), qa_ref, kcat_ref.at[slot], s_ref.at[slot])
        blocks = listed(i)
        _softmax_step(vs_ref, blocks, [None] * ATT_GROUP, [sel_row(j) for j in blocks], t, slope2, blk0,
                      s_ref.at[1 - slot], vcat_ref, p_ref, *slc_state)

    def slc_body(i2, carry):
        slc_step(2 * i2, 0)

        @pl.when(2 * i2 + 1 < n_steps)
        def _():
            slc_step(2 * i2 + 1, 1)

        return carry

    lax.fori_loop(0, (n_steps + 1) // 2, slc_body, 0)
    o_slc = result(l_ref, acc_ref)
    o_win = result(l_w_ref, acc_w_ref)

    gate = _sigmoid(gate_ref[...])
    o_ref[...] = (gate[0:1, :] * oc_ref[...] + gate[1:2, :] * o_slc + gate[2:3, :] * o_win).astype(o_ref.dtype)


def nsa_attend(flag_words, qt, slope_rows, ks, vs, kw, vw, sel, oc, gates, slopes):
    b, g, nqb, dh, ql = qt.shape
    ns = ks.shape[2]
    gkeys = ATT_GROUP * SLC_BLOCK
    wkeys = (WIN_BLOCKS + DIAG_BLOCKS) * SLC_BLOCK
    tile = lambda *shape: pl.BlockSpec((None, None, None) + shape, lambda bi, gi, qi, fw: (bi, gi, qi, 0, 0))
    seq = lambda *shape: pl.BlockSpec((None, None) + shape, lambda bi, gi, qi, fw: (bi, gi, 0, 0, 0))
    state = [pltpu.VMEM((1, ql), F32), pltpu.VMEM((8, ql), F32), pltpu.VMEM((dh, ql), F32)]
    grid_spec = pltpu.PrefetchScalarGridSpec(
        num_scalar_prefetch=1,
        grid=(b, g, nqb),
        in_specs=[
            tile(dh, ql),
            pl.BlockSpec((None, KEY_LANES - dh, ql), lambda bi, gi, qi, fw: (gi, 0, 0)),
            seq(ns, SLC_BLOCK, KEY_LANES), seq(ns, SLC_BLOCK, dh),
            seq(ns, SLC_BLOCK, KEY_LANES), seq(ns, SLC_BLOCK, dh),
            tile(ns, Q_BLOCK), tile(dh, ql), tile(3, ql),
            pl.BlockSpec((None, 1, ql), lambda bi, gi, qi, fw: (gi, 0, 0)),
        ],
        out_specs=tile(dh, ql),
        scratch_shapes=[
            pltpu.VMEM((KEY_LANES, ql), BF16),
            pltpu.VMEM((2, gkeys, KEY_LANES), BF16), pltpu.VMEM((2, gkeys, ql), F32),
            pltpu.VMEM((gkeys, dh), BF16), pltpu.VMEM((gkeys, ql), BF16), *state,
            pltpu.VMEM((wkeys, KEY_LANES), BF16), pltpu.VMEM((wkeys, ql), F32),
            pltpu.VMEM((wkeys, dh), BF16), pltpu.VMEM((wkeys, ql), BF16), *state,
            pltpu.SMEM((ns + LIST_PAD,), jnp.int32),
        ],
    )
    return pl.pallas_call(
        functools.partial(_nsa_attend_kernel, words_per_tile=ns // FLAG_BITS),
        grid_spec=grid_spec,
        out_shape=jax.ShapeDtypeStruct((b, g, nqb, dh, ql), BF16),
        compiler_params=_params("parallel", "parallel", "arbitrary"),
        name="nsa_attend",
    )(flag_words, qt, slope_rows, ks, vs, kw, vw, sel, oc, gates, slopes)


def nsa_mixer(q, kc, vc, ks, vs, kw, vw, gate_logits, cmp_pos, cmp_w1, cmp_b1, cmp_w2, cmp_b2, *, batch, seq):
    g, hpg, dh = NSA_KV_GROUPS, NSA_HPG, NSA_HEAD_DIM
    nqb, ns, nc = seq // Q_BLOCK, seq // SLC_BLOCK, seq // CMP_STRIDE

    qt = q.reshape(batch, nqb, Q_BLOCK, g, hpg, dh).transpose(0, 3, 1, 5, 4, 2).reshape(batch, g, nqb, dh, QL)
    gates = gate_logits.astype(F32).reshape(batch, nqb, Q_BLOCK, 3, g, hpg).transpose(0, 4, 1, 3, 5, 2)
    gates = gates.reshape(batch, g, nqb, 3, QL)
    slopes = 2.0 ** (-8.0 * jnp.arange(1, NSA_HEADS + 1, dtype=F32) / NSA_HEADS)
    slopes = jnp.repeat(slopes.reshape(g, 1, hpg), Q_BLOCK, axis=2)

    def halves(x):
        x = x.reshape(batch, nc, CMP_STRIDE, g, dh).transpose(0, 3, 1, 2, 4)
        return x.reshape(batch * g, nc, CMP_STRIDE * dh)

    def value_blocks(x):
        return x.reshape(batch, ns, SLC_BLOCK, g, dh).transpose(0, 3, 1, 2, 4)

    def key_blocks(x):
        r = jnp.arange(SLC_BLOCK, dtype=F32).astype(BF16).reshape(1, 1, 1, SLC_BLOCK, 1)
        r = jnp.broadcast_to(r, (batch, g, ns, SLC_BLOCK, SLOPE_PARTS))
        pad = jnp.zeros((batch, g, ns, SLC_BLOCK, KEY_LANES - dh - SLOPE_PARTS), BF16)
        return jnp.concatenate([value_blocks(x), r, pad], axis=-1)

    s2 = slopes * LOG2E
    parts = []
    for _ in range(SLOPE_PARTS):
        part = s2.astype(BF16)
        parts.append(part)
        s2 = s2 - part.astype(F32)
    slope_rows = jnp.concatenate(parts + [jnp.zeros((g, KEY_LANES - dh - SLOPE_PARTS, QL), BF16)], axis=1)

    cmp = compress_tokens(jnp.stack([halves(kc), halves(vc)]), cmp_pos, cmp_w1, cmp_b1, cmp_w2, cmp_b2)
    cmp = cmp.astype(BF16).reshape(2, batch, g, nc, dh)
    r = (jnp.arange(nc) % CMP_CHUNK * CMP_STRIDE).astype(BF16).reshape(1, 1, nc, 1)
    k_cmp = jnp.concatenate([cmp[0], jnp.broadcast_to(r, (batch, g, nc, SLOPE_PARTS)),
                             jnp.zeros((batch, g, nc, KEY_LANES - dh - SLOPE_PARTS), BF16)], axis=-1)
    v_cmp_t = cmp[1].transpose(0, 1, 3, 2)

    oc, sel, flags = nsa_select(qt, slope_rows, k_cmp, v_cmp_t, slopes)
    flag_words = flags[:, :, :, 0, :ns // FLAG_BITS].astype(jnp.int32).reshape(-1)
    ot = nsa_attend(flag_words, qt, slope_rows, key_blocks(ks), value_blocks(vs), key_blocks(kw),
                    value_blocks(vw), sel, oc, gates, slopes)
    ot = ot.reshape(batch, g, nqb, dh, hpg, Q_BLOCK).transpose(0, 2, 5, 1, 4, 3)
    return ot.reshape(batch * seq, g * hpg * dh)


ROW_TILE = 1024
COL_TILE = 1024
FFN_ROW_TILE = 1024
FFN_HID_TILE = 512
CONV_SEQ_TILE = 512
HGRN_SEQ_TILE = 512


def _even_layer(x, nw, w_in, conv_w, conv_b, ln_w, ln_b, cmp_pos, cmp_w1, cmp_b1, cmp_w2, cmp_b2, w_out, *,
                batch, seq):
    c = conv_w.shape[-1]
    nq = NSA_HEADS * NSA_HEAD_DIM
    nkv = NSA_KV_GROUPS * NSA_HEAD_DIM
    n_in = w_in.shape[1]
    n_pad = -(-n_in // COL_TILE) * COL_TILE
    w_in = jnp.pad(w_in, ((0, 0), (0, n_pad - n_in))).astype(BF16)
    u = norm_matmul(x, nw, w_in, tm=ROW_TILE, tn=COL_TILE)
    a_out = conformer_conv(u, conv_w.reshape(CONV_WIDTH, c), conv_b, ln_w, ln_b, seq=seq, ts=CONV_SEQ_TILE)
    off = 2 * c
    q = u[:, off:off + nq]
    off += nq
    kvs = [u[:, off + i * nkv:off + (i + 1) * nkv] for i in range(6)]
    off += 6 * nkv
    gate_logits = u[:, off:off + 3 * NSA_HEADS]
    b_out = nsa_mixer(q, *kvs, gate_logits, cmp_pos, cmp_w1, cmp_b1, cmp_w2, cmp_b2, batch=batch, seq=seq)
    return matmul_residual(a_out, b_out, 0, 0, w_out.astype(BF16), x, tm=ROW_TILE, tn=COL_TILE)


def _odd_layer(x, nw, w_in, lb_gamma, gnorm_w, w_out, *, layer, batch, seq):
    u = norm_matmul(x, nw, w_in.astype(BF16), tm=ROW_TILE, tn=COL_TILE)
    o = hgrn2(u, lb_gamma, gnorm_w, layer=layer, batch=batch, seq=seq, tt=HGRN_SEQ_TILE)
    return matmul_residual(o, o, 0, 1, w_out.astype(BF16), x, tm=ROW_TILE, tn=COL_TILE)


def kernel(x, norm_w, final_norm_w, ev_w_in, ev_conv_w, ev_conv_b, ev_conv_ln_w, ev_conv_ln_b, ev_cmp_pos,
           ev_cmp_w1, ev_cmp_b1, ev_cmp_w2, ev_cmp_b2, ev_w_out, od_w_in, od_lb_gamma, od_gnorm_w, od_w_out,
           ffn_w_gu, ffn_w_down):
    batch, seq, d = x.shape
    depth = norm_w.shape[0]
    xs = x.reshape(batch * seq, d)
    for layer in range(depth):
        i = layer // 2
        if layer % 2 == 0:
            xs = _even_layer(xs, norm_w[layer, 0], ev_w_in[i], ev_conv_w[i], ev_conv_b[i], ev_conv_ln_w[i],
                             ev_conv_ln_b[i], ev_cmp_pos[i], ev_cmp_w1[i], ev_cmp_b1[i], ev_cmp_w2[i],
                             ev_cmp_b2[i], ev_w_out[i], batch=batch, seq=seq)
        else:
            xs = _odd_layer(xs, norm_w[layer, 0], od_w_in[i], od_lb_gamma.astype(F32), od_gnorm_w[i],
                            od_w_out[i], layer=layer, batch=batch, seq=seq)
        xs = ffn_block(xs, norm_w[layer, 1], ffn_w_gu[layer].astype(BF16), ffn_w_down[layer].astype(BF16),
                       final_norm_w, tm=FFN_ROW_TILE, th=FFN_HID_TILE, final_norm=layer == depth - 1)
    return xs.reshape(batch, seq, d)
```

```python
import functools

import jax
import jax.numpy as jnp
import numpy as np
from jax import lax
from jax.experimental import pallas as pl
from jax.experimental.pallas import tpu as pltpu

F32 = jnp.float32
BF16 = jnp.bfloat16

EPS = 1e-6
TINY = 1e-30
NEG = -1e30

VMEM_LIMIT_BYTES = 56 * 1024 * 1024

CONV_WIDTH = 31
NSA_HEADS = 16
NSA_HEAD_DIM = 64
NSA_KV_GROUPS = 4
NSA_HPG = NSA_HEADS // NSA_KV_GROUPS
CMP_STRIDE = 16
CMP_BLOCK = 32
SLC_BLOCK = 64
SLC_TOPK = 16
WINDOW = 512
Q_BLOCK = 128
FORCE_SCORE = 1.0e3
HG_HEADS = 16
HG_DK = 128
HG_DV = 128
HG_CHUNK = 64
HG_SUB = 16
HG_GROUP = 4


def _params(*sem):
    return pltpu.CompilerParams(dimension_semantics=sem, vmem_limit_bytes=VMEM_LIMIT_BYTES)


def _sigmoid(x):
    return 1.0 / (1.0 + jnp.exp(-x))


def _silu(x):
    return x * _sigmoid(x)


def _dot(a, b):
    return jnp.dot(a, b, preferred_element_type=F32)


def _dot_nt(a, b):
    return lax.dot_general(a, b, (((1,), (1,)), ((), ())), preferred_element_type=F32)


def _dot_tn(a, b):
    return lax.dot_general(a, b, (((0,), (0,)), ((), ())), preferred_element_type=F32)


def _norm_matmul_kernel(x_ref, nw_ref, w_ref, o_ref, h_ref):
    @pl.when(pl.program_id(1) == 0)
    def _():
        x = x_ref[...]
        ms = jnp.mean(x * x, axis=-1, keepdims=True)
        h_ref[...] = (x * lax.rsqrt(ms + EPS) * nw_ref[...]).astype(BF16)

    o_ref[...] = _dot(h_ref[...], w_ref[...]).astype(o_ref.dtype)


def norm_matmul(x, nw, w, *, tm, tn, out_dtype=BF16):
    m, k = x.shape
    n = w.shape[1]
    return pl.pallas_call(
        _norm_matmul_kernel,
        grid=(m // tm, n // tn),
        in_specs=[
            pl.BlockSpec((tm, k), lambda i, j: (i, 0)),
            pl.BlockSpec((1, k), lambda i, j: (0, 0)),
            pl.BlockSpec((k, tn), lambda i, j: (0, j)),
        ],
        out_specs=pl.BlockSpec((tm, tn), lambda i, j: (i, j)),
        out_shape=jax.ShapeDtypeStruct((m, n), out_dtype),
        scratch_shapes=[pltpu.VMEM((tm, k), BF16)],
        compiler_params=_params("parallel", "arbitrary"),
        name="norm_matmul",
    )(x, nw.reshape(1, k), w)


def _matmul_res_kernel(a1_ref, a2_ref, w1_ref, w2_ref, r_ref, o_ref):
    acc = _dot(a1_ref[...], w1_ref[...]) + _dot(a2_ref[...], w2_ref[...])
    o_ref[...] = r_ref[...] + acc


def matmul_residual(a1, a2, blk1, blk2, w, res, *, tm, tn):
    m = res.shape[0]
    k, n = w.shape
    kh = k // 2
    return pl.pallas_call(
        _matmul_res_kernel,
        grid=(m // tm, n // tn),
        in_specs=[
            pl.BlockSpec((tm, kh), lambda i, j: (i, blk1)),
            pl.BlockSpec((tm, kh), lambda i, j: (i, blk2)),
            pl.BlockSpec((kh, tn), lambda i, j: (0, j)),
            pl.BlockSpec((kh, tn), lambda i, j: (1, j)),
            pl.BlockSpec((tm, tn), lambda i, j: (i, j)),
        ],
        out_specs=pl.BlockSpec((tm, tn), lambda i, j: (i, j)),
        out_shape=jax.ShapeDtypeStruct((m, n), F32),
        compiler_params=_params("parallel", "arbitrary"),
        name="matmul_residual",
    )(a1, a2, w, w, res)


def _ffn_kernel(x_ref, nw_ref, wg_ref, wu_ref, wd_ref, fw_ref, o_ref, h_ref, *, final_norm):
    j = pl.program_id(1)

    @pl.when(j == 0)
    def _():
        x = x_ref[...]
        ms = jnp.mean(x * x, axis=-1, keepdims=True)
        h_ref[...] = (x * lax.rsqrt(ms + EPS) * nw_ref[...]).astype(BF16)
        o_ref[...] = x

    h = h_ref[...]
    a = _dot(h, wg_ref[...])
    b = _dot(h, wu_ref[...])
    z = (_silu(a) * b).astype(BF16)
    o_ref[...] += _dot(z, wd_ref[...])

    if final_norm:
        @pl.when(j == pl.num_programs(1) - 1)
        def _():
            y = o_ref[...]
            ms = jnp.mean(y * y, axis=-1, keepdims=True)
            o_ref[...] = y * lax.rsqrt(ms + EPS) * fw_ref[...]


def ffn_block(x, nw, w_gu, w_down, final_w, *, tm, th, final_norm):
    m, d = x.shape
    hid = w_down.shape[0]
    nh = hid // th
    return pl.pallas_call(
        functools.partial(_ffn_kernel, final_norm=final_norm),
        grid=(m // tm, nh),
        in_specs=[
            pl.BlockSpec((tm, d), lambda i, j: (i, 0)),
            pl.BlockSpec((1, d), lambda i, j: (0, 0)),
            pl.BlockSpec((d, th), lambda i, j: (0, j)),
            pl.BlockSpec((d, th), lambda i, j: (0, j + nh)),
            pl.BlockSpec((th, d), lambda i, j: (j, 0)),
            pl.BlockSpec((1, d), lambda i, j: (0, 0)),
        ],
        out_specs=pl.BlockSpec((tm, d), lambda i, j: (i, 0)),
        out_shape=jax.ShapeDtypeStruct((m, d), F32),
        scratch_shapes=[pltpu.VMEM((tm, d), BF16)],
        compiler_params=_params("parallel", "arbitrary"),
        name="ffn_block",
    )(x, nw.reshape(1, d), w_gu, w_gu, w_down, final_w.reshape(1, d))


CONV_HALO = 32
CONV_ROWS = 16


def _conv_kernel(a_ref, g_ref, ah_ref, gh_ref, cw_ref, cb_ref, lw_ref, lb_ref, o_ref, hs_ref, sh_ref, cv_ref, *,
                 tiles_per_seq):
    ts = a_ref.shape[0]
    first = (pl.program_id(0) % tiles_per_seq) == 0
    hprev = ah_ref[...].astype(F32) * _sigmoid(gh_ref[...].astype(F32))
    hs_ref[0:CONV_HALO, :] = jnp.where(first, 0.0, hprev)
    hs_ref[CONV_HALO:CONV_HALO + ts, :] = a_ref[...].astype(F32) * _sigmoid(g_ref[...].astype(F32))
    off = CONV_HALO - (CONV_WIDTH - 1)
    n_shifted = sh_ref.shape[1]
    for r in range(1, 8):
        sh_ref[r - 1] = hs_ref[r:r + n_shifted, :]

    def body(i, carry):
        base = pl.multiple_of(i * CONV_ROWS, CONV_ROWS)
        accs = [jnp.zeros((CONV_ROWS // 8, 8, hs_ref.shape[1]), F32) + cb_ref[...],
                jnp.zeros((CONV_ROWS // 8, 8, hs_ref.shape[1]), F32)]
        for w in range(CONV_WIDTH):
            r, a = (off + w) % 8, (off + w) // 8 * 8
            src = hs_ref if r == 0 else sh_ref.at[r - 1]
            rows = src[pl.ds(base + a, CONV_ROWS), :].reshape(CONV_ROWS // 8, 8, -1)
            accs[w % 2] = accs[w % 2] + rows * cw_ref[w * 8:(w + 1) * 8, :]
        cv_ref[pl.ds(base, CONV_ROWS), :] = (accs[0] + accs[1]).reshape(CONV_ROWS, -1)
        return carry

    lax.fori_loop(0, ts // CONV_ROWS, body, 0)

    y = cv_ref[...]
    mu = jnp.mean(y, axis=-1, keepdims=True)
    d = y - mu
    var = jnp.mean(d * d, axis=-1, keepdims=True)
    hn = d * lax.rsqrt(var + EPS) * lw_ref[...] + lb_ref[...]
    o_ref[...] = _silu(hn).astype(o_ref.dtype)


def conformer_conv(u, conv_w, conv_b, ln_w, ln_b, *, seq, ts):
    t = u.shape[0]
    c = conv_w.shape[1]
    hb = ts // CONV_HALO
    return pl.pallas_call(
        functools.partial(_conv_kernel, tiles_per_seq=seq // ts),
        grid=(t // ts,),
        in_specs=[
            pl.BlockSpec((ts, c), lambda i: (i, 0)),
            pl.BlockSpec((ts, c), lambda i: (i, 1)),
            pl.BlockSpec((CONV_HALO, c), lambda i: (jnp.maximum(i * hb - 1, 0), 0)),
            pl.BlockSpec((CONV_HALO, c), lambda i: (jnp.maximum(i * hb - 1, 0), 1)),
            pl.BlockSpec((CONV_WIDTH * 8, c), lambda i: (0, 0)),
            pl.BlockSpec((1, c), lambda i: (0, 0)),
            pl.BlockSpec((1, c), lambda i: (0, 0)),
            pl.BlockSpec((1, c), lambda i: (0, 0)),
        ],
        out_specs=pl.BlockSpec((ts, c), lambda i: (i, 0)),
        out_shape=jax.ShapeDtypeStruct((t, c), BF16),
        scratch_shapes=[pltpu.VMEM((CONV_HALO + ts, c), F32), pltpu.VMEM((7, CONV_HALO + ts - 8, c), F32),
                        pltpu.VMEM((ts, c), F32)],
        compiler_params=_params("parallel"),
        name="conformer_conv",
    )(u, u, u, u, jnp.repeat(conv_w, 8, axis=0), conv_b.reshape(1, c), ln_w.reshape(1, c), ln_b.reshape(1, c))


def _compress_kernel(x_ref, pos_ref, w1_ref, b1_ref, w2_ref, b2_ref, o_ref, sh_ref):
    n = x_ref.shape[0]
    hw = x_ref.shape[1]
    x = x_ref[...].astype(F32)
    xa = (x + pos_ref[:, 0:hw]).astype(BF16)
    xb = (x + pos_ref[:, hw:2 * hw]).astype(BF16)
    p1 = _dot(xa, w1_ref[0:hw, :])
    sh_ref[0:n, :] = _dot(xb, w1_ref[hw:2 * hw, :])
    sh_ref[n:n + 8, :] = jnp.zeros((8, sh_ref.shape[1]), F32)
    hid = p1 + sh_ref[1:n + 1, :] + b1_ref[...]
    o_ref[...] = _dot(_silu(hid).astype(BF16), w2_ref[...]) + b2_ref[...]


def compress_tokens(xh, pos, w1, b1, w2, b2):
    _, bg, n, hw = xh.shape
    hid = w1.shape[-1]
    dh = w2.shape[-1]
    return pl.pallas_call(
        _compress_kernel,
        grid=(2, bg),
        in_specs=[
            pl.BlockSpec((None, None, n, hw), lambda s, i: (s, i, 0, 0)),
            pl.BlockSpec((None, 1, 2 * hw), lambda s, i: (s, 0, 0)),
            pl.BlockSpec((None, 2 * hw, hid), lambda s, i: (s, 0, 0)),
            pl.BlockSpec((None, 1, hid), lambda s, i: (s, 0, 0)),
            pl.BlockSpec((None, hid, dh), lambda s, i: (s, 0, 0)),
            pl.BlockSpec((None, 1, dh), lambda s, i: (s, 0, 0)),
        ],
        out_specs=pl.BlockSpec((None, None, n, dh), lambda s, i: (s, i, 0, 0)),
        out_shape=jax.ShapeDtypeStruct((2, bg, n, dh), F32),
        scratch_shapes=[pltpu.VMEM((n + 8, hid), F32)],
        compiler_params=_params("parallel", "parallel"),
        name="compress_tokens",
    )(xh, pos.reshape(2, 1, 2 * hw), w1.astype(BF16), b1.reshape(2, 1, hid), w2.astype(BF16),
      b2.reshape(2, 1, dh))


def _split_bf16(x):
    hi = x.astype(BF16)
    lo = (x - hi.astype(F32)).astype(BF16)
    return hi, lo


def _hgrn_kernel(x_ref, gamma_ref, gn_ref, tril_ref, sum_ref, o_ref,
                 st_ref, gc_ref, k_ref, hk_ref, qg_ref, u_ref, dcat_ref, ps_ref, opart_ref, *, layer):
    c, sub = HG_CHUNK, HG_SUB
    nsub = c // sub
    tt, dk = o_ref.shape
    n_chunks = tt // c
    q_ref, f_ref, i_ref, g_ref = (x_ref.at[:, i * dk:(i + 1) * dk] for i in range(4))

    @pl.when(pl.program_id(2) == 0)
    def _():
        st_ref[...] = jnp.zeros_like(st_ref)

    gamma = gamma_ref[...]
    e = jnp.exp(gamma - jnp.max(gamma, axis=0, keepdims=True))
    sm = e / jnp.sum(e, axis=0, keepdims=True)
    lb = jnp.sum(sm[0:layer + 1, :], axis=0, keepdims=True) - sm[0:1, :]

    def gates(rows):
        f = lb + (1.0 - lb) * _sigmoid(f_ref[rows, :].astype(F32))
        k = 1.0 - f
        hi, lo = _split_bf16(jnp.log(jnp.maximum(f, TINY)))
        gc = _dot(tril_ref[...], hi) + _dot(tril_ref[...], lo)
        gc_ref[rows, :] = gc
        k_ref[rows, :] = k
        hk_ref[rows, :] = gc - jnp.log(k)
        qg_ref[rows, :] = (q_ref[rows, :].astype(F32) * jnp.exp(gc)).astype(BF16)

    r8 = lax.broadcasted_iota(jnp.int32, (1, 8, 1), 1)
    srow = lax.broadcasted_iota(jnp.int32, (c, 1), 0)
    tcol = lax.broadcasted_iota(jnp.int32, (c, c), 1)

    def half_major(x):
        return [jnp.concatenate([x[(2 * i + h) * 8:(2 * i + h + 1) * 8, :] for i in range(nsub)], axis=0)
                .reshape(nsub, 8, -1) for h in range(2)]

    def state_free_part(ci):
        rows = slice(ci * c, (ci + 1) * c)
        q = q_ref[rows, :].astype(F32)
        gc = gc_ref[rows, :]
        k = k_ref[rows, :]
        ivb = i_ref[rows, :]
        glast = gc[c - 1:c, :]
        u_ref[ci] = _dot_tn((k * jnp.exp(glast - gc)).astype(BF16), ivb)

        qhat = [jnp.zeros((sub, dk), F32)]
        khat = []
        for si in range(1, nsub):
            lo_r, hi_r = si * sub, (si + 1) * sub
            ref = gc[lo_r - 1:lo_r, :]
            qhat.append(q[lo_r:hi_r, :] * jnp.exp(gc[lo_r:hi_r, :] - ref))
            khat.append(k * jnp.exp(jnp.where(srow < lo_r, ref - gc, NEG)))
        at = _dot_nt(jnp.concatenate(khat, axis=0).astype(BF16), jnp.concatenate(qhat, axis=0).astype(BF16))
        at_off = jnp.zeros((c, c), F32)
        for si in range(1, nsub):
            at_off = jnp.where(tcol // sub == si, at[(si - 1) * c:si * c, :], at_off)
        opart_ref[rows, :] = _dot_tn(at_off.astype(BF16), ivb)

        lo_rows = slice(ci * c, ci * c + c // 2)
        hi_rows = slice(ci * c + c // 2, (ci + 1) * c)
        q0, q1 = half_major(q)
        g0, g1 = half_major(gc)
        h0, h1 = half_major(hk_ref[rows, :])
        for s in range(sub // 2):
            key = h0[:, s:s + 1, :]
            d0 = q0 * jnp.exp(jnp.where(r8 >= s, g0 - key, NEG))
            d1 = q1 * jnp.exp(g1 - key)
            dcat_ref[lo_rows, s * dk:(s + 1) * dk] = d0.reshape(c // 2, dk).astype(BF16)
            dcat_ref[hi_rows, s * dk:(s + 1) * dk] = d1.reshape(c // 2, dk).astype(BF16)
        dcat_ref[lo_rows, (sub // 2) * dk:] = jnp.zeros((c // 2, (sub // 2) * dk), BF16)
        for s in range(sub // 2):
            key = h1[:, s:s + 1, :]
            d1 = q1 * jnp.exp(jnp.where(r8 >= s, g1 - key, NEG))
            col = (sub // 2 + s) * dk
            dcat_ref[hi_rows, col:col + dk] = d1.reshape(c // 2, dk).astype(BF16)

    group = tril_ref.shape[0]
    for g0 in range(0, tt, group):
        grows = slice(g0, g0 + group)
        gates(grows)
        for ci in range(g0 // c, (g0 + group) // c):
            state_free_part(ci)
        ps_ref[grows, :] = _dot(dcat_ref[grows, :], sum_ref[...])

    glast = [gc_ref[(ci + 1) * c - 1:(ci + 1) * c, :] for ci in range(n_chunks)]
    glast += [jnp.zeros((1, dk), F32)] * (-n_chunks % 8)
    decay_cols = jnp.exp(jnp.concatenate(glast, axis=0)).T
    st = st_ref[...]
    for ci in range(n_chunks):
        rows = slice(ci * c, (ci + 1) * c)
        opart_ref[rows, :] += _dot(qg_ref[rows, :], st.astype(BF16))
        st = decay_cols[:, ci:ci + 1] * st + u_ref[ci]
    st_ref[...] = st

    own_block = (lax.broadcasted_iota(jnp.int32, (c, dk), 1) // sub
                 == (lax.broadcasted_iota(jnp.int32, (c, dk), 0) // 8) % nsub)
    for ci in range(n_chunks):
        rows = slice(ci * c, (ci + 1) * c)
        ivb = i_ref[rows, :]
        a = jnp.where(own_block, ps_ref[rows, :], 0.0)
        od = _dot(a.astype(BF16), jnp.concatenate([ivb, jnp.zeros((dk - c, ivb.shape[1]), BF16)], axis=0))
        o = opart_ref[rows, :] + jnp.concatenate([od[(h * nsub + i) * 8:(h * nsub + i + 1) * 8, :]
                                                  for i in range(nsub) for h in range(2)], axis=0)
        o = o * lax.rsqrt(jnp.mean(o * o, axis=-1, keepdims=True) + EPS) * gn_ref[...]
        o_ref[rows, :] = (o * _silu(g_ref[rows, :].astype(F32))).astype(o_ref.dtype)


def hgrn2(u, lb_gamma, gnorm_w, *, layer, batch, seq, tt):
    h, dk, dv = HG_HEADS, HG_DK, HG_DV
    nt = seq // tt
    depth = lb_gamma.shape[0]
    r = jnp.arange(HG_GROUP * HG_CHUNK)
    tril = ((r[:, None] >= r[None, :]) & (r[:, None] // HG_CHUNK == r[None, :] // HG_CHUNK)).astype(BF16)
    key_offset = jnp.arange(HG_SUB * dk) // dk
    summer = (key_offset[:, None] == jnp.arange(dk)[None, :] % HG_SUB).astype(BF16)
    return pl.pallas_call(
        functools.partial(_hgrn_kernel, layer=layer),
        grid=(batch, h, nt),
        in_specs=[
            pl.BlockSpec((tt, 4 * dk), lambda b, hh, t: (b * nt + t, hh)),
            pl.BlockSpec((depth, dk), lambda b, hh, t: (0, hh)),
            pl.BlockSpec((1, dv), lambda b, hh, t: (0, hh)),
            pl.BlockSpec((HG_GROUP * HG_CHUNK, HG_GROUP * HG_CHUNK), lambda b, hh, t: (0, 0)),
            pl.BlockSpec((HG_SUB * dk, dk), lambda b, hh, t: (0, 0)),
        ],
        out_specs=pl.BlockSpec((tt, dv), lambda b, hh, t: (b * nt + t, hh)),
        out_shape=jax.ShapeDtypeStruct((batch * seq, h * dv), BF16),
        scratch_shapes=[
            pltpu.VMEM((dk, dv), F32),
            pltpu.VMEM((tt, dk), F32), pltpu.VMEM((tt, dk), F32), pltpu.VMEM((tt, dk), F32),
            pltpu.VMEM((tt, dk), BF16), pltpu.VMEM((tt // HG_CHUNK, dk, dv), F32),
            pltpu.VMEM((tt, HG_SUB * dk), BF16), pltpu.VMEM((tt, dk), F32), pltpu.VMEM((tt, dv), F32),
        ],
        compiler_params=_params("parallel", "parallel", "arbitrary"),
        name="hgrn2",
    )(u, lb_gamma, gnorm_w.reshape(1, h * dv), tril, summer)


QL = NSA_HPG * Q_BLOCK
FLAG_BITS = 16


def _tile_positions(qb):
    lane = lax.broadcasted_iota(jnp.int32, (1, QL), 1)
    return qb * Q_BLOCK + (lane & (Q_BLOCK - 1))


CMP_CHUNK = 128
SEL_CLASSES = 4
KEY_LANES = 128
SLOPE_PARTS = 3
LOG2E = 1.4426950408889634


def _load_query(qa_ref, q_ref, sa_ref):
    dh = NSA_HEAD_DIM
    qt = (q_ref[...].astype(F32) * (dh ** -0.5 * LOG2E)).T
    qa_ref[0:dh, :] = jnp.concatenate([qt[h * dh:(h + 1) * dh, :] for h in range(NSA_HPG)], axis=1).astype(BF16)
    qa_ref[dh:, :] = sa_ref[...]


def _nsa_select_body(nch, q_ref, sa_ref, kc_ref, vct_ref, slope_ref, oc_ref, sel_ref, flag_ref,
                     qa_ref, s_ref, pb_ref, p_ref):
    ck = CMP_CHUNK
    nk = nch * ck
    ns = nk * CMP_STRIDE // SLC_BLOCK
    qb = pl.program_id(2)
    _load_query(qa_ref, q_ref, sa_ref)
    t = _tile_positions(qb)
    slope2 = slope_ref[...] * LOG2E

    rows = []
    mc = None
    for ch in range(nch):
        rs = slice(ch * ck, (ch + 1) * ck)
        s = _dot(kc_ref[rs, :], qa_ref[...])
        end = (ch * ck + lax.broadcasted_iota(jnp.int32, (ck, 1), 0)) * CMP_STRIDE + (CMP_BLOCK - 1)
        s = jnp.where(t >= end, s, NEG)
        s_ref[rs, :] = s
        brow = slope2 * (ch * ck * CMP_STRIDE + (CMP_BLOCK - 1) - qb * Q_BLOCK).astype(F32)
        rows.append(brow)
        cm = jnp.max(s.reshape(ck // 8, 8, QL), axis=0) + brow
        mc = cm if mc is None else jnp.maximum(mc, cm)
    m = jnp.max(mc, axis=0, keepdims=True)

    lsum = None
    for ch in range(nch):
        rs = slice(ch * ck, (ch + 1) * ck)
        p = jnp.exp2(s_ref[rs, :] - (m - rows[ch]))
        ls = jnp.sum(p.reshape(ck // 8, 8, QL), axis=0)
        lsum = ls if lsum is None else lsum + ls
        s_ref[rs, :] = p
        pb_ref[rs, :] = p.astype(BF16)
    l = jnp.sum(lsum, axis=0, keepdims=True)
    inv = jnp.where(t >= CMP_BLOCK - 1, 1.0 / jnp.maximum(l, TINY), 0.0)
    oc_ref[...] = _dot(vct_ref[:, 0:nk], pb_ref[0:nk, :]) * inv

    p_ref[0:8, :] = jnp.zeros((8, Q_BLOCK), F32)
    for ch in range(nch):
        pn = s_ref[ch * ck:(ch + 1) * ck, :] * inv
        psum = pn[:, 0:Q_BLOCK]
        for h in range(1, NSA_HPG):
            psum = psum + pn[:, h * Q_BLOCK:(h + 1) * Q_BLOCK]
        p_ref[8 + ch * ck:8 + (ch + 1) * ck, :] = psum
    p_ref[8 + nk:16 + nk, :] = jnp.zeros((8, Q_BLOCK), F32)

    ratio = SLC_BLOCK // CMP_STRIDE
    imp = p_ref[pl.ds(7, ns, stride=ratio), :]
    for r in range(1, ratio + 1):
        imp = imp + p_ref[pl.ds(7 + r, ns, stride=ratio), :]

    j = lax.broadcasted_iota(jnp.int32, (ns, 1), 0)
    cur = t[:, 0:Q_BLOCK] // SLC_BLOCK
    forced = (j == 0) | (j == cur) | (j == cur - 1)
    imp = jnp.where(forced, FORCE_SCORE, jnp.where(j > cur, -1.0, imp))
    for _ in range(min(SLC_TOPK, ns)):
        mx = jnp.max(imp, axis=0, keepdims=True)
        idx = jnp.min(jnp.where(imp == mx, j, ns), axis=0, keepdims=True)
        imp = jnp.where(j == idx, NEG, imp)
    sel = jnp.where((imp == NEG) & (j <= cur), 1.0, 0.0)
    sel_ref[0:ns, :] = sel
    if ns < sel_ref.shape[0]:
        sel_ref[ns:, :] = jnp.zeros((sel_ref.shape[0] - ns, Q_BLOCK), F32)

    cnt = _dot_nt(jnp.ones((8, Q_BLOCK), BF16), sel.astype(BF16))
    flags = jnp.where(cnt > 0.0, 1.0, 0.0).astype(BF16)
    jj = lax.broadcasted_iota(jnp.int32, (ns, 128), 0)
    ww = lax.broadcasted_iota(jnp.int32, (ns, 128), 1)
    bit = jnp.left_shift(1, jj & (FLAG_BITS - 1)).astype(F32)
    pack = jnp.where(ww == jj // FLAG_BITS, bit, 0.0).astype(BF16)
    flag_ref[...] = _dot(flags, pack)


def _nsa_select_kernel(q_ref, sa_ref, kc_ref, *rest):
    n_chunks = kc_ref.shape[0] // CMP_CHUNK
    n_cls = min(SEL_CLASSES, n_chunks)
    cls = pl.program_id(2) * n_cls // pl.num_programs(2)
    for c in range(n_cls):
        @pl.when(cls == c)
        def _(c=c):
            _nsa_select_body((c + 1) * n_chunks // n_cls, q_ref, sa_ref, kc_ref, *rest)


def nsa_select(u, q_col, slope_rows, kc, vct, slopes):
    b, g, nc = kc.shape[:3]
    dh, ql = NSA_HEAD_DIM, QL
    nqb = u.shape[0] // (b * Q_BLOCK)
    ns = nc * CMP_STRIDE // SLC_BLOCK
    tile = lambda *shape: pl.BlockSpec((None, None, None) + shape, lambda bi, gi, qi: (bi, gi, qi, 0, 0))
    return pl.pallas_call(
        _nsa_select_kernel,
        grid=(b, g, nqb),
        in_specs=[
            pl.BlockSpec((Q_BLOCK, NSA_HPG * dh), lambda bi, gi, qi: (bi * nqb + qi, q_col + gi)),
            pl.BlockSpec((None, KEY_LANES - dh, ql), lambda bi, gi, qi: (gi, 0, 0)),
            pl.BlockSpec((None, None, nc, KEY_LANES), lambda bi, gi, qi: (bi, gi, 0, 0)),
            pl.BlockSpec((None, None, dh, nc), lambda bi, gi, qi: (bi, gi, 0, 0)),
            pl.BlockSpec((None, 1, ql), lambda bi, gi, qi: (gi, 0, 0)),
        ],
        out_specs=[tile(dh, ql), tile(ns, Q_BLOCK), tile(8, 128)],
        out_shape=[
            jax.ShapeDtypeStruct((b, g, nqb, dh, ql), F32),
            jax.ShapeDtypeStruct((b, g, nqb, ns, Q_BLOCK), F32),
            jax.ShapeDtypeStruct((b, g, nqb, 8, 128), F32),
        ],
        scratch_shapes=[
            pltpu.VMEM((KEY_LANES, ql), BF16), pltpu.VMEM((nc, ql), F32), pltpu.VMEM((nc, ql), BF16),
            pltpu.VMEM((nc + 16, Q_BLOCK), F32),
        ],
        compiler_params=_params("parallel", "parallel", "parallel"),
        name="nsa_select",
    )(u, slope_rows, kc, vct, slopes)


ATT_GROUP = 8
LIST_PAD = 3 * ATT_GROUP
DIAG_BLOCKS = Q_BLOCK // SLC_BLOCK
WIN_BLOCKS = WINDOW // SLC_BLOCK


def _score_blocks(k_src, blocks, qa_ref, kcat_ref, s_ref):
    kb = SLC_BLOCK
    for b, j in enumerate(blocks):
        kcat_ref[b * kb:(b + 1) * kb, :] = k_src[jnp.maximum(j, 0)]
    n = len(blocks) * kb
    s_ref[0:n, :] = _dot(kcat_ref[0:n, :], qa_ref[...])


def _softmax_step(v_src, blocks, masks, sel_rows, t, slope2, blk0, s_ref, vcat_ref, p_ref, m_ref, l_ref, acc_ref):
    nb = len(blocks)
    kb = SLC_BLOCK
    for b, j in enumerate(blocks):
        vcat_ref[b * kb:(b + 1) * kb, :] = v_src[jnp.maximum(j, 0)]
    rows = []
    mc = None
    for b, j in enumerate(blocks):
        brow = slope2 * ((j - blk0) * kb).astype(F32)
        if sel_rows[b] is not None:
            brow = jnp.where(sel_rows[b] > 0.0, brow, NEG)
        brow = jnp.where(j >= 0, brow, NEG)
        rows.append(brow)
        s = s_ref[b * kb:(b + 1) * kb, :]
        if masks[b] is not None:
            kpos = j * kb + lax.broadcasted_iota(jnp.int32, (kb, 1), 0)
            valid = (t >= kpos) if masks[b] == "causal" else (t - kpos < WINDOW)
            s = jnp.where(valid, s, NEG)
            s_ref[b * kb:(b + 1) * kb, :] = s
        cm = jnp.max(s.reshape(kb // 8, 8, QL), axis=0) + brow
        mc = cm if mc is None else jnp.maximum(mc, cm)
    m_old = m_ref[...]
    m_new = jnp.maximum(m_old, jnp.max(mc, axis=0, keepdims=True))
    alpha = jnp.exp2(m_old - m_new)
    m_ref[...] = m_new

    lsum = None
    for b in range(nb):
        p = jnp.exp2(s_ref[b * kb:(b + 1) * kb, :] - (m_new - rows[b]))
        ls = jnp.sum(p.reshape(kb // 8, 8, QL), axis=0)
        lsum = ls if lsum is None else lsum + ls
        p_ref[b * kb:(b + 1) * kb, :] = p.astype(BF16)
    l_ref[...] = alpha * l_ref[...] + lsum
    acc_ref[...] = alpha * acc_ref[...] + _dot_tn(vcat_ref[0:nb * kb, :], p_ref[0:nb * kb, :])


def _nsa_attend_kernel(fw_ref, q_ref, sa_ref, ks_ref, vs_ref, kw_ref, vw_ref, sel_ref, oc_ref, gate_ref, slope_ref,
                       o_ref, qa_ref, kcat_ref, s_ref, vcat_ref, p_ref, m_ref, l_ref, acc_ref,
                       kcat_w_ref, s_w_ref, vcat_w_ref, p_w_ref, m_w_ref, l_w_ref, acc_w_ref, list_ref, *,
                       words_per_tile):
    bi, gi, qb = pl.program_id(0), pl.program_id(1), pl.program_id(2)
    tile_id = (bi * pl.num_programs(1) + gi) * pl.num_programs(2) + qb
    _load_query(qa_ref, q_ref, sa_ref)
    t = _tile_positions(qb)
    slope2 = slope_ref[...] * LOG2E
    blk0 = qb * DIAG_BLOCKS
    diag = [blk0 + i for i in range(DIAG_BLOCKS)]
    past = [blk0 - WIN_BLOCKS + i for i in range(WIN_BLOCKS)]

    def sel_row(j):
        row = sel_ref[pl.ds(jnp.maximum(j, 0), 1), :]
        return jnp.concatenate([row] * NSA_HPG, axis=1)

    def reset(m, l, acc):
        m[...] = jnp.full_like(m, NEG)
        l[...] = jnp.zeros_like(l)
        acc[...] = jnp.zeros_like(acc)

    def result(l, acc):
        return acc[...] * (1.0 / jnp.maximum(jnp.sum(l[...], axis=0, keepdims=True), TINY))

    def scan_flags(w, n):
        base = w * FLAG_BITS
        word = fw_ref[tile_id * words_per_tile + w] & ((1 << jnp.minimum(blk0 - base, FLAG_BITS)) - 1)
        for i in range(FLAG_BITS):
            list_ref[n] = base + i
            n = n + ((word >> i) & 1)
        return n

    n_sel = lax.fori_loop(0, (blk0 + FLAG_BITS - 1) // FLAG_BITS, scan_flags, 0)
    for i in range(LIST_PAD):
        list_ref[n_sel + i] = -1

    def listed(i):
        return [list_ref[i * ATT_GROUP + b] for b in range(ATT_GROUP)]

    _score_blocks(kw_ref, past + diag, qa_ref, kcat_w_ref, s_w_ref)
    _score_blocks(ks_ref, diag, qa_ref, kcat_ref.at[0], s_ref.at[0])
    _score_blocks(ks_ref, listed(0), qa_ref, kcat_ref.at[1], s_ref.at[1])

    slc_state = (m_ref, l_ref, acc_ref)
    reset(*slc_state)
    _softmax_step(vs_ref, diag, ["causal"] * DIAG_BLOCKS, [sel_row(j) for j in diag], t, slope2, blk0,
                  s_ref.at[0], vcat_ref, p_ref, *slc_state)

    win_state = (m_w_ref, l_w_ref, acc_w_ref)
    reset(*win_state)
    masks = ["window"] * DIAG_BLOCKS + [None] * (WIN_BLOCKS - DIAG_BLOCKS) + ["causal"] * DIAG_BLOCKS
    _softmax_step(vw_ref, past + diag, masks, [None] * len(masks), t, slope2, blk0,
                  s_w_ref, vcat_w_ref, p_w_ref, *win_state)

    n_steps = (n_sel + ATT_GROUP - 1) // ATT_GROUP

    def slc_step(i, slot):
        _score_blocks(ks_ref, listed(i + 1), qa_ref, kcat_ref.at[slot], s_ref.at[slot])
        blocks = listed(i)
        _softmax_step(vs_ref, blocks, [None] * ATT_GROUP, [sel_row(j) for j in blocks], t, slope2, blk0,
                      s_ref.at[1 - slot], vcat_ref, p_ref, *slc_state)

    def slc_body(i2, carry):
        slc_step(2 * i2, 0)

        @pl.when(2 * i2 + 1 < n_steps)
        def _():
            slc_step(2 * i2 + 1, 1)

        return carry

    lax.fori_loop(0, (n_steps + 1) // 2, slc_body, 0)
    o_slc = result(l_ref, acc_ref)
    o_win = result(l_w_ref, acc_w_ref)

    gate = _sigmoid(gate_ref[...])
    o = gate[0:1, :] * oc_ref[...] + gate[1:2, :] * o_slc + gate[2:3, :] * o_win
    o = jnp.concatenate([o[:, h * Q_BLOCK:(h + 1) * Q_BLOCK] for h in range(NSA_HPG)], axis=0)
    o_ref[...] = o.T.astype(o_ref.dtype)


def nsa_attend(flag_words, u, q_col, slope_rows, ks, vs, kw, vw, sel, oc, gates, slopes):
    b, g, ns = ks.shape[:3]
    dh, ql = NSA_HEAD_DIM, QL
    nqb = u.shape[0] // (b * Q_BLOCK)
    gkeys = ATT_GROUP * SLC_BLOCK
    wkeys = (WIN_BLOCKS + DIAG_BLOCKS) * SLC_BLOCK
    tile = lambda *shape: pl.BlockSpec((None, None, None) + shape, lambda bi, gi, qi, fw: (bi, gi, qi, 0, 0))
    seq = lambda *shape: pl.BlockSpec((None, None) + shape, lambda bi, gi, qi, fw: (bi, gi, 0, 0, 0))
    state = [pltpu.VMEM((1, ql), F32), pltpu.VMEM((8, ql), F32), pltpu.VMEM((dh, ql), F32)]
    grid_spec = pltpu.PrefetchScalarGridSpec(
        num_scalar_prefetch=1,
        grid=(b, g, nqb),
        in_specs=[
            pl.BlockSpec((Q_BLOCK, NSA_HPG * dh), lambda bi, gi, qi, fw: (bi * nqb + qi, q_col + gi)),
            pl.BlockSpec((None, KEY_LANES - dh, ql), lambda bi, gi, qi, fw: (gi, 0, 0)),
            seq(ns, SLC_BLOCK, KEY_LANES), seq(ns, SLC_BLOCK, dh),
            seq(ns, SLC_BLOCK, KEY_LANES), seq(ns, SLC_BLOCK, dh),
            tile(ns, Q_BLOCK), tile(dh, ql), tile(3, ql),
            pl.BlockSpec((None, 1, ql), lambda bi, gi, qi, fw: (gi, 0, 0)),
        ],
        out_specs=pl.BlockSpec((Q_BLOCK, NSA_HPG * dh), lambda bi, gi, qi, fw: (bi * nqb + qi, gi)),
        scratch_shapes=[
            pltpu.VMEM((KEY_LANES, ql), BF16),
            pltpu.VMEM((2, gkeys, KEY_LANES), BF16), pltpu.VMEM((2, gkeys, ql), F32),
            pltpu.VMEM((gkeys, dh), BF16), pltpu.VMEM((gkeys, ql), BF16), *state,
            pltpu.VMEM((wkeys, KEY_LANES), BF16), pltpu.VMEM((wkeys, ql), F32),
            pltpu.VMEM((wkeys, dh), BF16), pltpu.VMEM((wkeys, ql), BF16), *state,
            pltpu.SMEM((ns + LIST_PAD,), jnp.int32),
        ],
    )
    return pl.pallas_call(
        functools.partial(_nsa_attend_kernel, words_per_tile=ns // FLAG_BITS),
        grid_spec=grid_spec,
        out_shape=jax.ShapeDtypeStruct((b * nqb * Q_BLOCK, g * NSA_HPG * dh), BF16),
        compiler_params=_params("parallel", "parallel", "arbitrary"),
        name="nsa_attend",
    )(flag_words, u, slope_rows, ks, vs, kw, vw, sel, oc, gates, slopes)


def nsa_mixer(u, q_col, kc, vc, ks, vs, kw, vw, gate_logits, cmp_pos, cmp_w1, cmp_b1, cmp_w2, cmp_b2, *,
              batch, seq):
    g, hpg, dh = NSA_KV_GROUPS, NSA_HPG, NSA_HEAD_DIM
    nqb, ns, nc = seq // Q_BLOCK, seq // SLC_BLOCK, seq // CMP_STRIDE

    gates = gate_logits.astype(F32).reshape(batch, nqb, Q_BLOCK, 3, g, hpg).transpose(0, 4, 1, 3, 5, 2)
    gates = gates.reshape(batch, g, nqb, 3, QL)
    slopes = 2.0 ** (-8.0 * jnp.arange(1, NSA_HEADS + 1, dtype=F32) / NSA_HEADS)
    slopes = jnp.repeat(slopes.reshape(g, 1, hpg), Q_BLOCK, axis=2)

    def halves(x):
        x = x.reshape(batch, nc, CMP_STRIDE, g, dh).transpose(0, 3, 1, 2, 4)
        return x.reshape(batch * g, nc, CMP_STRIDE * dh)

    def value_blocks(x):
        return x.reshape(batch, ns, SLC_BLOCK, g, dh).transpose(0, 3, 1, 2, 4)

    def key_blocks(x):
        r = jnp.arange(SLC_BLOCK, dtype=F32).astype(BF16).reshape(1, 1, 1, SLC_BLOCK, 1)
        r = jnp.broadcast_to(r, (batch, g, ns, SLC_BLOCK, SLOPE_PARTS))
        pad = jnp.zeros((batch, g, ns, SLC_BLOCK, KEY_LANES - dh - SLOPE_PARTS), BF16)
        return jnp.concatenate([value_blocks(x), r, pad], axis=-1)

    s2 = slopes * LOG2E
    parts = []
    for _ in range(SLOPE_PARTS):
        part = s2.astype(BF16)
        parts.append(part)
        s2 = s2 - part.astype(F32)
    slope_rows = jnp.concatenate(parts + [jnp.zeros((g, KEY_LANES - dh - SLOPE_PARTS, QL), BF16)], axis=1)

    cmp = compress_tokens(jnp.stack([halves(kc), halves(vc)]), cmp_pos, cmp_w1, cmp_b1, cmp_w2, cmp_b2)
    cmp = cmp.astype(BF16).reshape(2, batch, g, nc, dh)
    r = (jnp.arange(nc) % CMP_CHUNK * CMP_STRIDE).astype(BF16).reshape(1, 1, nc, 1)
    k_cmp = jnp.concatenate([cmp[0], jnp.broadcast_to(r, (batch, g, nc, SLOPE_PARTS)),
                             jnp.zeros((batch, g, nc, KEY_LANES - dh - SLOPE_PARTS), BF16)], axis=-1)
    v_cmp_t = cmp[1].transpose(0, 1, 3, 2)

    oc, sel, flags = nsa_select(u, q_col, slope_rows, k_cmp, v_cmp_t, slopes)
    flag_words = flags[:, :, :, 0, :ns // FLAG_BITS].astype(jnp.int32).reshape(-1)
    return nsa_attend(flag_words, u, q_col, slope_rows, key_blocks(ks), value_blocks(vs), key_blocks(kw),
                      value_blocks(vw), sel, oc, gates, slopes)


ROW_TILE = 1024
COL_TILE = 1024
FFN_ROW_TILE = 1024
FFN_HID_TILE = 512
CONV_SEQ_TILE = 512
HGRN_SEQ_TILE = 1024


def _even_layer(x, nw, w_in, conv_w, conv_b, ln_w, ln_b, cmp_pos, cmp_w1, cmp_b1, cmp_w2, cmp_b2, w_out, *,
                batch, seq):
    c = conv_w.shape[-1]
    nq = NSA_HEADS * NSA_HEAD_DIM
    nkv = NSA_KV_GROUPS * NSA_HEAD_DIM
    n_in = w_in.shape[1]
    n_pad = -(-n_in // COL_TILE) * COL_TILE
    w_in = jnp.pad(w_in, ((0, 0), (0, n_pad - n_in))).astype(BF16)
    u = norm_matmul(x, nw, w_in, tm=ROW_TILE, tn=COL_TILE)
    a_out = conformer_conv(u, conv_w.reshape(CONV_WIDTH, c), conv_b, ln_w, ln_b, seq=seq, ts=CONV_SEQ_TILE)
    off = 2 * c
    q_col = off // (NSA_HPG * NSA_HEAD_DIM)
    off += nq
    kvs = [u[:, off + i * nkv:off + (i + 1) * nkv] for i in range(6)]
    off += 6 * nkv
    gate_logits = u[:, off:off + 3 * NSA_HEADS]
    b_out = nsa_mixer(u, q_col, *kvs, gate_logits, cmp_pos, cmp_w1, cmp_b1, cmp_w2, cmp_b2,
                      batch=batch, seq=seq)
    return matmul_residual(a_out, b_out, 0, 0, w_out.astype(BF16), x, tm=ROW_TILE, tn=COL_TILE)


def _odd_layer(x, nw, w_in, lb_gamma, gnorm_w, w_out, *, layer, batch, seq):
    d = w_in.shape[0]
    w_in = w_in.reshape(d, 4, HG_HEADS, HG_DK).transpose(0, 2, 1, 3).reshape(d, 4 * HG_HEADS * HG_DK)
    u = norm_matmul(x, nw, w_in.astype(BF16), tm=ROW_TILE, tn=COL_TILE)
    o = hgrn2(u, lb_gamma, gnorm_w, layer=layer, batch=batch, seq=seq, tt=HGRN_SEQ_TILE)
    return matmul_residual(o, o, 0, 1, w_out.astype(BF16), x, tm=ROW_TILE, tn=COL_TILE)


def kernel(x, norm_w, final_norm_w, ev_w_in, ev_conv_w, ev_conv_b, ev_conv_ln_w, ev_conv_ln_b, ev_cmp_pos,
           ev_cmp_w1, ev_cmp_b1, ev_cmp_w2, ev_cmp_b2, ev_w_out, od_w_in, od_lb_gamma, od_gnorm_w, od_w_out,
           ffn_w_gu, ffn_w_down):
    batch, seq, d = x.shape
    depth = norm_w.shape[0]
    xs = x.reshape(batch * seq, d)
    for layer in range(depth):
        i = layer // 2
        if layer % 2 == 0:
            xs = _even_layer(xs, norm_w[layer, 0], ev_w_in[i], ev_conv_w[i], ev_conv_b[i], ev_conv_ln_w[i],
                             ev_conv_ln_b[i], ev_cmp_pos[i], ev_cmp_w1[i], ev_cmp_b1[i], ev_cmp_w2[i],
                             ev_cmp_b2[i], ev_w_out[i], batch=batch, seq=seq)
        else:
            xs = _odd_layer(xs, norm_w[layer, 0], od_w_in[i], od_lb_gamma.astype(F32), od_gnorm_w[i],
                            od_w_out[i], layer=layer, batch=batch, seq=seq)
        xs = ffn_block(xs, norm_w[layer, 1], ffn_w_gu[layer].astype(BF16), ffn_w_down[layer].astype(BF16),
                       final_norm_w, tm=FFN_ROW_TILE, th=FFN_HID_TILE, final_norm=layer == depth - 1)
    return xs.reshape(batch, seq, d)
```

```python
import functools

import jax
import jax.numpy as jnp
import numpy as np
from jax import lax
from jax.experimental import pallas as pl
from jax.experimental.pallas import tpu as pltpu

F32 = jnp.float32
BF16 = jnp.bfloat16

EPS = 1e-6
TINY = 1e-30
NEG = -1e30

VMEM_LIMIT_BYTES = 56 * 1024 * 1024

CONV_WIDTH = 31
NSA_HEADS = 16
NSA_HEAD_DIM = 64
NSA_KV_GROUPS = 4
NSA_HPG = NSA_HEADS // NSA_KV_GROUPS
CMP_STRIDE = 16
CMP_BLOCK = 32
SLC_BLOCK = 64
SLC_TOPK = 16
WINDOW = 512
Q_BLOCK = 128
HG_HEADS = 16
HG_DK = 128
HG_DV = 128
HG_CHUNK = 64
HG_SUB = 16
HG_GROUP = 4


def _params(*sem):
    return pltpu.CompilerParams(dimension_semantics=sem, vmem_limit_bytes=VMEM_LIMIT_BYTES)


def _sigmoid(x):
    return 1.0 / (1.0 + jnp.exp(-x))


def _silu(x):
    return x * _sigmoid(x)


def _dot(a, b):
    return jnp.dot(a, b, preferred_element_type=F32)


def _dot_nt(a, b):
    return lax.dot_general(a, b, (((1,), (1,)), ((), ())), preferred_element_type=F32)


def _dot_tn(a, b):
    return lax.dot_general(a, b, (((0,), (0,)), ((), ())), preferred_element_type=F32)


def _norm_matmul_kernel(x_ref, nw_ref, w_ref, o_ref, h_ref):
    @pl.when(pl.program_id(1) == 0)
    def _():
        x = x_ref[...]
        ms = jnp.mean(x * x, axis=-1, keepdims=True)
        h_ref[...] = (x * lax.rsqrt(ms + EPS) * nw_ref[...]).astype(BF16)

    o_ref[...] = _dot(h_ref[...], w_ref[...]).astype(o_ref.dtype)


def norm_matmul(x, nw, w, *, tm, tn, out_dtype=BF16):
    m, k = x.shape
    n = w.shape[1]
    return pl.pallas_call(
        _norm_matmul_kernel,
        grid=(m // tm, n // tn),
        in_specs=[
            pl.BlockSpec((tm, k), lambda i, j: (i, 0)),
            pl.BlockSpec((1, k), lambda i, j: (0, 0)),
            pl.BlockSpec((k, tn), lambda i, j: (0, j)),
        ],
        out_specs=pl.BlockSpec((tm, tn), lambda i, j: (i, j)),
        out_shape=jax.ShapeDtypeStruct((m, n), out_dtype),
        scratch_shapes=[pltpu.VMEM((tm, k), BF16)],
        compiler_params=_params("parallel", "arbitrary"),
        name="norm_matmul",
    )(x, nw.reshape(1, k), w)


def _matmul_res_kernel(a1_ref, a2_ref, w1_ref, w2_ref, r_ref, o_ref):
    acc = _dot(a1_ref[...], w1_ref[...]) + _dot(a2_ref[...], w2_ref[...])
    o_ref[...] = r_ref[...] + acc


def matmul_residual(a1, a2, blk1, blk2, w, res, *, tm, tn):
    m = res.shape[0]
    k, n = w.shape
    kh = k // 2
    return pl.pallas_call(
        _matmul_res_kernel,
        grid=(m // tm, n // tn),
        in_specs=[
            pl.BlockSpec((tm, kh), lambda i, j: (i, blk1)),
            pl.BlockSpec((tm, kh), lambda i, j: (i, blk2)),
            pl.BlockSpec((kh, tn), lambda i, j: (0, j)),
            pl.BlockSpec((kh, tn), lambda i, j: (1, j)),
            pl.BlockSpec((tm, tn), lambda i, j: (i, j)),
        ],
        out_specs=pl.BlockSpec((tm, tn), lambda i, j: (i, j)),
        out_shape=jax.ShapeDtypeStruct((m, n), F32),
        compiler_params=_params("parallel", "arbitrary"),
        name="matmul_residual",
    )(a1, a2, w, w, res)


def _ffn_kernel(x_ref, nw_ref, wg_ref, wu_ref, wd_ref, fw_ref, o_ref, h_ref, *, final_norm):
    j = pl.program_id(1)

    @pl.when(j == 0)
    def _():
        x = x_ref[...]
        ms = jnp.mean(x * x, axis=-1, keepdims=True)
        h_ref[...] = (x * lax.rsqrt(ms + EPS) * nw_ref[...]).astype(BF16)
        o_ref[...] = x

    h = h_ref[...]
    a = _dot(h, wg_ref[...])
    b = _dot(h, wu_ref[...])
    z = (_silu(a) * b).astype(BF16)
    o_ref[...] += _dot(z, wd_ref[...])

    if final_norm:
        @pl.when(j == pl.num_programs(1) - 1)
        def _():
            y = o_ref[...]
            ms = jnp.mean(y * y, axis=-1, keepdims=True)
            o_ref[...] = y * lax.rsqrt(ms + EPS) * fw_ref[...]


def ffn_block(x, nw, w_gu, w_down, final_w, *, tm, th, final_norm):
    m, d = x.shape
    hid = w_down.shape[0]
    nh = hid // th
    return pl.pallas_call(
        functools.partial(_ffn_kernel, final_norm=final_norm),
        grid=(m // tm, nh),
        in_specs=[
            pl.BlockSpec((tm, d), lambda i, j: (i, 0)),
            pl.BlockSpec((1, d), lambda i, j: (0, 0)),
            pl.BlockSpec((d, th), lambda i, j: (0, j)),
            pl.BlockSpec((d, th), lambda i, j: (0, j + nh)),
            pl.BlockSpec((th, d), lambda i, j: (j, 0)),
            pl.BlockSpec((1, d), lambda i, j: (0, 0)),
        ],
        out_specs=pl.BlockSpec((tm, d), lambda i, j: (i, 0)),
        out_shape=jax.ShapeDtypeStruct((m, d), F32),
        scratch_shapes=[pltpu.VMEM((tm, d), BF16)],
        compiler_params=_params("parallel", "arbitrary"),
        name="ffn_block",
    )(x, nw.reshape(1, d), w_gu, w_gu, w_down, final_w.reshape(1, d))


CONV_HALO = 32
CONV_ROWS = 32


def _conv_kernel(a_ref, g_ref, ah_ref, gh_ref, cw_ref, cb_ref, lw_ref, lb_ref, o_ref, hs_ref, sh_ref, cv_ref, *,
                 tiles_per_seq):
    ts = a_ref.shape[0]
    first = (pl.program_id(0) % tiles_per_seq) == 0
    hprev = ah_ref[...].astype(F32) * _sigmoid(gh_ref[...].astype(F32))
    hs_ref[0:CONV_HALO, :] = jnp.where(first, 0.0, hprev)
    hs_ref[CONV_HALO:CONV_HALO + ts, :] = a_ref[...].astype(F32) * _sigmoid(g_ref[...].astype(F32))
    off = CONV_HALO - (CONV_WIDTH - 1)
    n_shifted = sh_ref.shape[1]
    for r in range(1, 8):
        sh_ref[r - 1] = hs_ref[r:r + n_shifted, :]

    def body(i, carry):
        base = pl.multiple_of(i * CONV_ROWS, CONV_ROWS)
        acc = jnp.zeros((CONV_ROWS // 8, 8, hs_ref.shape[1]), F32) + cb_ref[...]
        for w in range(CONV_WIDTH):
            r, a = (off + w) % 8, (off + w) // 8 * 8
            src = hs_ref if r == 0 else sh_ref.at[r - 1]
            rows = src[pl.ds(base + a, CONV_ROWS), :].reshape(CONV_ROWS // 8, 8, -1)
            acc = acc + rows * cw_ref[w * 8:(w + 1) * 8, :]
        cv_ref[pl.ds(base, CONV_ROWS), :] = acc.reshape(CONV_ROWS, -1)
        return carry

    lax.fori_loop(0, ts // CONV_ROWS, body, 0)

    y = cv_ref[...]
    mu = jnp.mean(y, axis=-1, keepdims=True)
    d = y - mu
    var = jnp.mean(d * d, axis=-1, keepdims=True)
    hn = d * lax.rsqrt(var + EPS) * lw_ref[...] + lb_ref[...]
    o_ref[...] = _silu(hn).astype(o_ref.dtype)


def conformer_conv(u, conv_w, conv_b, ln_w, ln_b, *, seq, ts):
    t = u.shape[0]
    c = conv_w.shape[1]
    hb = ts // CONV_HALO
    return pl.pallas_call(
        functools.partial(_conv_kernel, tiles_per_seq=seq // ts),
        grid=(t // ts,),
        in_specs=[
            pl.BlockSpec((ts, c), lambda i: (i, 0)),
            pl.BlockSpec((ts, c), lambda i: (i, 1)),
            pl.BlockSpec((CONV_HALO, c), lambda i: (jnp.maximum(i * hb - 1, 0), 0)),
            pl.BlockSpec((CONV_HALO, c), lambda i: (jnp.maximum(i * hb - 1, 0), 1)),
            pl.BlockSpec((CONV_WIDTH * 8, c), lambda i: (0, 0)),
            pl.BlockSpec((1, c), lambda i: (0, 0)),
            pl.BlockSpec((1, c), lambda i: (0, 0)),
            pl.BlockSpec((1, c), lambda i: (0, 0)),
        ],
        out_specs=pl.BlockSpec((ts, c), lambda i: (i, 0)),
        out_shape=jax.ShapeDtypeStruct((t, c), BF16),
        scratch_shapes=[pltpu.VMEM((CONV_HALO + ts, c), F32), pltpu.VMEM((7, CONV_HALO + ts - 8, c), F32),
                        pltpu.VMEM((ts, c), F32)],
        compiler_params=_params("parallel"),
        name="conformer_conv",
    )(u, u, u, u, jnp.repeat(conv_w, 8, axis=0), conv_b.reshape(1, c), ln_w.reshape(1, c), ln_b.reshape(1, c))


def _compress_kernel(x_ref, pos_ref, w1_ref, b1_ref, w2_ref, b2_ref, o_ref, sh_ref):
    n = x_ref.shape[0]
    hw = x_ref.shape[1]
    x = x_ref[...].astype(F32)
    xa = (x + pos_ref[:, 0:hw]).astype(BF16)
    xb = (x + pos_ref[:, hw:2 * hw]).astype(BF16)
    p1 = _dot(xa, w1_ref[0:hw, :])
    sh_ref[0:n, :] = _dot(xb, w1_ref[hw:2 * hw, :])
    sh_ref[n:n + 8, :] = jnp.zeros((8, sh_ref.shape[1]), F32)
    hid = p1 + sh_ref[1:n + 1, :] + b1_ref[...]
    o_ref[...] = _dot(_silu(hid).astype(BF16), w2_ref[...]) + b2_ref[...]


def compress_tokens(xh, pos, w1, b1, w2, b2):
    _, bg, n, hw = xh.shape
    hid = w1.shape[-1]
    dh = w2.shape[-1]
    return pl.pallas_call(
        _compress_kernel,
        grid=(2, bg),
        in_specs=[
            pl.BlockSpec((None, None, n, hw), lambda s, i: (s, i, 0, 0)),
            pl.BlockSpec((None, 1, 2 * hw), lambda s, i: (s, 0, 0)),
            pl.BlockSpec((None, 2 * hw, hid), lambda s, i: (s, 0, 0)),
            pl.BlockSpec((None, 1, hid), lambda s, i: (s, 0, 0)),
            pl.BlockSpec((None, hid, dh), lambda s, i: (s, 0, 0)),
            pl.BlockSpec((None, 1, dh), lambda s, i: (s, 0, 0)),
        ],
        out_specs=pl.BlockSpec((None, None, n, dh), lambda s, i: (s, i, 0, 0)),
        out_shape=jax.ShapeDtypeStruct((2, bg, n, dh), F32),
        scratch_shapes=[pltpu.VMEM((n + 8, hid), F32)],
        compiler_params=_params("parallel", "parallel"),
        name="compress_tokens",
    )(xh, pos.reshape(2, 1, 2 * hw), w1.astype(BF16), b1.reshape(2, 1, hid), w2.astype(BF16),
      b2.reshape(2, 1, dh))


def _split_bf16(x):
    hi = x.astype(BF16)
    lo = (x - hi.astype(F32)).astype(BF16)
    return hi, lo


def _hgrn_kernel(x_ref, gamma_ref, gn_ref, tril_ref, sum_ref, o_ref,
                 st_ref, gc_ref, k_ref, hk_ref, qg_ref, u_ref, dcat_ref, ps_ref, opart_ref, *, layer):
    c, sub = HG_CHUNK, HG_SUB
    nsub = c // sub
    tt, dk = o_ref.shape
    n_chunks = tt // c
    q_ref, f_ref, i_ref, g_ref = (x_ref.at[:, i * dk:(i + 1) * dk] for i in range(4))

    @pl.when(pl.program_id(2) == 0)
    def _():
        st_ref[...] = jnp.zeros_like(st_ref)

    gamma = gamma_ref[...]
    e = jnp.exp(gamma - jnp.max(gamma, axis=0, keepdims=True))
    sm = e / jnp.sum(e, axis=0, keepdims=True)
    lb = jnp.sum(sm[0:layer + 1, :], axis=0, keepdims=True) - sm[0:1, :]

    def gates(rows):
        f = lb + (1.0 - lb) * _sigmoid(f_ref[rows, :].astype(F32))
        k = 1.0 - f
        hi, lo = _split_bf16(jnp.log(jnp.maximum(f, TINY)))
        gc = _dot(tril_ref[...], hi) + _dot(tril_ref[...], lo)
        gc_ref[rows, :] = gc
        k_ref[rows, :] = k
        hk_ref[rows, :] = gc - jnp.log(k)
        qg_ref[rows, :] = (q_ref[rows, :].astype(F32) * jnp.exp(gc)).astype(BF16)

    r8 = lax.broadcasted_iota(jnp.int32, (1, 8, 1), 1)
    srow = lax.broadcasted_iota(jnp.int32, (c, 1), 0)
    tcol = lax.broadcasted_iota(jnp.int32, (c, c), 1)

    def half_major(x):
        return [jnp.concatenate([x[(2 * i + h) * 8:(2 * i + h + 1) * 8, :] for i in range(nsub)], axis=0)
                .reshape(nsub, 8, -1) for h in range(2)]

    def state_free_part(ci):
        rows = slice(ci * c, (ci + 1) * c)
        q = q_ref[rows, :].astype(F32)
        gc = gc_ref[rows, :]
        k = k_ref[rows, :]
        ivb = i_ref[rows, :]
        glast = gc[c - 1:c, :]
        u_ref[ci] = _dot_tn((k * jnp.exp(glast - gc)).astype(BF16), ivb)

        qhat = [jnp.zeros((sub, dk), F32)]
        khat = []
        for si in range(1, nsub):
            lo_r, hi_r = si * sub, (si + 1) * sub
            ref = gc[lo_r - 1:lo_r, :]
            qhat.append(q[lo_r:hi_r, :] * jnp.exp(gc[lo_r:hi_r, :] - ref))
            khat.append(k * jnp.exp(jnp.where(srow < lo_r, ref - gc, NEG)))
        at = _dot_nt(jnp.concatenate(khat, axis=0).astype(BF16), jnp.concatenate(qhat, axis=0).astype(BF16))
        at_off = jnp.zeros((c, c), F32)
        for si in range(1, nsub):
            at_off = jnp.where(tcol // sub == si, at[(si - 1) * c:si * c, :], at_off)
        opart_ref[rows, :] = _dot_tn(at_off.astype(BF16), ivb)

        lo_rows = slice(ci * c, ci * c + c // 2)
        hi_rows = slice(ci * c + c // 2, (ci + 1) * c)
        q0, q1 = half_major(q)
        g0, g1 = half_major(gc)
        h0, h1 = half_major(hk_ref[rows, :])
        for s in range(sub // 2):
            key = h0[:, s:s + 1, :]
            d0 = q0 * jnp.exp(jnp.where(r8 >= s, g0 - key, NEG))
            d1 = q1 * jnp.exp(g1 - key)
            dcat_ref[lo_rows, s * dk:(s + 1) * dk] = d0.reshape(c // 2, dk).astype(BF16)
            dcat_ref[hi_rows, s * dk:(s + 1) * dk] = d1.reshape(c // 2, dk).astype(BF16)
        dcat_ref[lo_rows, (sub // 2) * dk:] = jnp.zeros((c // 2, (sub // 2) * dk), BF16)
        for s in range(sub // 2):
            key = h1[:, s:s + 1, :]
            d1 = q1 * jnp.exp(jnp.where(r8 >= s, g1 - key, NEG))
            col = (sub // 2 + s) * dk
            dcat_ref[hi_rows, col:col + dk] = d1.reshape(c // 2, dk).astype(BF16)

    group = tril_ref.shape[0]
    for g0 in range(0, tt, group):
        grows = slice(g0, g0 + group)
        gates(grows)
        for ci in range(g0 // c, (g0 + group) // c):
            state_free_part(ci)
        ps_ref[grows, :] = _dot(dcat_ref[grows, :], sum_ref[...])

    glast = [gc_ref[(ci + 1) * c - 1:(ci + 1) * c, :] for ci in range(n_chunks)]
    glast += [jnp.zeros((1, dk), F32)] * (-n_chunks % 8)
    decay_cols = jnp.exp(jnp.concatenate(glast, axis=0)).T
    st = st_ref[...]
    for ci in range(n_chunks):
        rows = slice(ci * c, (ci + 1) * c)
        opart_ref[rows, :] += _dot(qg_ref[rows, :], st.astype(BF16))
        st = decay_cols[:, ci:ci + 1] * st + u_ref[ci]
    st_ref[...] = st

    own_block = (lax.broadcasted_iota(jnp.int32, (c, dk), 1) // sub
                 == (lax.broadcasted_iota(jnp.int32, (c, dk), 0) // 8) % nsub)
    for ci in range(n_chunks):
        rows = slice(ci * c, (ci + 1) * c)
        ivb = i_ref[rows, :]
        a = jnp.where(own_block, ps_ref[rows, :], 0.0)
        od = _dot(a.astype(BF16), jnp.concatenate([ivb, jnp.zeros((dk - c, ivb.shape[1]), BF16)], axis=0))
        o = opart_ref[rows, :] + jnp.concatenate([od[(h * nsub + i) * 8:(h * nsub + i + 1) * 8, :]
                                                  for i in range(nsub) for h in range(2)], axis=0)
        o = o * lax.rsqrt(jnp.mean(o * o, axis=-1, keepdims=True) + EPS) * gn_ref[...]
        o_ref[rows, :] = (o * _silu(g_ref[rows, :].astype(F32))).astype(o_ref.dtype)


def hgrn2(u, lb_gamma, gnorm_w, *, layer, batch, seq, tt):
    h, dk, dv = HG_HEADS, HG_DK, HG_DV
    nt = seq // tt
    depth = lb_gamma.shape[0]
    r = jnp.arange(HG_GROUP * HG_CHUNK)
    tril = ((r[:, None] >= r[None, :]) & (r[:, None] // HG_CHUNK == r[None, :] // HG_CHUNK)).astype(BF16)
    key_offset = jnp.arange(HG_SUB * dk) // dk
    summer = (key_offset[:, None] == jnp.arange(dk)[None, :] % HG_SUB).astype(BF16)
    return pl.pallas_call(
        functools.partial(_hgrn_kernel, layer=layer),
        grid=(batch, h, nt),
        in_specs=[
            pl.BlockSpec((tt, 4 * dk), lambda b, hh, t: (b * nt + t, hh)),
            pl.BlockSpec((depth, dk), lambda b, hh, t: (0, hh)),
            pl.BlockSpec((1, dv), lambda b, hh, t: (0, hh)),
            pl.BlockSpec((HG_GROUP * HG_CHUNK, HG_GROUP * HG_CHUNK), lambda b, hh, t: (0, 0)),
            pl.BlockSpec((HG_SUB * dk, dk), lambda b, hh, t: (0, 0)),
        ],
        out_specs=pl.BlockSpec((tt, dv), lambda b, hh, t: (b * nt + t, hh)),
        out_shape=jax.ShapeDtypeStruct((batch * seq, h * dv), BF16),
        scratch_shapes=[
            pltpu.VMEM((dk, dv), F32),
            pltpu.VMEM((tt, dk), F32), pltpu.VMEM((tt, dk), F32), pltpu.VMEM((tt, dk), F32),
            pltpu.VMEM((tt, dk), BF16), pltpu.VMEM((tt // HG_CHUNK, dk, dv), F32),
            pltpu.VMEM((tt, HG_SUB * dk), BF16), pltpu.VMEM((tt, dk), F32), pltpu.VMEM((tt, dv), F32),
        ],
        compiler_params=_params("parallel", "parallel", "arbitrary"),
        name="hgrn2",
    )(u, lb_gamma, gnorm_w.reshape(1, h * dv), tril, summer)


QL = NSA_HPG * Q_BLOCK
FLAG_BITS = 16


def _tile_positions(qb):
    lane = lax.broadcasted_iota(jnp.int32, (1, QL), 1)
    return qb * Q_BLOCK + (lane & (Q_BLOCK - 1))


CMP_CHUNK = 128
SEL_CLASSES = 8
KEY_LANES = 128
SLOPE_PARTS = 3
LOG2E = 1.4426950408889634


def _load_query(qa_ref, q_ref, sa_ref):
    dh = NSA_HEAD_DIM
    qt = (q_ref[...].astype(F32) * (dh ** -0.5 * LOG2E)).T
    qa_ref[0:dh, :] = jnp.concatenate([qt[h * dh:(h + 1) * dh, :] for h in range(NSA_HPG)], axis=1).astype(BF16)
    qa_ref[dh:, :] = sa_ref[...]


def _nsa_select_body(nch, t_min, q_ref, sa_ref, kc_ref, vct_ref, slope_ref, oc_ref, sel_ref, flag_ref,
                     qa_ref, s_ref, pb_ref, p_ref):
    ck = CMP_CHUNK
    nk = nch * ck
    ns = nk * CMP_STRIDE // SLC_BLOCK
    qb = pl.program_id(2)
    _load_query(qa_ref, q_ref, sa_ref)
    t = _tile_positions(qb)
    slope2 = slope_ref[...] * LOG2E

    rows = []
    mc = None
    for ch in range(nch):
        rs = slice(ch * ck, (ch + 1) * ck)
        s = _dot(kc_ref[rs, :], qa_ref[...])
        if ((ch + 1) * ck - 1) * CMP_STRIDE + (CMP_BLOCK - 1) > t_min:
            end = (ch * ck + lax.broadcasted_iota(jnp.int32, (ck, 1), 0)) * CMP_STRIDE + (CMP_BLOCK - 1)
            s = jnp.where(t >= end, s, NEG)
        s_ref[rs, :] = s
        brow = slope2 * (ch * ck * CMP_STRIDE + (CMP_BLOCK - 1) - qb * Q_BLOCK).astype(F32)
        rows.append(brow)
        cm = jnp.max(s.reshape(ck // 8, 8, QL), axis=0) + brow
        mc = cm if mc is None else jnp.maximum(mc, cm)
    m = jnp.max(mc, axis=0, keepdims=True)

    lsum = None
    for ch in range(nch):
        rs = slice(ch * ck, (ch + 1) * ck)
        p = jnp.exp2(s_ref[rs, :] - (m - rows[ch]))
        ls = jnp.sum(p.reshape(ck // 8, 8, QL), axis=0)
        lsum = ls if lsum is None else lsum + ls
        s_ref[rs, :] = p
        pb_ref[rs, :] = p.astype(BF16)
    l = jnp.sum(lsum, axis=0, keepdims=True)
    inv = jnp.where(t >= CMP_BLOCK - 1, 1.0 / jnp.maximum(l, TINY), 0.0)
    oc_ref[...] = _dot(vct_ref[:, 0:nk], pb_ref[0:nk, :]) * inv

    p_ref[0:8, :] = jnp.zeros((8, Q_BLOCK), F32)
    for ch in range(nch):
        pn = s_ref[ch * ck:(ch + 1) * ck, :] * inv
        psum = pn[:, 0:Q_BLOCK]
        for h in range(1, NSA_HPG):
            psum = psum + pn[:, h * Q_BLOCK:(h + 1) * Q_BLOCK]
        p_ref[8 + ch * ck:8 + (ch + 1) * ck, :] = psum
    p_ref[8 + nk:16 + nk, :] = jnp.zeros((8, Q_BLOCK), F32)

    ratio = SLC_BLOCK // CMP_STRIDE
    imp = p_ref[pl.ds(7, ns, stride=ratio), :]
    for r in range(1, ratio + 1):
        imp = imp + p_ref[pl.ds(7 + r, ns, stride=ratio), :]

    j = lax.broadcasted_iota(jnp.int32, (ns, 1), 0)
    cur = t[:, 0:Q_BLOCK] // SLC_BLOCK
    forced = (j == 0) | (j == cur) | (j == cur - 1)
    imp = jnp.where(forced, NEG, jnp.where(j > cur, -1.0, imp))
    for _ in range(min(SLC_TOPK, ns) - 3):
        mx = jnp.max(imp, axis=0, keepdims=True)
        idx = jnp.min(jnp.where(imp == mx, j, ns), axis=0, keepdims=True)
        imp = jnp.where(j == idx, NEG, imp)
    sel = jnp.where((imp == NEG) & (j <= cur), 1.0, 0.0)
    sel_ref[0:ns, :] = sel
    if ns < sel_ref.shape[0]:
        sel_ref[ns:, :] = jnp.zeros((sel_ref.shape[0] - ns, Q_BLOCK), F32)

    chosen = jnp.max(sel, axis=1, keepdims=True)
    bit = jnp.left_shift(1, j & (FLAG_BITS - 1)).astype(F32)
    words = jnp.sum((chosen * bit).reshape(ns // FLAG_BITS, FLAG_BITS, 1), axis=1)
    flag_ref[0:ns // FLAG_BITS, :] = jnp.broadcast_to(words, (ns // FLAG_BITS, flag_ref.shape[1]))
    if ns // FLAG_BITS < flag_ref.shape[0]:
        flag_ref[ns // FLAG_BITS:, :] = jnp.zeros((flag_ref.shape[0] - ns // FLAG_BITS, flag_ref.shape[1]), F32)


def _nsa_select_kernel(q_ref, sa_ref, kc_ref, *rest):
    n_chunks = kc_ref.shape[0] // CMP_CHUNK
    n_tiles = kc_ref.shape[0] * CMP_STRIDE // Q_BLOCK
    n_cls = min(SEL_CLASSES, n_chunks)
    cls = pl.program_id(2) * n_cls // n_tiles
    for c in range(n_cls):
        @pl.when(cls == c)
        def _(c=c):
            first_tile = -(-c * n_tiles // n_cls)
            _nsa_select_body((c + 1) * n_chunks // n_cls, first_tile * Q_BLOCK, q_ref, sa_ref, kc_ref, *rest)


def nsa_select(u, q_col, slope_rows, kc, vct, slopes):
    b, g, nc = kc.shape[:3]
    dh, ql = NSA_HEAD_DIM, QL
    nqb = u.shape[0] // (b * Q_BLOCK)
    ns = nc * CMP_STRIDE // SLC_BLOCK
    tile = lambda *shape: pl.BlockSpec((None, None, None) + shape, lambda bi, gi, qi: (bi, gi, qi, 0, 0))
    return pl.pallas_call(
        _nsa_select_kernel,
        grid=(b, g, nqb),
        in_specs=[
            pl.BlockSpec((Q_BLOCK, NSA_HPG * dh), lambda bi, gi, qi: (bi * nqb + qi, q_col + gi)),
            pl.BlockSpec((None, KEY_LANES - dh, ql), lambda bi, gi, qi: (gi, 0, 0)),
            pl.BlockSpec((None, None, nc, KEY_LANES), lambda bi, gi, qi: (bi, gi, 0, 0)),
            pl.BlockSpec((None, None, dh, nc), lambda bi, gi, qi: (bi, gi, 0, 0)),
            pl.BlockSpec((None, 1, ql), lambda bi, gi, qi: (gi, 0, 0)),
        ],
        out_specs=[tile(dh, ql), tile(ns, Q_BLOCK), tile(ns // FLAG_BITS, 128)],
        out_shape=[
            jax.ShapeDtypeStruct((b, g, nqb, dh, ql), F32),
            jax.ShapeDtypeStruct((b, g, nqb, ns, Q_BLOCK), F32),
            jax.ShapeDtypeStruct((b, g, nqb, ns // FLAG_BITS, 128), F32),
        ],
        scratch_shapes=[
            pltpu.VMEM((KEY_LANES, ql), BF16), pltpu.VMEM((nc, ql), F32), pltpu.VMEM((nc, ql), BF16),
            pltpu.VMEM((nc + 16, Q_BLOCK), F32),
        ],
        compiler_params=_params("parallel", "parallel", "parallel"),
        name="nsa_select",
    )(u, slope_rows, kc, vct, slopes)


ATT_GROUP = 8
LIST_PAD = 3 * ATT_GROUP
DIAG_BLOCKS = Q_BLOCK // SLC_BLOCK
WIN_BLOCKS = WINDOW // SLC_BLOCK


def _score_blocks(k_src, blocks, qa_ref, kcat_ref, s_ref):
    kb = SLC_BLOCK
    for b, j in enumerate(blocks):
        kcat_ref[b * kb:(b + 1) * kb, :] = k_src[jnp.maximum(j, 0)]
    n = len(blocks) * kb
    s_ref[0:n, :] = _dot(kcat_ref[0:n, :], qa_ref[...])


def _softmax_step(v_src, blocks, masks, sel_rows, t, slope2, blk0, s_ref, vcat_ref, p_ref, m_ref, l_ref, acc_ref):
    nb = len(blocks)
    kb = SLC_BLOCK
    for b, j in enumerate(blocks):
        vcat_ref[b * kb:(b + 1) * kb, :] = v_src[jnp.maximum(j, 0)]
    rows = []
    mc = None
    for b, j in enumerate(blocks):
        brow = slope2 * ((j - blk0) * kb).astype(F32)
        if sel_rows[b] is not None:
            brow = jnp.where(sel_rows[b] > 0.0, brow, NEG)
        brow = jnp.where(j >= 0, brow, NEG)
        rows.append(brow)
        s = s_ref[b * kb:(b + 1) * kb, :]
        if masks[b] is not None:
            kpos = j * kb + lax.broadcasted_iota(jnp.int32, (kb, 1), 0)
            valid = (t >= kpos) if masks[b] == "causal" else (t - kpos < WINDOW)
            s = jnp.where(valid, s, NEG)
            s_ref[b * kb:(b + 1) * kb, :] = s
        cm = jnp.max(s.reshape(kb // 8, 8, QL), axis=0) + brow
        mc = cm if mc is None else jnp.maximum(mc, cm)
    m_old = m_ref[...]
    m_new = jnp.maximum(m_old, jnp.max(mc, axis=0, keepdims=True))
    alpha = jnp.exp2(m_old - m_new)
    m_ref[...] = m_new

    lsum = None
    for b in range(nb):
        p = jnp.exp2(s_ref[b * kb:(b + 1) * kb, :] - (m_new - rows[b]))
        ls = jnp.sum(p.reshape(kb // 8, 8, QL), axis=0)
        lsum = ls if lsum is None else lsum + ls
        p_ref[b * kb:(b + 1) * kb, :] = p.astype(BF16)
    l_ref[...] = alpha * l_ref[...] + lsum
    acc_ref[...] = alpha * acc_ref[...] + _dot_tn(vcat_ref[0:nb * kb, :], p_ref[0:nb * kb, :])


def _nsa_attend_kernel(fw_ref, q_ref, sa_ref, ks_ref, vs_ref, kw_ref, vw_ref, sel_ref, oc_ref, gate_ref, slope_ref,
                       o_ref, qa_ref, kcat_ref, s_ref, vcat_ref, p_ref, m_ref, l_ref, acc_ref,
                       kcat_w_ref, s_w_ref, vcat_w_ref, p_w_ref, m_w_ref, l_w_ref, acc_w_ref, list_ref, *,
                       words_per_tile):
    bi, gi, qb = pl.program_id(0), pl.program_id(1), pl.program_id(2)
    tile_id = (bi * pl.num_programs(1) + gi) * pl.num_programs(2) + qb
    _load_query(qa_ref, q_ref, sa_ref)
    t = _tile_positions(qb)
    slope2 = slope_ref[...] * LOG2E
    blk0 = qb * DIAG_BLOCKS
    diag = [blk0 + i for i in range(DIAG_BLOCKS)]
    past = [blk0 - WIN_BLOCKS + i for i in range(WIN_BLOCKS)]

    def sel_row(j):
        row = sel_ref[pl.ds(jnp.maximum(j, 0), 1), :]
        return jnp.concatenate([row] * NSA_HPG, axis=1)

    def reset(m, l, acc):
        m[...] = jnp.full_like(m, NEG)
        l[...] = jnp.zeros_like(l)
        acc[...] = jnp.zeros_like(acc)

    def result(l, acc):
        return acc[...] * (1.0 / jnp.maximum(jnp.sum(l[...], axis=0, keepdims=True), TINY))

    def scan_flags(w, n):
        base = w * FLAG_BITS
        word = fw_ref[tile_id * words_per_tile + w] & ((1 << jnp.minimum(blk0 - base, FLAG_BITS)) - 1)
        for i in range(FLAG_BITS):
            list_ref[n] = base + i
            n = n + ((word >> i) & 1)
        return n

    n_sel = lax.fori_loop(0, (blk0 + FLAG_BITS - 1) // FLAG_BITS, scan_flags, 0)
    for i in range(LIST_PAD):
        list_ref[n_sel + i] = -1

    def listed(i):
        return [list_ref[i * ATT_GROUP + b] for b in range(ATT_GROUP)]

    _score_blocks(kw_ref, past + diag, qa_ref, kcat_w_ref, s_w_ref)
    _score_blocks(ks_ref, diag, qa_ref, kcat_ref.at[0], s_ref.at[0])
    _score_blocks(ks_ref, listed(0), qa_ref, kcat_ref.at[1], s_ref.at[1])

    slc_state = (m_ref, l_ref, acc_ref)
    reset(*slc_state)
    _softmax_step(vs_ref, diag, ["causal"] * DIAG_BLOCKS, [sel_row(j) for j in diag], t, slope2, blk0,
                  s_ref.at[0], vcat_ref, p_ref, *slc_state)

    win_state = (m_w_ref, l_w_ref, acc_w_ref)
    reset(*win_state)
    masks = ["window"] * DIAG_BLOCKS + [None] * (WIN_BLOCKS - DIAG_BLOCKS) + ["causal"] * DIAG_BLOCKS
    _softmax_step(vw_ref, past + diag, masks, [None] * len(masks), t, slope2, blk0,
                  s_w_ref, vcat_w_ref, p_w_ref, *win_state)

    n_steps = (n_sel + ATT_GROUP - 1) // ATT_GROUP

    def slc_step(i, slot):
        _score_blocks(ks_ref, listed(i + 1), qa_ref, kcat_ref.at[slot], s_ref.at[slot])
        blocks = listed(i)
        _softmax_step(vs_ref, blocks, [None] * ATT_GROUP, [sel_row(j) for j in blocks], t, slope2, blk0,
                      s_ref.at[1 - slot], vcat_ref, p_ref, *slc_state)

    def slc_body(i2, carry):
        slc_step(2 * i2, 0)

        @pl.when(2 * i2 + 1 < n_steps)
        def _():
            slc_step(2 * i2 + 1, 1)

        return carry

    lax.fori_loop(0, (n_steps + 1) // 2, slc_body, 0)
    o_slc = result(l_ref, acc_ref)
    o_win = result(l_w_ref, acc_w_ref)

    gate = _sigmoid(gate_ref[...])
    o = gate[0:1, :] * oc_ref[...] + gate[1:2, :] * o_slc + gate[2:3, :] * o_win
    o = jnp.concatenate([o[:, h * Q_BLOCK:(h + 1) * Q_BLOCK] for h in range(NSA_HPG)], axis=0)
    o_ref[...] = o.T.astype(o_ref.dtype)


def nsa_attend(flag_words, u, q_col, slope_rows, ks, vs, kw, vw, sel, oc, gates, slopes):
    b, g, ns = ks.shape[:3]
    dh, ql = NSA_HEAD_DIM, QL
    nqb = u.shape[0] // (b * Q_BLOCK)
    gkeys = ATT_GROUP * SLC_BLOCK
    wkeys = (WIN_BLOCKS + DIAG_BLOCKS) * SLC_BLOCK
    tile = lambda *shape: pl.BlockSpec((None, None, None) + shape, lambda bi, gi, qi, fw: (bi, gi, qi, 0, 0))
    seq = lambda *shape: pl.BlockSpec((None, None) + shape, lambda bi, gi, qi, fw: (bi, gi, 0, 0, 0))
    state = [pltpu.VMEM((1, ql), F32), pltpu.VMEM((8, ql), F32), pltpu.VMEM((dh, ql), F32)]
    grid_spec = pltpu.PrefetchScalarGridSpec(
        num_scalar_prefetch=1,
        grid=(b, g, nqb),
        in_specs=[
            pl.BlockSpec((Q_BLOCK, NSA_HPG * dh), lambda bi, gi, qi, fw: (bi * nqb + qi, q_col + gi)),
            pl.BlockSpec((None, KEY_LANES - dh, ql), lambda bi, gi, qi, fw: (gi, 0, 0)),
            seq(ns, SLC_BLOCK, KEY_LANES), seq(ns, SLC_BLOCK, dh),
            seq(ns, SLC_BLOCK, KEY_LANES), seq(ns, SLC_BLOCK, dh),
            tile(ns, Q_BLOCK), tile(dh, ql), tile(3, ql),
            pl.BlockSpec((None, 1, ql), lambda bi, gi, qi, fw: (gi, 0, 0)),
        ],
        out_specs=pl.BlockSpec((Q_BLOCK, NSA_HPG * dh), lambda bi, gi, qi, fw: (bi * nqb + qi, gi)),
        scratch_shapes=[
            pltpu.VMEM((KEY_LANES, ql), BF16),
            pltpu.VMEM((2, gkeys, KEY_LANES), BF16), pltpu.VMEM((2, gkeys, ql), F32),
            pltpu.VMEM((gkeys, dh), BF16), pltpu.VMEM((gkeys, ql), BF16), *state,
            pltpu.VMEM((wkeys, KEY_LANES), BF16), pltpu.VMEM((wkeys, ql), F32),
            pltpu.VMEM((wkeys, dh), BF16), pltpu.VMEM((wkeys, ql), BF16), *state,
            pltpu.SMEM((ns + LIST_PAD,), jnp.int32),
        ],
    )
    return pl.pallas_call(
        functools.partial(_nsa_attend_kernel, words_per_tile=ns // FLAG_BITS),
        grid_spec=grid_spec,
        out_shape=jax.ShapeDtypeStruct((b * nqb * Q_BLOCK, g * NSA_HPG * dh), BF16),
        compiler_params=_params("parallel", "parallel", "arbitrary"),
        name="nsa_attend",
    )(flag_words, u, slope_rows, ks, vs, kw, vw, sel, oc, gates, slopes)


def nsa_mixer(u, q_col, kc, vc, ks, vs, kw, vw, gate_logits, cmp_pos, cmp_w1, cmp_b1, cmp_w2, cmp_b2, *,
              batch, seq):
    g, hpg, dh = NSA_KV_GROUPS, NSA_HPG, NSA_HEAD_DIM
    nqb, ns, nc = seq // Q_BLOCK, seq // SLC_BLOCK, seq // CMP_STRIDE

    gates = gate_logits.astype(F32).reshape(batch, nqb, Q_BLOCK, 3, g, hpg).transpose(0, 4, 1, 3, 5, 2)
    gates = gates.reshape(batch, g, nqb, 3, QL)
    slopes = 2.0 ** (-8.0 * jnp.arange(1, NSA_HEADS + 1, dtype=F32) / NSA_HEADS)
    slopes = jnp.repeat(slopes.reshape(g, 1, hpg), Q_BLOCK, axis=2)

    def halves(x):
        x = x.reshape(batch, nc, CMP_STRIDE, g, dh).transpose(0, 3, 1, 2, 4)
        return x.reshape(batch * g, nc, CMP_STRIDE * dh)

    def value_blocks(x):
        return x.reshape(batch, ns, SLC_BLOCK, g, dh).transpose(0, 3, 1, 2, 4)

    def key_blocks(x):
        r = jnp.arange(SLC_BLOCK, dtype=F32).astype(BF16).reshape(1, 1, 1, SLC_BLOCK, 1)
        r = jnp.broadcast_to(r, (batch, g, ns, SLC_BLOCK, SLOPE_PARTS))
        pad = jnp.zeros((batch, g, ns, SLC_BLOCK, KEY_LANES - dh - SLOPE_PARTS), BF16)
        return jnp.concatenate([value_blocks(x), r, pad], axis=-1)

    s2 = slopes * LOG2E
    parts = []
    for _ in range(SLOPE_PARTS):
        part = s2.astype(BF16)
        parts.append(part)
        s2 = s2 - part.astype(F32)
    slope_rows = jnp.concatenate(parts + [jnp.zeros((g, KEY_LANES - dh - SLOPE_PARTS, QL), BF16)], axis=1)

    cmp = compress_tokens(jnp.stack([halves(kc), halves(vc)]), cmp_pos, cmp_w1, cmp_b1, cmp_w2, cmp_b2)
    cmp = cmp.astype(BF16).reshape(2, batch, g, nc, dh)
    r = (jnp.arange(nc) % CMP_CHUNK * CMP_STRIDE).astype(BF16).reshape(1, 1, nc, 1)
    k_cmp = jnp.concatenate([cmp[0], jnp.broadcast_to(r, (batch, g, nc, SLOPE_PARTS)),
                             jnp.zeros((batch, g, nc, KEY_LANES - dh - SLOPE_PARTS), BF16)], axis=-1)
    v_cmp_t = cmp[1].transpose(0, 1, 3, 2)

    oc, sel, flags = nsa_select(u, q_col, slope_rows, k_cmp, v_cmp_t, slopes)
    flag_words = flags[:, :, :, :, 0].astype(jnp.int32).reshape(-1)
    return nsa_attend(flag_words, u, q_col, slope_rows, key_blocks(ks), value_blocks(vs), key_blocks(kw),
                      value_blocks(vw), sel, oc, gates, slopes)


ROW_TILE = 1024
COL_TILE = 1024
OUT_ROW_TILE = 512
FFN_ROW_TILE = 1024
FFN_HID_TILE = 512
CONV_SEQ_TILE = 512
HGRN_SEQ_TILE = 1024


def _even_layer(x, nw, w_in, conv_w, conv_b, ln_w, ln_b, cmp_pos, cmp_w1, cmp_b1, cmp_w2, cmp_b2, w_out, *,
                batch, seq):
    c = conv_w.shape[-1]
    nq = NSA_HEADS * NSA_HEAD_DIM
    nkv = NSA_KV_GROUPS * NSA_HEAD_DIM
    n_in = w_in.shape[1]
    n_pad = -(-n_in // COL_TILE) * COL_TILE
    w_in = jnp.pad(w_in, ((0, 0), (0, n_pad - n_in))).astype(BF16)
    u = norm_matmul(x, nw, w_in, tm=ROW_TILE, tn=COL_TILE)
    a_out = conformer_conv(u, conv_w.reshape(CONV_WIDTH, c), conv_b, ln_w, ln_b, seq=seq, ts=CONV_SEQ_TILE)
    off = 2 * c
    q_col = off // (NSA_HPG * NSA_HEAD_DIM)
    off += nq
    kvs = [u[:, off + i * nkv:off + (i + 1) * nkv] for i in range(6)]
    off += 6 * nkv
    gate_logits = u[:, off:off + 3 * NSA_HEADS]
    b_out = nsa_mixer(u, q_col, *kvs, gate_logits, cmp_pos, cmp_w1, cmp_b1, cmp_w2, cmp_b2,
                      batch=batch, seq=seq)
    return matmul_residual(a_out, b_out, 0, 0, w_out.astype(BF16), x, tm=OUT_ROW_TILE, tn=w_out.shape[1])


def _odd_layer(x, nw, w_in, lb_gamma, gnorm_w, w_out, *, layer, batch, seq):
    d = w_in.shape[0]
    w_in = w_in.reshape(d, 4, HG_HEADS, HG_DK).transpose(0, 2, 1, 3).reshape(d, 4 * HG_HEADS * HG_DK)
    u = norm_matmul(x, nw, w_in.astype(BF16), tm=ROW_TILE, tn=COL_TILE)
    o = hgrn2(u, lb_gamma, gnorm_w, layer=layer, batch=batch, seq=seq, tt=HGRN_SEQ_TILE)
    return matmul_residual(o, o, 0, 1, w_out.astype(BF16), x, tm=OUT_ROW_TILE, tn=w_out.shape[1])


def kernel(x, norm_w, final_norm_w, ev_w_in, ev_conv_w, ev_conv_b, ev_conv_ln_w, ev_conv_ln_b, ev_cmp_pos,
           ev_cmp_w1, ev_cmp_b1, ev_cmp_w2, ev_cmp_b2, ev_w_out, od_w_in, od_lb_gamma, od_gnorm_w, od_w_out,
           ffn_w_gu, ffn_w_down):
    batch, seq, d = x.shape
    depth = norm_w.shape[0]
    xs = x.reshape(batch * seq, d)
    for layer in range(depth):
        i = layer // 2
        if layer % 2 == 0:
            xs = _even_layer(xs, norm_w[layer, 0], ev_w_in[i], ev_conv_w[i], ev_conv_b[i], ev_conv_ln_w[i],
                             ev_conv_ln_b[i], ev_cmp_pos[i], ev_cmp_w1[i], ev_cmp_b1[i], ev_cmp_w2[i],
                             ev_cmp_b2[i], ev_w_out[i], batch=batch, seq=seq)
        else:
            xs = _odd_layer(xs, norm_w[layer, 0], od_w_in[i], od_lb_gamma.astype(F32), od_gnorm_w[i],
                            od_w_out[i], layer=layer, batch=batch, seq=seq)
        xs = ffn_block(xs, norm_w[layer, 1], ffn_w_gu[layer].astype(BF16), ffn_w_down[layer].astype(BF16),
                       final_norm_w, tm=FFN_ROW_TILE, th=FFN_HID_TILE, final_norm=layer == depth - 1)
    return xs.reshape(batch, seq, d)
```

```python
import functools

import jax
import jax.numpy as jnp
import numpy as np
from jax import lax
from jax.experimental import pallas as pl
from jax.experimental.pallas import tpu as pltpu

F32 = jnp.float32
BF16 = jnp.bfloat16

EPS = 1e-6
TINY = 1e-30
NEG = -1e30

VMEM_LIMIT_BYTES = 56 * 1024 * 1024

CONV_WIDTH = 31
NSA_HEADS = 16
NSA_HEAD_DIM = 64
NSA_KV_GROUPS = 4
NSA_HPG = NSA_HEADS // NSA_KV_GROUPS
CMP_STRIDE = 16
CMP_BLOCK = 32
SLC_BLOCK = 64
SLC_TOPK = 16
WINDOW = 512
Q_BLOCK = 128
HG_HEADS = 16
HG_DK = 128
HG_DV = 128
HG_CHUNK = 64
HG_SUB = 16
HG_GROUP = 4


def _params(*sem):
    return pltpu.CompilerParams(dimension_semantics=sem, vmem_limit_bytes=VMEM_LIMIT_BYTES)


def _sigmoid(x):
    return 1.0 / (1.0 + jnp.exp(-x))


def _silu(x):
    return x * _sigmoid(x)


def _dot(a, b):
    return jnp.dot(a, b, preferred_element_type=F32)


def _dot_nt(a, b):
    return lax.dot_general(a, b, (((1,), (1,)), ((), ())), preferred_element_type=F32)


def _dot_tn(a, b):
    return lax.dot_general(a, b, (((0,), (0,)), ((), ())), preferred_element_type=F32)


def _norm_matmul_kernel(x_ref, nw_ref, w_ref, o_ref, h_ref):
    @pl.when(pl.program_id(1) == 0)
    def _():
        x = x_ref[...]
        ms = jnp.mean(x * x, axis=-1, keepdims=True)
        h_ref[...] = (x * lax.rsqrt(ms + EPS) * nw_ref[...]).astype(BF16)

    o_ref[...] = _dot(h_ref[...], w_ref[...]).astype(o_ref.dtype)


def norm_matmul(x, nw, w, *, tm, tn, out_dtype=BF16):
    m, k = x.shape
    n = w.shape[1]
    return pl.pallas_call(
        _norm_matmul_kernel,
        grid=(m // tm, n // tn),
        in_specs=[
            pl.BlockSpec((tm, k), lambda i, j: (i, 0)),
            pl.BlockSpec((1, k), lambda i, j: (0, 0)),
            pl.BlockSpec((k, tn), lambda i, j: (0, j)),
        ],
        out_specs=pl.BlockSpec((tm, tn), lambda i, j: (i, j)),
        out_shape=jax.ShapeDtypeStruct((m, n), out_dtype),
        scratch_shapes=[pltpu.VMEM((tm, k), BF16)],
        compiler_params=_params("parallel", "arbitrary"),
        name="norm_matmul",
    )(x, nw.reshape(1, k), w)


def _matmul_res_kernel(a1_ref, a2_ref, w1_ref, w2_ref, r_ref, o_ref):
    acc = _dot(a1_ref[...], w1_ref[...]) + _dot(a2_ref[...], w2_ref[...])
    o_ref[...] = r_ref[...] + acc


def matmul_residual(a1, a2, blk1, blk2, w, res, *, tm, tn):
    m = res.shape[0]
    k, n = w.shape
    kh = k // 2
    return pl.pallas_call(
        _matmul_res_kernel,
        grid=(m // tm, n // tn),
        in_specs=[
            pl.BlockSpec((tm, kh), lambda i, j: (i, blk1)),
            pl.BlockSpec((tm, kh), lambda i, j: (i, blk2)),
            pl.BlockSpec((kh, tn), lambda i, j: (0, j)),
            pl.BlockSpec((kh, tn), lambda i, j: (1, j)),
            pl.BlockSpec((tm, tn), lambda i, j: (i, j)),
        ],
        out_specs=pl.BlockSpec((tm, tn), lambda i, j: (i, j)),
        out_shape=jax.ShapeDtypeStruct((m, n), F32),
        compiler_params=_params("parallel", "arbitrary"),
        name="matmul_residual",
    )(a1, a2, w, w, res)


def _ffn_kernel(x_ref, nw_ref, wg_ref, wu_ref, wd_ref, fw_ref, o_ref, h_ref, *, final_norm):
    j = pl.program_id(1)

    @pl.when(j == 0)
    def _():
        x = x_ref[...]
        ms = jnp.mean(x * x, axis=-1, keepdims=True)
        h_ref[...] = (x * lax.rsqrt(ms + EPS) * nw_ref[...]).astype(BF16)
        o_ref[...] = x

    h = h_ref[...]
    a = _dot(h, wg_ref[...])
    b = _dot(h, wu_ref[...])
    z = (_silu(a) * b).astype(BF16)
    o_ref[...] += _dot(z, wd_ref[...])

    if final_norm:
        @pl.when(j == pl.num_programs(1) - 1)
        def _():
            y = o_ref[...]
            ms = jnp.mean(y * y, axis=-1, keepdims=True)
            o_ref[...] = y * lax.rsqrt(ms + EPS) * fw_ref[...]


def ffn_block(x, nw, w_gu, w_down, final_w, *, tm, th, final_norm):
    m, d = x.shape
    hid = w_down.shape[0]
    nh = hid // th
    return pl.pallas_call(
        functools.partial(_ffn_kernel, final_norm=final_norm),
        grid=(m // tm, nh),
        in_specs=[
            pl.BlockSpec((tm, d), lambda i, j: (i, 0)),
            pl.BlockSpec((1, d), lambda i, j: (0, 0)),
            pl.BlockSpec((d, th), lambda i, j: (0, j)),
            pl.BlockSpec((d, th), lambda i, j: (0, j + nh)),
            pl.BlockSpec((th, d), lambda i, j: (j, 0)),
            pl.BlockSpec((1, d), lambda i, j: (0, 0)),
        ],
        out_specs=pl.BlockSpec((tm, d), lambda i, j: (i, 0)),
        out_shape=jax.ShapeDtypeStruct((m, d), F32),
        scratch_shapes=[pltpu.VMEM((tm, d), BF16)],
        compiler_params=_params("parallel", "arbitrary"),
        name="ffn_block",
    )(x, nw.reshape(1, d), w_gu, w_gu, w_down, final_w.reshape(1, d))


CONV_HALO = 32
CONV_ROWS = 32


def _conv_kernel(a_ref, g_ref, ah_ref, gh_ref, cw_ref, cb_ref, lw_ref, lb_ref, o_ref, hs_ref, sh_ref, cv_ref, *,
                 tiles_per_seq):
    ts = a_ref.shape[0]
    first = (pl.program_id(0) % tiles_per_seq) == 0
    hprev = ah_ref[...].astype(F32) * _sigmoid(gh_ref[...].astype(F32))
    hs_ref[0:CONV_HALO, :] = jnp.where(first, 0.0, hprev)
    hs_ref[CONV_HALO:CONV_HALO + ts, :] = a_ref[...].astype(F32) * _sigmoid(g_ref[...].astype(F32))
    off = CONV_HALO - (CONV_WIDTH - 1)
    n_shifted = sh_ref.shape[1]
    for r in range(1, 8):
        sh_ref[r - 1] = hs_ref[r:r + n_shifted, :]

    def body(i, carry):
        base = pl.multiple_of(i * CONV_ROWS, CONV_ROWS)
        acc = jnp.zeros((CONV_ROWS // 8, 8, hs_ref.shape[1]), F32) + cb_ref[...]
        for w in range(CONV_WIDTH):
            r, a = (off + w) % 8, (off + w) // 8 * 8
            src = hs_ref if r == 0 else sh_ref.at[r - 1]
            rows = src[pl.ds(base + a, CONV_ROWS), :].reshape(CONV_ROWS // 8, 8, -1)
            acc = acc + rows * cw_ref[w * 8:(w + 1) * 8, :]
        cv_ref[pl.ds(base, CONV_ROWS), :] = acc.reshape(CONV_ROWS, -1)
        return carry

    lax.fori_loop(0, ts // CONV_ROWS, body, 0)

    y = cv_ref[...]
    mu = jnp.mean(y, axis=-1, keepdims=True)
    d = y - mu
    var = jnp.mean(d * d, axis=-1, keepdims=True)
    hn = d * lax.rsqrt(var + EPS) * lw_ref[...] + lb_ref[...]
    o_ref[...] = _silu(hn).astype(o_ref.dtype)


def conformer_conv(u, conv_w, conv_b, ln_w, ln_b, *, seq, ts):
    t = u.shape[0]
    c = conv_w.shape[1]
    hb = ts // CONV_HALO
    return pl.pallas_call(
        functools.partial(_conv_kernel, tiles_per_seq=seq // ts),
        grid=(t // ts,),
        in_specs=[
            pl.BlockSpec((ts, c), lambda i: (i, 0)),
            pl.BlockSpec((ts, c), lambda i: (i, 1)),
            pl.BlockSpec((CONV_HALO, c), lambda i: (jnp.maximum(i * hb - 1, 0), 0)),
            pl.BlockSpec((CONV_HALO, c), lambda i: (jnp.maximum(i * hb - 1, 0), 1)),
            pl.BlockSpec((CONV_WIDTH * 8, c), lambda i: (0, 0)),
            pl.BlockSpec((1, c), lambda i: (0, 0)),
            pl.BlockSpec((1, c), lambda i: (0, 0)),
            pl.BlockSpec((1, c), lambda i: (0, 0)),
        ],
        out_specs=pl.BlockSpec((ts, c), lambda i: (i, 0)),
        out_shape=jax.ShapeDtypeStruct((t, c), BF16),
        scratch_shapes=[pltpu.VMEM((CONV_HALO + ts, c), F32), pltpu.VMEM((7, CONV_HALO + ts - 8, c), F32),
                        pltpu.VMEM((ts, c), F32)],
        compiler_params=_params("parallel"),
        name="conformer_conv",
    )(u, u, u, u, jnp.repeat(conv_w, 8, axis=0), conv_b.reshape(1, c), ln_w.reshape(1, c), ln_b.reshape(1, c))


def _compress_kernel(x_ref, pos_ref, w1_ref, b1_ref, w2_ref, b2_ref, o_ref, sh_ref):
    n = x_ref.shape[0]
    hw = x_ref.shape[1]
    x = x_ref[...].astype(F32)
    xa = (x + pos_ref[:, 0:hw]).astype(BF16)
    xb = (x + pos_ref[:, hw:2 * hw]).astype(BF16)
    p1 = _dot(xa, w1_ref[0:hw, :])
    sh_ref[0:n, :] = _dot(xb, w1_ref[hw:2 * hw, :])
    sh_ref[n:n + 8, :] = jnp.zeros((8, sh_ref.shape[1]), F32)
    hid = p1 + sh_ref[1:n + 1, :] + b1_ref[...]
    o_ref[...] = _dot(_silu(hid).astype(BF16), w2_ref[...]) + b2_ref[...]


def compress_tokens(xh, pos, w1, b1, w2, b2):
    _, bg, n, hw = xh.shape
    hid = w1.shape[-1]
    dh = w2.shape[-1]
    return pl.pallas_call(
        _compress_kernel,
        grid=(2, bg),
        in_specs=[
            pl.BlockSpec((None, None, n, hw), lambda s, i: (s, i, 0, 0)),
            pl.BlockSpec((None, 1, 2 * hw), lambda s, i: (s, 0, 0)),
            pl.BlockSpec((None, 2 * hw, hid), lambda s, i: (s, 0, 0)),
            pl.BlockSpec((None, 1, hid), lambda s, i: (s, 0, 0)),
            pl.BlockSpec((None, hid, dh), lambda s, i: (s, 0, 0)),
            pl.BlockSpec((None, 1, dh), lambda s, i: (s, 0, 0)),
        ],
        out_specs=pl.BlockSpec((None, None, n, dh), lambda s, i: (s, i, 0, 0)),
        out_shape=jax.ShapeDtypeStruct((2, bg, n, dh), F32),
        scratch_shapes=[pltpu.VMEM((n + 8, hid), F32)],
        compiler_params=_params("parallel", "parallel"),
        name="compress_tokens",
    )(xh, pos.reshape(2, 1, 2 * hw), w1.astype(BF16), b1.reshape(2, 1, hid), w2.astype(BF16),
      b2.reshape(2, 1, dh))


def _split_bf16(x):
    hi = x.astype(BF16)
    lo = (x - hi.astype(F32)).astype(BF16)
    return hi, lo


def _hgrn_kernel(x_ref, gamma_ref, gn_ref, tril_ref, sum_ref, o_ref,
                 st_ref, gc_ref, k_ref, hk_ref, qg_ref, u_ref, dcat_ref, ps_ref, opart_ref, *, layer):
    c, sub = HG_CHUNK, HG_SUB
    nsub = c // sub
    tt, dk = o_ref.shape
    n_chunks = tt // c
    q_ref, f_ref, i_ref, g_ref = (x_ref.at[:, i * dk:(i + 1) * dk] for i in range(4))

    @pl.when(pl.program_id(2) == 0)
    def _():
        st_ref[...] = jnp.zeros_like(st_ref)

    gamma = gamma_ref[...]
    e = jnp.exp(gamma - jnp.max(gamma, axis=0, keepdims=True))
    sm = e / jnp.sum(e, axis=0, keepdims=True)
    lb = jnp.sum(sm[0:layer + 1, :], axis=0, keepdims=True) - sm[0:1, :]

    def gates(rows):
        f = lb + (1.0 - lb) * _sigmoid(f_ref[rows, :].astype(F32))
        k = 1.0 - f
        hi, lo = _split_bf16(jnp.log2(jnp.maximum(f, TINY)))
        gc = _dot(tril_ref[...], hi) + _dot(tril_ref[...], lo)
        gc_ref[rows, :] = gc
        k_ref[rows, :] = k
        hk_ref[rows, :] = gc - jnp.log2(k)
        qg_ref[rows, :] = (q_ref[rows, :].astype(F32) * jnp.exp2(gc)).astype(BF16)

    r8 = lax.broadcasted_iota(jnp.int32, (1, 8, 1), 1)
    srow = lax.broadcasted_iota(jnp.int32, (c, 1), 0)
    tcol = lax.broadcasted_iota(jnp.int32, (c, c), 1)

    def half_major(x):
        return [jnp.concatenate([x[(2 * i + h) * 8:(2 * i + h + 1) * 8, :] for i in range(nsub)], axis=0)
                .reshape(nsub, 8, -1) for h in range(2)]

    def state_free_part(ci):
        rows = slice(ci * c, (ci + 1) * c)
        q = q_ref[rows, :].astype(F32)
        gc = gc_ref[rows, :]
        k = k_ref[rows, :]
        ivb = i_ref[rows, :]
        glast = gc[c - 1:c, :]
        u_ref[ci] = _dot_tn((k * jnp.exp2(glast - gc)).astype(BF16), ivb)

        qhat = [jnp.zeros((sub, dk), F32)]
        khat = []
        for si in range(1, nsub):
            lo_r, hi_r = si * sub, (si + 1) * sub
            ref = gc[lo_r - 1:lo_r, :]
            qhat.append(q[lo_r:hi_r, :] * jnp.exp2(gc[lo_r:hi_r, :] - ref))
            khat.append(k * jnp.exp2(jnp.where(srow < lo_r, ref - gc, NEG)))
        at = _dot_nt(jnp.concatenate(khat, axis=0).astype(BF16), jnp.concatenate(qhat, axis=0).astype(BF16))
        at_off = jnp.zeros((c, c), F32)
        for si in range(1, nsub):
            at_off = jnp.where(tcol // sub == si, at[(si - 1) * c:si * c, :], at_off)
        opart_ref[rows, :] = _dot_tn(at_off.astype(BF16), ivb)

        lo_rows = slice(ci * c, ci * c + c // 2)
        hi_rows = slice(ci * c + c // 2, (ci + 1) * c)
        q0, q1 = half_major(q)
        g0, g1 = half_major(gc)
        h0, h1 = half_major(hk_ref[rows, :])
        for s in range(sub // 2):
            key = h0[:, s:s + 1, :]
            d0 = q0 * jnp.exp2(jnp.where(r8 >= s, g0 - key, NEG))
            d1 = q1 * jnp.exp2(g1 - key)
            dcat_ref[lo_rows, s * dk:(s + 1) * dk] = d0.reshape(c // 2, dk).astype(BF16)
            dcat_ref[hi_rows, s * dk:(s + 1) * dk] = d1.reshape(c // 2, dk).astype(BF16)
        dcat_ref[lo_rows, (sub // 2) * dk:] = jnp.zeros((c // 2, (sub // 2) * dk), BF16)
        for s in range(sub // 2):
            key = h1[:, s:s + 1, :]
            d1 = q1 * jnp.exp2(jnp.where(r8 >= s, g1 - key, NEG))
            col = (sub // 2 + s) * dk
            dcat_ref[hi_rows, col:col + dk] = d1.reshape(c // 2, dk).astype(BF16)

    group = tril_ref.shape[0]
    for g0 in range(0, tt, group):
        grows = slice(g0, g0 + group)
        gates(grows)
        for ci in range(g0 // c, (g0 + group) // c):
            state_free_part(ci)
        ps_ref[grows, :] = _dot(dcat_ref[grows, :], sum_ref[...])

    glast = [gc_ref[(ci + 1) * c - 1:(ci + 1) * c, :] for ci in range(n_chunks)]
    glast += [jnp.zeros((1, dk), F32)] * (-n_chunks % 8)
    decay_cols = jnp.exp2(jnp.concatenate(glast, axis=0)).T
    st = st_ref[...]
    for ci in range(n_chunks):
        rows = slice(ci * c, (ci + 1) * c)
        opart_ref[rows, :] += _dot(qg_ref[rows, :], st.astype(BF16))
        st = decay_cols[:, ci:ci + 1] * st + u_ref[ci]
    st_ref[...] = st

    own_block = (lax.broadcasted_iota(jnp.int32, (c, dk), 1) // sub
                 == (lax.broadcasted_iota(jnp.int32, (c, dk), 0) // 8) % nsub)
    for ci in range(n_chunks):
        rows = slice(ci * c, (ci + 1) * c)
        ivb = i_ref[rows, :]
        a = jnp.where(own_block, ps_ref[rows, :], 0.0)
        od = _dot(a.astype(BF16), jnp.concatenate([ivb, jnp.zeros((dk - c, ivb.shape[1]), BF16)], axis=0))
        o = opart_ref[rows, :] + jnp.concatenate([od[(h * nsub + i) * 8:(h * nsub + i + 1) * 8, :]
                                                  for i in range(nsub) for h in range(2)], axis=0)
        o = o * lax.rsqrt(jnp.mean(o * o, axis=-1, keepdims=True) + EPS) * gn_ref[...]
        o_ref[rows, :] = (o * _silu(g_ref[rows, :].astype(F32))).astype(o_ref.dtype)


def hgrn2(u, lb_gamma, gnorm_w, *, layer, batch, seq, tt):
    h, dk, dv = HG_HEADS, HG_DK, HG_DV
    nt = seq // tt
    depth = lb_gamma.shape[0]
    r = jnp.arange(HG_GROUP * HG_CHUNK)
    tril = ((r[:, None] >= r[None, :]) & (r[:, None] // HG_CHUNK == r[None, :] // HG_CHUNK)).astype(BF16)
    key_offset = jnp.arange(HG_SUB * dk) // dk
    summer = (key_offset[:, None] == jnp.arange(dk)[None, :] % HG_SUB).astype(BF16)
    return pl.pallas_call(
        functools.partial(_hgrn_kernel, layer=layer),
        grid=(batch, h, nt),
        in_specs=[
            pl.BlockSpec((tt, 4 * dk), lambda b, hh, t: (b * nt + t, hh)),
            pl.BlockSpec((depth, dk), lambda b, hh, t: (0, hh)),
            pl.BlockSpec((1, dv), lambda b, hh, t: (0, hh)),
            pl.BlockSpec((HG_GROUP * HG_CHUNK, HG_GROUP * HG_CHUNK), lambda b, hh, t: (0, 0)),
            pl.BlockSpec((HG_SUB * dk, dk), lambda b, hh, t: (0, 0)),
        ],
        out_specs=pl.BlockSpec((tt, dv), lambda b, hh, t: (b * nt + t, hh)),
        out_shape=jax.ShapeDtypeStruct((batch * seq, h * dv), BF16),
        scratch_shapes=[
            pltpu.VMEM((dk, dv), F32),
            pltpu.VMEM((tt, dk), F32), pltpu.VMEM((tt, dk), F32), pltpu.VMEM((tt, dk), F32),
            pltpu.VMEM((tt, dk), BF16), pltpu.VMEM((tt // HG_CHUNK, dk, dv), F32),
            pltpu.VMEM((tt, HG_SUB * dk), BF16), pltpu.VMEM((tt, dk), F32), pltpu.VMEM((tt, dv), F32),
        ],
        compiler_params=_params("parallel", "parallel", "arbitrary"),
        name="hgrn2",
    )(u, lb_gamma, gnorm_w.reshape(1, h * dv), tril, summer)


QL = NSA_HPG * Q_BLOCK
FLAG_BITS = 16


def _tile_positions(first_tile, n_tiles=1):
    lane = lax.broadcasted_iota(jnp.int32, (1, n_tiles * QL), 1)
    return (first_tile + lane // QL) * Q_BLOCK + (lane & (Q_BLOCK - 1))


CMP_CHUNK = 128
SEL_CLASSES = 8
SEL_TILES = 2
KEY_LANES = 128
SLOPE_PARTS = 3
LOG2E = 1.4426950408889634


def _load_query(qa_ref, q_ref, sa_ref):
    dh = NSA_HEAD_DIM
    n_tiles = q_ref.shape[0] // Q_BLOCK
    qt = (q_ref[...].astype(F32) * (dh ** -0.5 * LOG2E)).T
    qa_ref[0:dh, :] = jnp.concatenate(
        [qt[h * dh:(h + 1) * dh, i * Q_BLOCK:(i + 1) * Q_BLOCK] for i in range(n_tiles) for h in range(NSA_HPG)],
        axis=1).astype(BF16)
    qa_ref[dh:, :] = jnp.concatenate([sa_ref[...]] * n_tiles, axis=1)


def _nsa_select_body(nch, t_min, q_ref, sa_ref, kc_ref, vct_ref, slope_ref, oc_ref, sel_ref, flag_ref,
                     qa_ref, s_ref, pb_ref, p_ref):
    ck = CMP_CHUNK
    nk = nch * ck
    ns = nk * CMP_STRIDE // SLC_BLOCK
    n_t = SEL_TILES
    lanes = n_t * QL
    first_tile = pl.program_id(2) * n_t
    _load_query(qa_ref, q_ref, sa_ref)
    t = _tile_positions(first_tile, n_t)
    slope2 = jnp.concatenate([slope_ref[...] * LOG2E] * n_t, axis=1)

    rows = []
    mc = None
    for ch in range(nch):
        rs = slice(ch * ck, (ch + 1) * ck)
        s = _dot(kc_ref[rs, :], qa_ref[...])
        if ((ch + 1) * ck - 1) * CMP_STRIDE + (CMP_BLOCK - 1) > t_min:
            end = (ch * ck + lax.broadcasted_iota(jnp.int32, (ck, 1), 0)) * CMP_STRIDE + (CMP_BLOCK - 1)
            s = jnp.where(t >= end, s, NEG)
        s_ref[rs, :] = s
        brow = slope2 * (ch * ck * CMP_STRIDE + (CMP_BLOCK - 1) - first_tile * Q_BLOCK).astype(F32)
        rows.append(brow)
        cm = jnp.max(s.reshape(ck // 8, 8, lanes), axis=0) + brow
        mc = cm if mc is None else jnp.maximum(mc, cm)
    m = jnp.max(mc, axis=0, keepdims=True)

    lsum = None
    for ch in range(nch):
        rs = slice(ch * ck, (ch + 1) * ck)
        p = jnp.exp2(s_ref[rs, :] - (m - rows[ch]))
        ls = jnp.sum(p.reshape(ck // 8, 8, lanes), axis=0)
        lsum = ls if lsum is None else lsum + ls
        s_ref[rs, :] = p
        pb_ref[rs, :] = p.astype(BF16)
    l = jnp.sum(lsum, axis=0, keepdims=True)
    inv = jnp.where(t >= CMP_BLOCK - 1, 1.0 / jnp.maximum(l, TINY), 0.0)
    oc = _dot(vct_ref[:, 0:nk], pb_ref[0:nk, :]) * inv
    for i in range(n_t):
        oc_ref[i] = oc[:, i * QL:(i + 1) * QL]

    for i in range(n_t):
        p_ref[i, 0:8, :] = jnp.zeros((8, Q_BLOCK), F32)
        p_ref[i, 8 + nk:16 + nk, :] = jnp.zeros((8, Q_BLOCK), F32)
    for ch in range(nch):
        pn = s_ref[ch * ck:(ch + 1) * ck, :] * inv
        for i in range(n_t):
            acc = pn[:, i * QL:i * QL + Q_BLOCK]
            for h in range(1, NSA_HPG):
                acc = acc + pn[:, i * QL + h * Q_BLOCK:i * QL + (h + 1) * Q_BLOCK]
            p_ref[i, 8 + ch * ck:8 + (ch + 1) * ck, :] = acc

    ratio = SLC_BLOCK // CMP_STRIDE
    imp = []
    for i in range(n_t):
        acc = p_ref[i, pl.ds(7, ns, stride=ratio), :]
        for r in range(1, ratio + 1):
            acc = acc + p_ref[i, pl.ds(7 + r, ns, stride=ratio), :]
        imp.append(acc)
    imp = jnp.concatenate(imp, axis=1)

    j = lax.broadcasted_iota(jnp.int32, (ns, 1), 0)
    cur = jnp.concatenate([t[:, i * QL:i * QL + Q_BLOCK] for i in range(n_t)], axis=1) // SLC_BLOCK
    forced = (j == 0) | (j == cur) | (j == cur - 1)
    imp = jnp.where(forced, NEG, jnp.where(j > cur, -1.0, imp))
    for _ in range(min(SLC_TOPK, ns) - 3):
        mx = jnp.max(imp, axis=0, keepdims=True)
        idx = jnp.min(jnp.where(imp == mx, j, ns), axis=0, keepdims=True)
        imp = jnp.where(j == idx, NEG, imp)
    sel = jnp.where((imp == NEG) & (j <= cur), 1.0, 0.0)
    ns_all, nw, nw_all = sel_ref.shape[1], ns // FLAG_BITS, flag_ref.shape[1]
    bit = jnp.left_shift(1, j & (FLAG_BITS - 1)).astype(F32)
    for i in range(n_t):
        sel_i = sel[:, i * Q_BLOCK:(i + 1) * Q_BLOCK]
        sel_ref[i, 0:ns, :] = sel_i
        if ns < ns_all:
            sel_ref[i, ns:, :] = jnp.zeros((ns_all - ns, Q_BLOCK), F32)
        chosen = jnp.max(sel_i, axis=1, keepdims=True)
        words = jnp.sum((chosen * bit).reshape(nw, FLAG_BITS, 1), axis=1)
        flag_ref[i, 0:nw, :] = jnp.broadcast_to(words, (nw, flag_ref.shape[2]))
        if nw < nw_all:
            flag_ref[i, nw:, :] = jnp.zeros((nw_all - nw, flag_ref.shape[2]), F32)


def _nsa_select_kernel(q_ref, sa_ref, kc_ref, *rest):
    n_chunks = kc_ref.shape[0] // CMP_CHUNK
    n_steps = kc_ref.shape[0] * CMP_STRIDE // (SEL_TILES * Q_BLOCK)
    n_cls = min(SEL_CLASSES, n_chunks)
    cls = pl.program_id(2) * n_cls // n_steps
    for c in range(n_cls):
        @pl.when(cls == c)
        def _(c=c):
            first_step = -(-c * n_steps // n_cls)
            _nsa_select_body((c + 1) * n_chunks // n_cls, first_step * SEL_TILES * Q_BLOCK,
                             q_ref, sa_ref, kc_ref, *rest)


def nsa_select(u, q_col, slope_rows, kc, vct, slopes):
    b, g, nc = kc.shape[:3]
    dh, ql, n_t = NSA_HEAD_DIM, QL, SEL_TILES
    nqb = u.shape[0] // (b * Q_BLOCK)
    n_steps = nqb // n_t
    ns = nc * CMP_STRIDE // SLC_BLOCK
    tiles = lambda *shape: pl.BlockSpec((None, None, n_t) + shape, lambda bi, gi, qi: (bi, gi, qi, 0, 0))
    return pl.pallas_call(
        _nsa_select_kernel,
        grid=(b, g, n_steps),
        in_specs=[
            pl.BlockSpec((n_t * Q_BLOCK, NSA_HPG * dh), lambda bi, gi, qi: (bi * n_steps + qi, q_col + gi)),
            pl.BlockSpec((None, KEY_LANES - dh, ql), lambda bi, gi, qi: (gi, 0, 0)),
            pl.BlockSpec((None, None, nc, KEY_LANES), lambda bi, gi, qi: (bi, gi, 0, 0)),
            pl.BlockSpec((None, None, dh, nc), lambda bi, gi, qi: (bi, gi, 0, 0)),
            pl.BlockSpec((None, 1, ql), lambda bi, gi, qi: (gi, 0, 0)),
        ],
        out_specs=[tiles(dh, ql), tiles(ns, Q_BLOCK), tiles(ns // FLAG_BITS, 128)],
        out_shape=[
            jax.ShapeDtypeStruct((b, g, nqb, dh, ql), F32),
            jax.ShapeDtypeStruct((b, g, nqb, ns, Q_BLOCK), F32),
            jax.ShapeDtypeStruct((b, g, nqb, ns // FLAG_BITS, 128), F32),
        ],
        scratch_shapes=[
            pltpu.VMEM((KEY_LANES, n_t * ql), BF16), pltpu.VMEM((nc, n_t * ql), F32),
            pltpu.VMEM((nc, n_t * ql), BF16), pltpu.VMEM((n_t, nc + 16, Q_BLOCK), F32),
        ],
        compiler_params=_params("parallel", "parallel", "parallel"),
        name="nsa_select",
    )(u, slope_rows, kc, vct, slopes)


ATT_GROUP = 8
LIST_PAD = 3 * ATT_GROUP
DIAG_BLOCKS = Q_BLOCK // SLC_BLOCK
WIN_BLOCKS = WINDOW // SLC_BLOCK


def _score_blocks(k_src, blocks, qa_ref, kcat_ref, s_ref):
    kb = SLC_BLOCK
    for b, j in enumerate(blocks):
        kcat_ref[b * kb:(b + 1) * kb, :] = k_src[jnp.maximum(j, 0)]
    n = len(blocks) * kb
    s_ref[0:n, :] = _dot(kcat_ref[0:n, :], qa_ref[...])


def _softmax_step(v_src, blocks, masks, sel_rows, t, slope2, blk0, s_ref, vcat_ref, p_ref, m_ref, l_ref, acc_ref):
    nb = len(blocks)
    kb = SLC_BLOCK
    for b, j in enumerate(blocks):
        vcat_ref[b * kb:(b + 1) * kb, :] = v_src[jnp.maximum(j, 0)]
    rows = []
    mc = None
    for b, j in enumerate(blocks):
        brow = slope2 * ((j - blk0) * kb).astype(F32)
        if sel_rows[b] is not None:
            brow = jnp.where(sel_rows[b] > 0.0, brow, NEG)
        brow = jnp.where(j >= 0, brow, NEG)
        rows.append(brow)
        s = s_ref[b * kb:(b + 1) * kb, :]
        if masks[b] is not None:
            kpos = j * kb + lax.broadcasted_iota(jnp.int32, (kb, 1), 0)
            valid = (t >= kpos) if masks[b] == "causal" else (t - kpos < WINDOW)
            s = jnp.where(valid, s, NEG)
            s_ref[b * kb:(b + 1) * kb, :] = s
        cm = jnp.max(s.reshape(kb // 8, 8, QL), axis=0) + brow
        mc = cm if mc is None else jnp.maximum(mc, cm)
    m_old = m_ref[...]
    m_new = jnp.maximum(m_old, jnp.max(mc, axis=0, keepdims=True))
    alpha = jnp.exp2(m_old - m_new)
    m_ref[...] = m_new

    lsum = None
    for b in range(nb):
        p = jnp.exp2(s_ref[b * kb:(b + 1) * kb, :] - (m_new - rows[b]))
        ls = jnp.sum(p.reshape(kb // 8, 8, QL), axis=0)
        lsum = ls if lsum is None else lsum + ls
        p_ref[b * kb:(b + 1) * kb, :] = p.astype(BF16)
    l_ref[...] = alpha * l_ref[...] + lsum
    acc_ref[...] = alpha * acc_ref[...] + _dot_tn(vcat_ref[0:nb * kb, :], p_ref[0:nb * kb, :])


def _nsa_attend_kernel(fw_ref, q_ref, sa_ref, ks_ref, vs_ref, kw_ref, vw_ref, sel_ref, oc_ref, gate_ref, slope_ref,
                       o_ref, qa_ref, kcat_ref, s_ref, vcat_ref, p_ref, m_ref, l_ref, acc_ref,
                       kcat_w_ref, s_w_ref, vcat_w_ref, p_w_ref, m_w_ref, l_w_ref, acc_w_ref, list_ref, *,
                       words_per_tile):
    bi, gi, qb = pl.program_id(0), pl.program_id(1), pl.program_id(2)
    tile_id = (bi * pl.num_programs(1) + gi) * pl.num_programs(2) + qb
    _load_query(qa_ref, q_ref, sa_ref)
    t = _tile_positions(qb)
    slope2 = slope_ref[...] * LOG2E
    blk0 = qb * DIAG_BLOCKS
    diag = [blk0 + i for i in range(DIAG_BLOCKS)]
    past = [blk0 - WIN_BLOCKS + i for i in range(WIN_BLOCKS)]

    def sel_row(j):
        row = sel_ref[pl.ds(jnp.maximum(j, 0), 1), :]
        return jnp.concatenate([row] * NSA_HPG, axis=1)

    def reset(m, l, acc):
        m[...] = jnp.full_like(m, NEG)
        l[...] = jnp.zeros_like(l)
        acc[...] = jnp.zeros_like(acc)

    def result(l, acc):
        return acc[...] * (1.0 / jnp.maximum(jnp.sum(l[...], axis=0, keepdims=True), TINY))

    def scan_flags(w, n):
        base = w * FLAG_BITS
        word = fw_ref[tile_id * words_per_tile + w] & ((1 << jnp.minimum(blk0 - base, FLAG_BITS)) - 1)
        for i in range(FLAG_BITS):
            list_ref[n] = base + i
            n = n + ((word >> i) & 1)
        return n

    n_sel = lax.fori_loop(0, (blk0 + FLAG_BITS - 1) // FLAG_BITS, scan_flags, 0)
    for i in range(LIST_PAD):
        list_ref[n_sel + i] = -1

    def listed(i):
        return [list_ref[i * ATT_GROUP + b] for b in range(ATT_GROUP)]

    _score_blocks(kw_ref, past + diag, qa_ref, kcat_w_ref, s_w_ref)
    _score_blocks(ks_ref, diag, qa_ref, kcat_ref.at[0], s_ref.at[0])
    _score_blocks(ks_ref, listed(0), qa_ref, kcat_ref.at[1], s_ref.at[1])

    slc_state = (m_ref, l_ref, acc_ref)
    reset(*slc_state)
    _softmax_step(vs_ref, diag, ["causal"] * DIAG_BLOCKS, [sel_row(j) for j in diag], t, slope2, blk0,
                  s_ref.at[0], vcat_ref, p_ref, *slc_state)

    win_state = (m_w_ref, l_w_ref, acc_w_ref)
    reset(*win_state)
    masks = ["window"] * DIAG_BLOCKS + [None] * (WIN_BLOCKS - DIAG_BLOCKS) + ["causal"] * DIAG_BLOCKS
    _softmax_step(vw_ref, past + diag, masks, [None] * len(masks), t, slope2, blk0,
                  s_w_ref, vcat_w_ref, p_w_ref, *win_state)

    n_steps = (n_sel + ATT_GROUP - 1) // ATT_GROUP

    def slc_step(i, slot):
        _score_blocks(ks_ref, listed(i + 1), qa_ref, kcat_ref.at[slot], s_ref.at[slot])
        blocks = listed(i)
        _softmax_step(vs_ref, blocks, [None] * ATT_GROUP, [sel_row(j) for j in blocks], t, slope2, blk0,
                      s_ref.at[1 - slot], vcat_ref, p_ref, *slc_state)

    def slc_body(i2, carry):
        slc_step(2 * i2, 0)

        @pl.when(2 * i2 + 1 < n_steps)
        def _():
            slc_step(2 * i2 + 1, 1)

        return carry

    lax.fori_loop(0, (n_steps + 1) // 2, slc_body, 0)
    o_slc = result(l_ref, acc_ref)
    o_win = result(l_w_ref, acc_w_ref)

    gate = _sigmoid(gate_ref[...])
    o = gate[0:1, :] * oc_ref[...] + gate[1:2, :] * o_slc + gate[2:3, :] * o_win
    o = jnp.concatenate([o[:, h * Q_BLOCK:(h + 1) * Q_BLOCK] for h in range(NSA_HPG)], axis=0)
    o_ref[...] = o.T.astype(o_ref.dtype)


def nsa_attend(flag_words, u, q_col, slope_rows, ks, vs, kw, vw, sel, oc, gates, slopes):
    b, g, ns = ks.shape[:3]
    dh, ql = NSA_HEAD_DIM, QL
    nqb = u.shape[0] // (b * Q_BLOCK)
    gkeys = ATT_GROUP * SLC_BLOCK
    wkeys = (WIN_BLOCKS + DIAG_BLOCKS) * SLC_BLOCK
    tile = lambda *shape: pl.BlockSpec((None, None, None) + shape, lambda bi, gi, qi, fw: (bi, gi, qi, 0, 0))
    seq = lambda *shape: pl.BlockSpec((None, None) + shape, lambda bi, gi, qi, fw: (bi, gi, 0, 0, 0))
    state = [pltpu.VMEM((1, ql), F32), pltpu.VMEM((8, ql), F32), pltpu.VMEM((dh, ql), F32)]
    grid_spec = pltpu.PrefetchScalarGridSpec(
        num_scalar_prefetch=1,
        grid=(b, g, nqb),
        in_specs=[
            pl.BlockSpec((Q_BLOCK, NSA_HPG * dh), lambda bi, gi, qi, fw: (bi * nqb + qi, q_col + gi)),
            pl.BlockSpec((None, KEY_LANES - dh, ql), lambda bi, gi, qi, fw: (gi, 0, 0)),
            seq(ns, SLC_BLOCK, KEY_LANES), seq(ns, SLC_BLOCK, dh),
            seq(ns, SLC_BLOCK, KEY_LANES), seq(ns, SLC_BLOCK, dh),
            tile(ns, Q_BLOCK), tile(dh, ql), tile(3, ql),
            pl.BlockSpec((None, 1, ql), lambda bi, gi, qi, fw: (gi, 0, 0)),
        ],
        out_specs=pl.BlockSpec((Q_BLOCK, NSA_HPG * dh), lambda bi, gi, qi, fw: (bi * nqb + qi, gi)),
        scratch_shapes=[
            pltpu.VMEM((KEY_LANES, ql), BF16),
            pltpu.VMEM((2, gkeys, KEY_LANES), BF16), pltpu.VMEM((2, gkeys, ql), F32),
            pltpu.VMEM((gkeys, dh), BF16), pltpu.VMEM((gkeys, ql), BF16), *state,
            pltpu.VMEM((wkeys, KEY_LANES), BF16), pltpu.VMEM((wkeys, ql), F32),
            pltpu.VMEM((wkeys, dh), BF16), pltpu.VMEM((wkeys, ql), BF16), *state,
            pltpu.SMEM((ns + LIST_PAD,), jnp.int32),
        ],
    )
    return pl.pallas_call(
        functools.partial(_nsa_attend_kernel, words_per_tile=ns // FLAG_BITS),
        grid_spec=grid_spec,
        out_shape=jax.ShapeDtypeStruct((b * nqb * Q_BLOCK, g * NSA_HPG * dh), BF16),
        compiler_params=_params("parallel", "parallel", "arbitrary"),
        name="nsa_attend",
    )(flag_words, u, slope_rows, ks, vs, kw, vw, sel, oc, gates, slopes)


def nsa_mixer(u, q_col, kc, vc, ks, vs, kw, vw, gate_logits, cmp_pos, cmp_w1, cmp_b1, cmp_w2, cmp_b2, *,
              batch, seq):
    g, hpg, dh = NSA_KV_GROUPS, NSA_HPG, NSA_HEAD_DIM
    nqb, ns, nc = seq // Q_BLOCK, seq // SLC_BLOCK, seq // CMP_STRIDE

    gates = gate_logits.astype(F32).reshape(batch, nqb, Q_BLOCK, 3, g, hpg).transpose(0, 4, 1, 3, 5, 2)
    gates = gates.reshape(batch, g, nqb, 3, QL)
    slopes = 2.0 ** (-8.0 * jnp.arange(1, NSA_HEADS + 1, dtype=F32) / NSA_HEADS)
    slopes = jnp.repeat(slopes.reshape(g, 1, hpg), Q_BLOCK, axis=2)

    def halves(x):
        x = x.reshape(batch, nc, CMP_STRIDE, g, dh).transpose(0, 3, 1, 2, 4)
        return x.reshape(batch * g, nc, CMP_STRIDE * dh)

    def value_blocks(x):
        return x.reshape(batch, ns, SLC_BLOCK, g, dh).transpose(0, 3, 1, 2, 4)

    def key_blocks(x):
        r = jnp.arange(SLC_BLOCK, dtype=F32).astype(BF16).reshape(1, 1, 1, SLC_BLOCK, 1)
        r = jnp.broadcast_to(r, (batch, g, ns, SLC_BLOCK, SLOPE_PARTS))
        pad = jnp.zeros((batch, g, ns, SLC_BLOCK, KEY_LANES - dh - SLOPE_PARTS), BF16)
        return jnp.concatenate([value_blocks(x), r, pad], axis=-1)

    s2 = slopes * LOG2E
    parts = []
    for _ in range(SLOPE_PARTS):
        part = s2.astype(BF16)
        parts.append(part)
        s2 = s2 - part.astype(F32)
    slope_rows = jnp.concatenate(parts + [jnp.zeros((g, KEY_LANES - dh - SLOPE_PARTS, QL), BF16)], axis=1)

    cmp = compress_tokens(jnp.stack([halves(kc), halves(vc)]), cmp_pos, cmp_w1, cmp_b1, cmp_w2, cmp_b2)
    cmp = cmp.astype(BF16).reshape(2, batch, g, nc, dh)
    r = (jnp.arange(nc) % CMP_CHUNK * CMP_STRIDE).astype(BF16).reshape(1, 1, nc, 1)
    k_cmp = jnp.concatenate([cmp[0], jnp.broadcast_to(r, (batch, g, nc, SLOPE_PARTS)),
                             jnp.zeros((batch, g, nc, KEY_LANES - dh - SLOPE_PARTS), BF16)], axis=-1)
    v_cmp_t = cmp[1].transpose(0, 1, 3, 2)

    oc, sel, flags = nsa_select(u, q_col, slope_rows, k_cmp, v_cmp_t, slopes)
    flag_words = flags[:, :, :, :, 0].astype(jnp.int32).reshape(-1)
    return nsa_attend(flag_words, u, q_col, slope_rows, key_blocks(ks), value_blocks(vs), key_blocks(kw),
                      value_blocks(vw), sel, oc, gates, slopes)


ROW_TILE = 1024
COL_TILE = 1024
OUT_ROW_TILE = 512
FFN_ROW_TILE = 1024
FFN_HID_TILE = 512
CONV_SEQ_TILE = 512
HGRN_SEQ_TILE = 1024


def _even_layer(x, nw, w_in, conv_w, conv_b, ln_w, ln_b, cmp_pos, cmp_w1, cmp_b1, cmp_w2, cmp_b2, w_out, *,
                batch, seq):
    c = conv_w.shape[-1]
    nq = NSA_HEADS * NSA_HEAD_DIM
    nkv = NSA_KV_GROUPS * NSA_HEAD_DIM
    n_in = w_in.shape[1]
    n_pad = -(-n_in // COL_TILE) * COL_TILE
    w_in = jnp.pad(w_in, ((0, 0), (0, n_pad - n_in))).astype(BF16)
    u = norm_matmul(x, nw, w_in, tm=ROW_TILE, tn=COL_TILE)
    a_out = conformer_conv(u, conv_w.reshape(CONV_WIDTH, c), conv_b, ln_w, ln_b, seq=seq, ts=CONV_SEQ_TILE)
    off = 2 * c
    q_col = off // (NSA_HPG * NSA_HEAD_DIM)
    off += nq
    kvs = [u[:, off + i * nkv:off + (i + 1) * nkv] for i in range(6)]
    off += 6 * nkv
    gate_logits = u[:, off:off + 3 * NSA_HEADS]
    b_out = nsa_mixer(u, q_col, *kvs, gate_logits, cmp_pos, cmp_w1, cmp_b1, cmp_w2, cmp_b2,
                      batch=batch, seq=seq)
    return matmul_residual(a_out, b_out, 0, 0, w_out.astype(BF16), x, tm=OUT_ROW_TILE, tn=w_out.shape[1])


def _odd_layer(x, nw, w_in, lb_gamma, gnorm_w, w_out, *, layer, batch, seq):
    d = w_in.shape[0]
    w_in = w_in.reshape(d, 4, HG_HEADS, HG_DK).transpose(0, 2, 1, 3).reshape(d, 4 * HG_HEADS * HG_DK)
    u = norm_matmul(x, nw, w_in.astype(BF16), tm=ROW_TILE, tn=COL_TILE)
    o = hgrn2(u, lb_gamma, gnorm_w, layer=layer, batch=batch, seq=seq, tt=HGRN_SEQ_TILE)
    return matmul_residual(o, o, 0, 1, w_out.astype(BF16), x, tm=OUT_ROW_TILE, tn=w_out.shape[1])


def kernel(x, norm_w, final_norm_w, ev_w_in, ev_conv_w, ev_conv_b, ev_conv_ln_w, ev_conv_ln_b, ev_cmp_pos,
           ev_cmp_w1, ev_cmp_b1, ev_cmp_w2, ev_cmp_b2, ev_w_out, od_w_in, od_lb_gamma, od_gnorm_w, od_w_out,
           ffn_w_gu, ffn_w_down):
    batch, seq, d = x.shape
    depth = norm_w.shape[0]
    xs = x.reshape(batch * seq, d)
    for layer in range(depth):
        i = layer // 2
        if layer % 2 == 0:
            xs = _even_layer(xs, norm_w[layer, 0], ev_w_in[i], ev_conv_w[i], ev_conv_b[i], ev_conv_ln_w[i],
                             ev_conv_ln_b[i], ev_cmp_pos[i], ev_cmp_w1[i], ev_cmp_b1[i], ev_cmp_w2[i],
                             ev_cmp_b2[i], ev_w_out[i], batch=batch, seq=seq)
        else:
            xs = _odd_layer(xs, norm_w[layer, 0], od_w_in[i], od_lb_gamma.astype(F32), od_gnorm_w[i],
                            od_w_out[i], layer=layer, batch=batch, seq=seq)
        xs = ffn_block(xs, norm_w[layer, 1], ffn_w_gu[layer].astype(BF16), ffn_w_down[layer].astype(BF16),
                       final_norm_w, tm=FFN_ROW_TILE, th=FFN_HID_TILE, final_norm=layer == depth - 1)
    return xs.reshape(batch, seq, d)
```

```python
import functools

import jax
import jax.numpy as jnp
import numpy as np
from jax import lax
from jax.experimental import pallas as pl
from jax.experimental.pallas import tpu as pltpu

F32 = jnp.float32
BF16 = jnp.bfloat16

EPS = 1e-6
TINY = 1e-30
NEG = -1e30

VMEM_LIMIT_BYTES = 56 * 1024 * 1024

CONV_WIDTH = 31
NSA_HEADS = 16
NSA_HEAD_DIM = 64
NSA_KV_GROUPS = 4
NSA_HPG = NSA_HEADS // NSA_KV_GROUPS
CMP_STRIDE = 16
CMP_BLOCK = 32
SLC_BLOCK = 64
SLC_TOPK = 16
WINDOW = 512
Q_BLOCK = 128
HG_HEADS = 16
HG_DK = 128
HG_DV = 128
HG_CHUNK = 64
HG_SUB = 16
HG_GROUP = 4


def _params(*sem):
    return pltpu.CompilerParams(dimension_semantics=sem, vmem_limit_bytes=VMEM_LIMIT_BYTES)


def _sigmoid(x):
    return 1.0 / (1.0 + jnp.exp(-x))


def _silu(x):
    return x * _sigmoid(x)


def _dot(a, b):
    return jnp.dot(a, b, preferred_element_type=F32)


def _dot_nt(a, b):
    return lax.dot_general(a, b, (((1,), (1,)), ((), ())), preferred_element_type=F32)


def _dot_tn(a, b):
    return lax.dot_general(a, b, (((0,), (0,)), ((), ())), preferred_element_type=F32)


def _norm_matmul_kernel(x_ref, nw_ref, w_ref, o_ref, h_ref):
    @pl.when(pl.program_id(1) == 0)
    def _():
        x = x_ref[...]
        ms = jnp.mean(x * x, axis=-1, keepdims=True)
        h_ref[...] = (x * lax.rsqrt(ms + EPS) * nw_ref[...]).astype(BF16)

    o_ref[...] = _dot(h_ref[...], w_ref[...]).astype(o_ref.dtype)


def norm_matmul(x, nw, w, *, tm, tn, out_dtype=BF16):
    m, k = x.shape
    n = w.shape[1]
    return pl.pallas_call(
        _norm_matmul_kernel,
        grid=(m // tm, n // tn),
        in_specs=[
            pl.BlockSpec((tm, k), lambda i, j: (i, 0)),
            pl.BlockSpec((1, k), lambda i, j: (0, 0)),
            pl.BlockSpec((k, tn), lambda i, j: (0, j)),
        ],
        out_specs=pl.BlockSpec((tm, tn), lambda i, j: (i, j)),
        out_shape=jax.ShapeDtypeStruct((m, n), out_dtype),
        scratch_shapes=[pltpu.VMEM((tm, k), BF16)],
        compiler_params=_params("parallel", "arbitrary"),
        name="norm_matmul",
    )(x, nw.reshape(1, k), w)


def _matmul_res_kernel(a1_ref, a2_ref, w1_ref, w2_ref, r_ref, o_ref):
    acc = _dot(a1_ref[...], w1_ref[...]) + _dot(a2_ref[...], w2_ref[...])
    o_ref[...] = r_ref[...] + acc


def matmul_residual(a1, a2, blk1, blk2, w, res, *, tm, tn):
    m = res.shape[0]
    k, n = w.shape
    kh = k // 2
    return pl.pallas_call(
        _matmul_res_kernel,
        grid=(m // tm, n // tn),
        in_specs=[
            pl.BlockSpec((tm, kh), lambda i, j: (i, blk1)),
            pl.BlockSpec((tm, kh), lambda i, j: (i, blk2)),
            pl.BlockSpec((kh, tn), lambda i, j: (0, j)),
            pl.BlockSpec((kh, tn), lambda i, j: (1, j)),
            pl.BlockSpec((tm, tn), lambda i, j: (i, j)),
        ],
        out_specs=pl.BlockSpec((tm, tn), lambda i, j: (i, j)),
        out_shape=jax.ShapeDtypeStruct((m, n), F32),
        compiler_params=_params("parallel", "arbitrary"),
        name="matmul_residual",
    )(a1, a2, w, w, res)


def _ffn_kernel(x_ref, nw_ref, wg_ref, wu_ref, wd_ref, fw_ref, o_ref, h_ref, *, final_norm):
    j = pl.program_id(1)

    @pl.when(j == 0)
    def _():
        x = x_ref[...]
        ms = jnp.mean(x * x, axis=-1, keepdims=True)
        h_ref[...] = (x * lax.rsqrt(ms + EPS) * nw_ref[...]).astype(BF16)
        o_ref[...] = x

    h = h_ref[...]
    a = _dot(h, wg_ref[...])
    b = _dot(h, wu_ref[...])
    z = (_silu(a) * b).astype(BF16)
    o_ref[...] += _dot(z, wd_ref[...])

    if final_norm:
        @pl.when(j == pl.num_programs(1) - 1)
        def _():
            y = o_ref[...]
            ms = jnp.mean(y * y, axis=-1, keepdims=True)
            o_ref[...] = y * lax.rsqrt(ms + EPS) * fw_ref[...]


def ffn_block(x, nw, w_gu, w_down, final_w, *, tm, th, final_norm):
    m, d = x.shape
    hid = w_down.shape[0]
    nh = hid // th
    return pl.pallas_call(
        functools.partial(_ffn_kernel, final_norm=final_norm),
        grid=(m // tm, nh),
        in_specs=[
            pl.BlockSpec((tm, d), lambda i, j: (i, 0)),
            pl.BlockSpec((1, d), lambda i, j: (0, 0)),
            pl.BlockSpec((d, th), lambda i, j: (0, j)),
            pl.BlockSpec((d, th), lambda i, j: (0, j + nh)),
            pl.BlockSpec((th, d), lambda i, j: (j, 0)),
            pl.BlockSpec((1, d), lambda i, j: (0, 0)),
        ],
        out_specs=pl.BlockSpec((tm, d), lambda i, j: (i, 0)),
        out_shape=jax.ShapeDtypeStruct((m, d), F32),
        scratch_shapes=[pltpu.VMEM((tm, d), BF16)],
        compiler_params=_params("parallel", "arbitrary"),
        name="ffn_block",
    )(x, nw.reshape(1, d), w_gu, w_gu, w_down, final_w.reshape(1, d))


CONV_HALO = 32
CONV_ROWS = 32


def _conv_kernel(a_ref, g_ref, ah_ref, gh_ref, cw_ref, cb_ref, lw_ref, lb_ref, o_ref, hs_ref, sh_ref, cv_ref, *,
                 tiles_per_seq):
    ts = a_ref.shape[0]
    first = (pl.program_id(0) % tiles_per_seq) == 0
    hprev = ah_ref[...].astype(F32) * _sigmoid(gh_ref[...].astype(F32))
    hs_ref[0:CONV_HALO, :] = jnp.where(first, 0.0, hprev)
    hs_ref[CONV_HALO:CONV_HALO + ts, :] = a_ref[...].astype(F32) * _sigmoid(g_ref[...].astype(F32))
    off = CONV_HALO - (CONV_WIDTH - 1)
    n_shifted = sh_ref.shape[1]
    for r in range(1, 8):
        sh_ref[r - 1] = hs_ref[r:r + n_shifted, :]

    def body(i, carry):
        base = pl.multiple_of(i * CONV_ROWS, CONV_ROWS)
        acc = jnp.zeros((CONV_ROWS // 8, 8, hs_ref.shape[1]), F32) + cb_ref[...]
        for w in range(CONV_WIDTH):
            r, a = (off + w) % 8, (off + w) // 8 * 8
            src = hs_ref if r == 0 else sh_ref.at[r - 1]
            rows = src[pl.ds(base + a, CONV_ROWS), :].reshape(CONV_ROWS // 8, 8, -1)
            acc = acc + rows * cw_ref[w * 8:(w + 1) * 8, :]
        cv_ref[pl.ds(base, CONV_ROWS), :] = acc.reshape(CONV_ROWS, -1)
        return carry

    lax.fori_loop(0, ts // CONV_ROWS, body, 0)

    y = cv_ref[...]
    mu = jnp.mean(y, axis=-1, keepdims=True)
    d = y - mu
    var = jnp.mean(d * d, axis=-1, keepdims=True)
    hn = d * lax.rsqrt(var + EPS) * lw_ref[...] + lb_ref[...]
    o_ref[...] = _silu(hn).astype(o_ref.dtype)


def conformer_conv(u, conv_w, conv_b, ln_w, ln_b, *, seq, ts):
    t = u.shape[0]
    c = conv_w.shape[1]
    hb = ts // CONV_HALO
    return pl.pallas_call(
        functools.partial(_conv_kernel, tiles_per_seq=seq // ts),
        grid=(t // ts,),
        in_specs=[
            pl.BlockSpec((ts, c), lambda i: (i, 0)),
            pl.BlockSpec((ts, c), lambda i: (i, 1)),
            pl.BlockSpec((CONV_HALO, c), lambda i: (jnp.maximum(i * hb - 1, 0), 0)),
            pl.BlockSpec((CONV_HALO, c), lambda i: (jnp.maximum(i * hb - 1, 0), 1)),
            pl.BlockSpec((CONV_WIDTH * 8, c), lambda i: (0, 0)),
            pl.BlockSpec((1, c), lambda i: (0, 0)),
            pl.BlockSpec((1, c), lambda i: (0, 0)),
            pl.BlockSpec((1, c), lambda i: (0, 0)),
        ],
        out_specs=pl.BlockSpec((ts, c), lambda i: (i, 0)),
        out_shape=jax.ShapeDtypeStruct((t, c), BF16),
        scratch_shapes=[pltpu.VMEM((CONV_HALO + ts, c), F32), pltpu.VMEM((7, CONV_HALO + ts - 8, c), F32),
                        pltpu.VMEM((ts, c), F32)],
        compiler_params=_params("parallel"),
        name="conformer_conv",
    )(u, u, u, u, jnp.repeat(conv_w, 8, axis=0), conv_b.reshape(1, c), ln_w.reshape(1, c), ln_b.reshape(1, c))


def _compress_kernel(x_ref, pos_ref, w1_ref, b1_ref, w2_ref, b2_ref, o_ref, sh_ref):
    n = x_ref.shape[0]
    hw = x_ref.shape[1]
    x = x_ref[...].astype(F32)
    xa = (x + pos_ref[:, 0:hw]).astype(BF16)
    xb = (x + pos_ref[:, hw:2 * hw]).astype(BF16)
    p1 = _dot(xa, w1_ref[0:hw, :])
    sh_ref[0:n, :] = _dot(xb, w1_ref[hw:2 * hw, :])
    sh_ref[n:n + 8, :] = jnp.zeros((8, sh_ref.shape[1]), F32)
    hid = p1 + sh_ref[1:n + 1, :] + b1_ref[...]
    o_ref[...] = _dot(_silu(hid).astype(BF16), w2_ref[...]) + b2_ref[...]


def compress_tokens(xh, pos, w1, b1, w2, b2):
    _, bg, n, hw = xh.shape
    hid = w1.shape[-1]
    dh = w2.shape[-1]
    return pl.pallas_call(
        _compress_kernel,
        grid=(2, bg),
        in_specs=[
            pl.BlockSpec((None, None, n, hw), lambda s, i: (s, i, 0, 0)),
            pl.BlockSpec((None, 1, 2 * hw), lambda s, i: (s, 0, 0)),
            pl.BlockSpec((None, 2 * hw, hid), lambda s, i: (s, 0, 0)),
            pl.BlockSpec((None, 1, hid), lambda s, i: (s, 0, 0)),
            pl.BlockSpec((None, hid, dh), lambda s, i: (s, 0, 0)),
            pl.BlockSpec((None, 1, dh), lambda s, i: (s, 0, 0)),
        ],
        out_specs=pl.BlockSpec((None, None, n, dh), lambda s, i: (s, i, 0, 0)),
        out_shape=jax.ShapeDtypeStruct((2, bg, n, dh), F32),
        scratch_shapes=[pltpu.VMEM((n + 8, hid), F32)],
        compiler_params=_params("parallel", "parallel"),
        name="compress_tokens",
    )(xh, pos.reshape(2, 1, 2 * hw), w1.astype(BF16), b1.reshape(2, 1, hid), w2.astype(BF16),
      b2.reshape(2, 1, dh))


def _split_bf16(x):
    hi = x.astype(BF16)
    lo = (x - hi.astype(F32)).astype(BF16)
    return hi, lo


def _hgrn_kernel(x_ref, gamma_ref, gn_ref, tril_ref, sum_ref, o_ref,
                 st_ref, gc_ref, k_ref, hk_ref, qg_ref, u_ref, dcat_ref, ps_ref, opart_ref, *, layer):
    c, sub = HG_CHUNK, HG_SUB
    nsub = c // sub
    tt, dk = o_ref.shape
    n_chunks = tt // c
    q_ref, f_ref, i_ref, g_ref = (x_ref.at[:, i * dk:(i + 1) * dk] for i in range(4))

    @pl.when(pl.program_id(2) == 0)
    def _():
        st_ref[...] = jnp.zeros_like(st_ref)

    gamma = gamma_ref[...]
    e = jnp.exp(gamma - jnp.max(gamma, axis=0, keepdims=True))
    sm = e / jnp.sum(e, axis=0, keepdims=True)
    lb = jnp.sum(sm[0:layer + 1, :], axis=0, keepdims=True) - sm[0:1, :]

    def gates(rows):
        f = lb + (1.0 - lb) * _sigmoid(f_ref[rows, :].astype(F32))
        k = 1.0 - f
        hi, lo = _split_bf16(jnp.log2(jnp.maximum(f, TINY)))
        gc = _dot(tril_ref[...], hi) + _dot(tril_ref[...], lo)
        gc_ref[rows, :] = gc
        k_ref[rows, :] = k
        hk_ref[rows, :] = gc - jnp.log2(k)
        qg_ref[rows, :] = (q_ref[rows, :].astype(F32) * jnp.exp2(gc)).astype(BF16)

    r8 = lax.broadcasted_iota(jnp.int32, (1, 8, 1), 1)
    srow = lax.broadcasted_iota(jnp.int32, (c, 1), 0)
    tcol = lax.broadcasted_iota(jnp.int32, (c, c), 1)

    def half_major(x):
        return [jnp.concatenate([x[(2 * i + h) * 8:(2 * i + h + 1) * 8, :] for i in range(nsub)], axis=0)
                .reshape(nsub, 8, -1) for h in range(2)]

    def state_free_part(ci):
        rows = slice(ci * c, (ci + 1) * c)
        q = q_ref[rows, :].astype(F32)
        gc = gc_ref[rows, :]
        k = k_ref[rows, :]
        ivb = i_ref[rows, :]
        glast = gc[c - 1:c, :]
        u_ref[ci] = _dot_tn((k * jnp.exp2(glast - gc)).astype(BF16), ivb)

        qhat = [jnp.zeros((sub, dk), F32)]
        khat = []
        for si in range(1, nsub):
            lo_r, hi_r = si * sub, (si + 1) * sub
            ref = gc[lo_r - 1:lo_r, :]
            qhat.append(q[lo_r:hi_r, :] * jnp.exp2(gc[lo_r:hi_r, :] - ref))
            khat.append(k * jnp.exp2(jnp.where(srow < lo_r, ref - gc, NEG)))
        at = _dot_nt(jnp.concatenate(khat, axis=0).astype(BF16), jnp.concatenate(qhat, axis=0).astype(BF16))
        at_off = jnp.zeros((c, c), F32)
        for si in range(1, nsub):
            at_off = jnp.where(tcol // sub == si, at[(si - 1) * c:si * c, :], at_off)
        opart_ref[rows, :] = _dot_tn(at_off.astype(BF16), ivb)

        lo_rows = slice(ci * c, ci * c + c // 2)
        hi_rows = slice(ci * c + c // 2, (ci + 1) * c)
        q0, q1 = half_major(q)
        g0, g1 = half_major(gc)
        h0, h1 = half_major(hk_ref[rows, :])
        for s in range(sub // 2):
            key = h0[:, s:s + 1, :]
            d0 = q0 * jnp.exp2(jnp.where(r8 >= s, g0 - key, NEG))
            d1 = q1 * jnp.exp2(g1 - key)
            dcat_ref[lo_rows, s * dk:(s + 1) * dk] = d0.reshape(c // 2, dk).astype(BF16)
            dcat_ref[hi_rows, s * dk:(s + 1) * dk] = d1.reshape(c // 2, dk).astype(BF16)
        dcat_ref[lo_rows, (sub // 2) * dk:] = jnp.zeros((c // 2, (sub // 2) * dk), BF16)
        for s in range(sub // 2):
            key = h1[:, s:s + 1, :]
            d1 = q1 * jnp.exp2(jnp.where(r8 >= s, g1 - key, NEG))
            col = (sub // 2 + s) * dk
            dcat_ref[hi_rows, col:col + dk] = d1.reshape(c // 2, dk).astype(BF16)

    group = tril_ref.shape[0]
    for g0 in range(0, tt, group):
        grows = slice(g0, g0 + group)
        gates(grows)
        for ci in range(g0 // c, (g0 + group) // c):
            state_free_part(ci)
        ps_ref[grows, :] = _dot(dcat_ref[grows, :], sum_ref[...])

    glast = [gc_ref[(ci + 1) * c - 1:(ci + 1) * c, :] for ci in range(n_chunks)]
    glast += [jnp.zeros((1, dk), F32)] * (-n_chunks % 8)
    decay_cols = jnp.exp2(jnp.concatenate(glast, axis=0)).T
    st = st_ref[...]
    for ci in range(n_chunks):
        rows = slice(ci * c, (ci + 1) * c)
        opart_ref[rows, :] += _dot(qg_ref[rows, :], st.astype(BF16))
        st = decay_cols[:, ci:ci + 1] * st + u_ref[ci]
    st_ref[...] = st

    own_block = (lax.broadcasted_iota(jnp.int32, (c, dk), 1) // sub
                 == (lax.broadcasted_iota(jnp.int32, (c, dk), 0) // 8) % nsub)
    for ci in range(n_chunks):
        rows = slice(ci * c, (ci + 1) * c)
        ivb = i_ref[rows, :]
        a = jnp.where(own_block, ps_ref[rows, :], 0.0)
        od = _dot(a.astype(BF16), jnp.concatenate([ivb, jnp.zeros((dk - c, ivb.shape[1]), BF16)], axis=0))
        o = opart_ref[rows, :] + jnp.concatenate([od[(h * nsub + i) * 8:(h * nsub + i + 1) * 8, :]
                                                  for i in range(nsub) for h in range(2)], axis=0)
        o = o * lax.rsqrt(jnp.mean(o * o, axis=-1, keepdims=True) + EPS) * gn_ref[...]
        o_ref[rows, :] = (o * _silu(g_ref[rows, :].astype(F32))).astype(o_ref.dtype)


def hgrn2(u, lb_gamma, gnorm_w, *, layer, batch, seq, tt):
    h, dk, dv = HG_HEADS, HG_DK, HG_DV
    nt = seq // tt
    depth = lb_gamma.shape[0]
    r = np.arange(HG_GROUP * HG_CHUNK)
    tril = jnp.asarray((r[:, None] >= r[None, :]) & (r[:, None] // HG_CHUNK == r[None, :] // HG_CHUNK), BF16)
    key_offset = np.arange(HG_SUB * dk) // dk
    summer = jnp.asarray(key_offset[:, None] == np.arange(dk)[None, :] % HG_SUB, BF16)
    return pl.pallas_call(
        functools.partial(_hgrn_kernel, layer=layer),
        grid=(batch, h, nt),
        in_specs=[
            pl.BlockSpec((tt, 4 * dk), lambda b, hh, t: (b * nt + t, hh)),
            pl.BlockSpec((depth, dk), lambda b, hh, t: (0, hh)),
            pl.BlockSpec((1, dv), lambda b, hh, t: (0, hh)),
            pl.BlockSpec((HG_GROUP * HG_CHUNK, HG_GROUP * HG_CHUNK), lambda b, hh, t: (0, 0)),
            pl.BlockSpec((HG_SUB * dk, dk), lambda b, hh, t: (0, 0)),
        ],
        out_specs=pl.BlockSpec((tt, dv), lambda b, hh, t: (b * nt + t, hh)),
        out_shape=jax.ShapeDtypeStruct((batch * seq, h * dv), BF16),
        scratch_shapes=[
            pltpu.VMEM((dk, dv), F32),
            pltpu.VMEM((tt, dk), F32), pltpu.VMEM((tt, dk), F32), pltpu.VMEM((tt, dk), F32),
            pltpu.VMEM((tt, dk), BF16), pltpu.VMEM((tt // HG_CHUNK, dk, dv), F32),
            pltpu.VMEM((tt, HG_SUB * dk), BF16), pltpu.VMEM((tt, dk), F32), pltpu.VMEM((tt, dv), F32),
        ],
        compiler_params=_params("parallel", "parallel", "arbitrary"),
        name="hgrn2",
    )(u, lb_gamma, gnorm_w.reshape(1, h * dv), tril, summer)


QL = NSA_HPG * Q_BLOCK
FLAG_BITS = 16


def _tile_positions(first_tile, n_tiles=1):
    lane = lax.broadcasted_iota(jnp.int32, (1, n_tiles * QL), 1)
    return (first_tile + lane // QL) * Q_BLOCK + (lane & (Q_BLOCK - 1))


CMP_CHUNK = 128
SEL_CLASSES = 8
SEL_TILES = 2
KEY_LANES = 128
SLOPE_PARTS = 3
LOG2E = 1.4426950408889634


def _load_query(qa_ref, q_ref, sa_ref):
    dh = NSA_HEAD_DIM
    n_tiles = q_ref.shape[0] // Q_BLOCK
    qt = (q_ref[...].astype(F32) * (dh ** -0.5 * LOG2E)).T
    qa_ref[0:dh, :] = jnp.concatenate(
        [qt[h * dh:(h + 1) * dh, i * Q_BLOCK:(i + 1) * Q_BLOCK] for i in range(n_tiles) for h in range(NSA_HPG)],
        axis=1).astype(BF16)
    qa_ref[dh:, :] = jnp.concatenate([sa_ref[...]] * n_tiles, axis=1)


def _nsa_select_body(nch, t_min, q_ref, sa_ref, kc_ref, vct_ref, slope_ref, oc_ref, sel_ref, flag_ref,
                     qa_ref, s_ref, pb_ref, p_ref):
    ck = CMP_CHUNK
    nk = nch * ck
    ns = nk * CMP_STRIDE // SLC_BLOCK
    n_t = SEL_TILES
    lanes = n_t * QL
    first_tile = pl.program_id(2) * n_t
    _load_query(qa_ref, q_ref, sa_ref)
    t = _tile_positions(first_tile, n_t)
    slope2 = jnp.concatenate([slope_ref[...] * LOG2E] * n_t, axis=1)

    rows = []
    mc = None
    for ch in range(nch):
        rs = slice(ch * ck, (ch + 1) * ck)
        s = _dot(kc_ref[rs, :], qa_ref[...])
        if ((ch + 1) * ck - 1) * CMP_STRIDE + (CMP_BLOCK - 1) > t_min:
            end = (ch * ck + lax.broadcasted_iota(jnp.int32, (ck, 1), 0)) * CMP_STRIDE + (CMP_BLOCK - 1)
            s = jnp.where(t >= end, s, NEG)
        s_ref[rs, :] = s
        brow = slope2 * (ch * ck * CMP_STRIDE + (CMP_BLOCK - 1) - first_tile * Q_BLOCK).astype(F32)
        rows.append(brow)
        cm = jnp.max(s.reshape(ck // 8, 8, lanes), axis=0) + brow
        mc = cm if mc is None else jnp.maximum(mc, cm)
    m = jnp.max(mc, axis=0, keepdims=True)

    lsum = None
    for ch in range(nch):
        rs = slice(ch * ck, (ch + 1) * ck)
        p = jnp.exp2(s_ref[rs, :] - (m - rows[ch]))
        ls = jnp.sum(p.reshape(ck // 8, 8, lanes), axis=0)
        lsum = ls if lsum is None else lsum + ls
        s_ref[rs, :] = p
        pb_ref[rs, :] = p.astype(BF16)
    l = jnp.sum(lsum, axis=0, keepdims=True)
    inv = jnp.where(t >= CMP_BLOCK - 1, 1.0 / jnp.maximum(l, TINY), 0.0)
    oc = _dot(vct_ref[:, 0:nk], pb_ref[0:nk, :]) * inv
    for i in range(n_t):
        oc_ref[i] = oc[:, i * QL:(i + 1) * QL]

    for i in range(n_t):
        p_ref[i, 0:8, :] = jnp.zeros((8, Q_BLOCK), F32)
        p_ref[i, 8 + nk:16 + nk, :] = jnp.zeros((8, Q_BLOCK), F32)
    for ch in range(nch):
        pn = s_ref[ch * ck:(ch + 1) * ck, :] * inv
        for i in range(n_t):
            acc = pn[:, i * QL:i * QL + Q_BLOCK]
            for h in range(1, NSA_HPG):
                acc = acc + pn[:, i * QL + h * Q_BLOCK:i * QL + (h + 1) * Q_BLOCK]
            p_ref[i, 8 + ch * ck:8 + (ch + 1) * ck, :] = acc

    ratio = SLC_BLOCK // CMP_STRIDE
    imp = []
    for i in range(n_t):
        acc = p_ref[i, pl.ds(7, ns, stride=ratio), :]
        for r in range(1, ratio + 1):
            acc = acc + p_ref[i, pl.ds(7 + r, ns, stride=ratio), :]
        imp.append(acc)
    imp = jnp.concatenate(imp, axis=1)

    j = lax.broadcasted_iota(jnp.int32, (ns, 1), 0)
    cur = jnp.concatenate([t[:, i * QL:i * QL + Q_BLOCK] for i in range(n_t)], axis=1) // SLC_BLOCK
    forced = (j == 0) | (j == cur) | (j == cur - 1)
    imp = jnp.where(forced, NEG, jnp.where(j > cur, -1.0, imp))
    for _ in range(min(SLC_TOPK, ns) - 3):
        mx = jnp.max(imp, axis=0, keepdims=True)
        idx = jnp.min(jnp.where(imp == mx, j, ns), axis=0, keepdims=True)
        imp = jnp.where(j == idx, NEG, imp)
    sel = jnp.where((imp == NEG) & (j <= cur), 1.0, 0.0)
    ns_all, nw, nw_all = sel_ref.shape[1], ns // FLAG_BITS, flag_ref.shape[1]
    bit = jnp.left_shift(1, j & (FLAG_BITS - 1)).astype(F32)
    for i in range(n_t):
        sel_i = sel[:, i * Q_BLOCK:(i + 1) * Q_BLOCK]
        sel_ref[i, 0:ns, :] = sel_i
        if ns < ns_all:
            sel_ref[i, ns:, :] = jnp.zeros((ns_all - ns, Q_BLOCK), F32)
        chosen = jnp.max(sel_i, axis=1, keepdims=True)
        words = jnp.sum((chosen * bit).reshape(nw, FLAG_BITS, 1), axis=1)
        flag_ref[i, 0:nw, :] = jnp.broadcast_to(words, (nw, flag_ref.shape[2]))
        if nw < nw_all:
            flag_ref[i, nw:, :] = jnp.zeros((nw_all - nw, flag_ref.shape[2]), F32)


def _nsa_select_kernel(q_ref, sa_ref, kc_ref, *rest):
    n_chunks = kc_ref.shape[0] // CMP_CHUNK
    n_steps = kc_ref.shape[0] * CMP_STRIDE // (SEL_TILES * Q_BLOCK)
    n_cls = min(SEL_CLASSES, n_chunks)
    cls = pl.program_id(2) * n_cls // n_steps
    for c in range(n_cls):
        @pl.when(cls == c)
        def _(c=c):
            first_step = -(-c * n_steps // n_cls)
            _nsa_select_body((c + 1) * n_chunks // n_cls, first_step * SEL_TILES * Q_BLOCK,
                             q_ref, sa_ref, kc_ref, *rest)


def nsa_select(u, q_col, slope_rows, kc, vct, slopes):
    b, g, nc = kc.shape[:3]
    dh, ql, n_t = NSA_HEAD_DIM, QL, SEL_TILES
    nqb = u.shape[0] // (b * Q_BLOCK)
    n_steps = nqb // n_t
    ns = nc * CMP_STRIDE // SLC_BLOCK
    tiles = lambda *shape: pl.BlockSpec((None, None, n_t) + shape, lambda bi, gi, qi: (bi, gi, qi, 0, 0))
    return pl.pallas_call(
        _nsa_select_kernel,
        grid=(b, g, n_steps),
        in_specs=[
            pl.BlockSpec((n_t * Q_BLOCK, NSA_HPG * dh), lambda bi, gi, qi: (bi * n_steps + qi, q_col + gi)),
            pl.BlockSpec((None, KEY_LANES - dh, ql), lambda bi, gi, qi: (gi, 0, 0)),
            pl.BlockSpec((None, None, nc, KEY_LANES), lambda bi, gi, qi: (bi, gi, 0, 0)),
            pl.BlockSpec((None, None, dh, nc), lambda bi, gi, qi: (bi, gi, 0, 0)),
            pl.BlockSpec((None, 1, ql), lambda bi, gi, qi: (gi, 0, 0)),
        ],
        out_specs=[tiles(dh, ql), tiles(ns, Q_BLOCK), tiles(ns // FLAG_BITS, 128)],
        out_shape=[
            jax.ShapeDtypeStruct((b, g, nqb, dh, ql), F32),
            jax.ShapeDtypeStruct((b, g, nqb, ns, Q_BLOCK), F32),
            jax.ShapeDtypeStruct((b, g, nqb, ns // FLAG_BITS, 128), F32),
        ],
        scratch_shapes=[
            pltpu.VMEM((KEY_LANES, n_t * ql), BF16), pltpu.VMEM((nc, n_t * ql), F32),
            pltpu.VMEM((nc, n_t * ql), BF16), pltpu.VMEM((n_t, nc + 16, Q_BLOCK), F32),
        ],
        compiler_params=_params("parallel", "parallel", "parallel"),
        name="nsa_select",
    )(u, slope_rows, kc, vct, slopes)


ATT_GROUP = 8
LIST_PAD = 3 * ATT_GROUP
DIAG_BLOCKS = Q_BLOCK // SLC_BLOCK
WIN_BLOCKS = WINDOW // SLC_BLOCK


def _score_blocks(k_src, blocks, qa_ref, kcat_ref, s_ref):
    kb = SLC_BLOCK
    for b, j in enumerate(blocks):
        kcat_ref[b * kb:(b + 1) * kb, :] = k_src[jnp.maximum(j, 0)]
    n = len(blocks) * kb
    s_ref[0:n, :] = _dot(kcat_ref[0:n, :], qa_ref[...])


def _softmax_step(v_src, blocks, masks, sel_rows, t, slope2, blk0, s_ref, vcat_ref, p_ref, m_ref, l_ref, acc_ref):
    nb = len(blocks)
    kb = SLC_BLOCK
    for b, j in enumerate(blocks):
        vcat_ref[b * kb:(b + 1) * kb, :] = v_src[jnp.maximum(j, 0)]
    rows = []
    mc = None
    for b, j in enumerate(blocks):
        brow = slope2 * ((j - blk0) * kb).astype(F32)
        if sel_rows[b] is not None:
            brow = jnp.where(sel_rows[b] > 0.0, brow, NEG)
        brow = jnp.where(j >= 0, brow, NEG)
        rows.append(brow)
        s = s_ref[b * kb:(b + 1) * kb, :]
        if masks[b] is not None:
            kpos = j * kb + lax.broadcasted_iota(jnp.int32, (kb, 1), 0)
            valid = (t >= kpos) if masks[b] == "causal" else (t - kpos < WINDOW)
            s = jnp.where(valid, s, NEG)
            s_ref[b * kb:(b + 1) * kb, :] = s
        cm = jnp.max(s.reshape(kb // 8, 8, QL), axis=0) + brow
        mc = cm if mc is None else jnp.maximum(mc, cm)
    m_old = m_ref[...]
    m_new = jnp.maximum(m_old, jnp.max(mc, axis=0, keepdims=True))
    alpha = jnp.exp2(m_old - m_new)
    m_ref[...] = m_new

    lsum = None
    for b in range(nb):
        p = jnp.exp2(s_ref[b * kb:(b + 1) * kb, :] - (m_new - rows[b]))
        ls = jnp.sum(p.reshape(kb // 8, 8, QL), axis=0)
        lsum = ls if lsum is None else lsum + ls
        p_ref[b * kb:(b + 1) * kb, :] = p.astype(BF16)
    l_ref[...] = alpha * l_ref[...] + lsum
    acc_ref[...] = alpha * acc_ref[...] + _dot_tn(vcat_ref[0:nb * kb, :], p_ref[0:nb * kb, :])


def _nsa_attend_kernel(fw_ref, q_ref, sa_ref, ks_ref, vs_ref, kw_ref, vw_ref, sel_ref, oc_ref, gate_ref, slope_ref,
                       o_ref, qa_ref, kcat_ref, s_ref, vcat_ref, p_ref, m_ref, l_ref, acc_ref,
                       kcat_w_ref, s_w_ref, vcat_w_ref, p_w_ref, m_w_ref, l_w_ref, acc_w_ref,
                       kcat_f_ref, s_f_ref, vcat_f_ref, p_f_ref, list_ref, *,
                       words_per_tile):
    bi, gi, qb = pl.program_id(0), pl.program_id(1), pl.program_id(2)
    tile_id = (bi * pl.num_programs(1) + gi) * pl.num_programs(2) + qb
    _load_query(qa_ref, q_ref, sa_ref)
    t = _tile_positions(qb)
    slope2 = slope_ref[...] * LOG2E
    blk0 = qb * DIAG_BLOCKS
    diag = [blk0 + i for i in range(DIAG_BLOCKS)]
    past = [blk0 - WIN_BLOCKS + i for i in range(WIN_BLOCKS)]

    def sel_row(j):
        row = sel_ref[pl.ds(jnp.maximum(j, 0), 1), :]
        return jnp.concatenate([row] * NSA_HPG, axis=1)

    def reset(m, l, acc):
        m[...] = jnp.full_like(m, NEG)
        l[...] = jnp.zeros_like(l)
        acc[...] = jnp.zeros_like(acc)

    def result(l, acc):
        return acc[...] * (1.0 / jnp.maximum(jnp.sum(l[...], axis=0, keepdims=True), TINY))

    def scan_flags(w, n):
        base = w * FLAG_BITS
        word = fw_ref[tile_id * words_per_tile + w] & ((1 << jnp.minimum(blk0 - base, FLAG_BITS)) - 1)
        for i in range(FLAG_BITS):
            list_ref[n] = base + i
            n = n + ((word >> i) & 1)
        return n

    n_sel = lax.fori_loop(0, (blk0 + FLAG_BITS - 1) // FLAG_BITS, scan_flags, 0)
    for i in range(LIST_PAD):
        list_ref[n_sel + i] = -1

    def listed(i):
        return [list_ref[i * ATT_GROUP + b] for b in range(ATT_GROUP)]

    first = diag + listed(0)
    _score_blocks(kw_ref, past + diag, qa_ref, kcat_w_ref, s_w_ref)
    _score_blocks(ks_ref, first, qa_ref, kcat_f_ref, s_f_ref)
    _score_blocks(ks_ref, listed(1), qa_ref, kcat_ref.at[1], s_ref.at[1])

    slc_state = (m_ref, l_ref, acc_ref)
    reset(*slc_state)
    _softmax_step(vs_ref, first, ["causal"] * DIAG_BLOCKS + [None] * ATT_GROUP, [sel_row(j) for j in first],
                  t, slope2, blk0, s_f_ref, vcat_f_ref, p_f_ref, *slc_state)

    win_state = (m_w_ref, l_w_ref, acc_w_ref)
    reset(*win_state)
    masks = ["window"] * DIAG_BLOCKS + [None] * (WIN_BLOCKS - DIAG_BLOCKS) + ["causal"] * DIAG_BLOCKS
    _softmax_step(vw_ref, past + diag, masks, [None] * len(masks), t, slope2, blk0,
                  s_w_ref, vcat_w_ref, p_w_ref, *win_state)

    n_steps = jnp.maximum((n_sel + ATT_GROUP - 1) // ATT_GROUP - 1, 0)

    def slc_step(i, slot):
        _score_blocks(ks_ref, listed(i + 2), qa_ref, kcat_ref.at[slot], s_ref.at[slot])
        blocks = listed(i + 1)
        _softmax_step(vs_ref, blocks, [None] * ATT_GROUP, [sel_row(j) for j in blocks], t, slope2, blk0,
                      s_ref.at[1 - slot], vcat_ref, p_ref, *slc_state)

    def slc_body(i2, carry):
        slc_step(2 * i2, 0)

        @pl.when(2 * i2 + 1 < n_steps)
        def _():
            slc_step(2 * i2 + 1, 1)

        return carry

    lax.fori_loop(0, (n_steps + 1) // 2, slc_body, 0)
    o_slc = result(l_ref, acc_ref)
    o_win = result(l_w_ref, acc_w_ref)

    gate = _sigmoid(gate_ref[...])
    o = gate[0:1, :] * oc_ref[...] + gate[1:2, :] * o_slc + gate[2:3, :] * o_win
    o = jnp.concatenate([o[:, h * Q_BLOCK:(h + 1) * Q_BLOCK] for h in range(NSA_HPG)], axis=0)
    o_ref[...] = o.T.astype(o_ref.dtype)


def nsa_attend(flag_words, u, q_col, slope_rows, ks, vs, kw, vw, sel, oc, gates, slopes):
    b, g, ns = ks.shape[:3]
    dh, ql = NSA_HEAD_DIM, QL
    nqb = u.shape[0] // (b * Q_BLOCK)
    gkeys = ATT_GROUP * SLC_BLOCK
    wkeys = (WIN_BLOCKS + DIAG_BLOCKS) * SLC_BLOCK
    fkeys = (DIAG_BLOCKS + ATT_GROUP) * SLC_BLOCK
    tile = lambda *shape: pl.BlockSpec((None, None, None) + shape, lambda bi, gi, qi, fw: (bi, gi, qi, 0, 0))
    seq = lambda *shape: pl.BlockSpec((None, None) + shape, lambda bi, gi, qi, fw: (bi, gi, 0, 0, 0))
    state = [pltpu.VMEM((1, ql), F32), pltpu.VMEM((8, ql), F32), pltpu.VMEM((dh, ql), F32)]
    grid_spec = pltpu.PrefetchScalarGridSpec(
        num_scalar_prefetch=1,
        grid=(b, g, nqb),
        in_specs=[
            pl.BlockSpec((Q_BLOCK, NSA_HPG * dh), lambda bi, gi, qi, fw: (bi * nqb + qi, q_col + gi)),
            pl.BlockSpec((None, KEY_LANES - dh, ql), lambda bi, gi, qi, fw: (gi, 0, 0)),
            seq(ns, SLC_BLOCK, KEY_LANES), seq(ns, SLC_BLOCK, dh),
            seq(ns, SLC_BLOCK, KEY_LANES), seq(ns, SLC_BLOCK, dh),
            tile(ns, Q_BLOCK), tile(dh, ql), tile(3, ql),
            pl.BlockSpec((None, 1, ql), lambda bi, gi, qi, fw: (gi, 0, 0)),
        ],
        out_specs=pl.BlockSpec((Q_BLOCK, NSA_HPG * dh), lambda bi, gi, qi, fw: (bi * nqb + qi, gi)),
        scratch_shapes=[
            pltpu.VMEM((KEY_LANES, ql), BF16),
            pltpu.VMEM((2, gkeys, KEY_LANES), BF16), pltpu.VMEM((2, gkeys, ql), F32),
            pltpu.VMEM((gkeys, dh), BF16), pltpu.VMEM((gkeys, ql), BF16), *state,
            pltpu.VMEM((wkeys, KEY_LANES), BF16), pltpu.VMEM((wkeys, ql), F32),
            pltpu.VMEM((wkeys, dh), BF16), pltpu.VMEM((wkeys, ql), BF16), *state,
            pltpu.VMEM((fkeys, KEY_LANES), BF16), pltpu.VMEM((fkeys, ql), F32),
            pltpu.VMEM((fkeys, dh), BF16), pltpu.VMEM((fkeys, ql), BF16),
            pltpu.SMEM((ns + LIST_PAD,), jnp.int32),
        ],
    )
    return pl.pallas_call(
        functools.partial(_nsa_attend_kernel, words_per_tile=ns // FLAG_BITS),
        grid_spec=grid_spec,
        out_shape=jax.ShapeDtypeStruct((b * nqb * Q_BLOCK, g * NSA_HPG * dh), BF16),
        compiler_params=_params("parallel", "parallel", "arbitrary"),
        name="nsa_attend",
    )(flag_words, u, slope_rows, ks, vs, kw, vw, sel, oc, gates, slopes)


def nsa_mixer(u, q_col, kc, vc, ks, vs, kw, vw, gate_logits, cmp_pos, cmp_w1, cmp_b1, cmp_w2, cmp_b2, *,
              batch, seq):
    g, hpg, dh = NSA_KV_GROUPS, NSA_HPG, NSA_HEAD_DIM
    nqb, ns, nc = seq // Q_BLOCK, seq // SLC_BLOCK, seq // CMP_STRIDE

    gates = gate_logits.astype(F32).reshape(batch, nqb, Q_BLOCK, 3, g, hpg).transpose(0, 4, 1, 3, 5, 2)
    gates = gates.reshape(batch, g, nqb, 3, QL)
    slopes = 2.0 ** (-8.0 * jnp.arange(1, NSA_HEADS + 1, dtype=F32) / NSA_HEADS)
    slopes = jnp.repeat(slopes.reshape(g, 1, hpg), Q_BLOCK, axis=2)

    def halves(x):
        x = x.reshape(batch, nc, CMP_STRIDE, g, dh).transpose(0, 3, 1, 2, 4)
        return x.reshape(batch * g, nc, CMP_STRIDE * dh)

    def value_blocks(x):
        return x.reshape(batch, ns, SLC_BLOCK, g, dh).transpose(0, 3, 1, 2, 4)

    def key_blocks(x):
        r = jnp.arange(SLC_BLOCK, dtype=F32).astype(BF16).reshape(1, 1, 1, SLC_BLOCK, 1)
        r = jnp.broadcast_to(r, (batch, g, ns, SLC_BLOCK, SLOPE_PARTS))
        pad = jnp.zeros((batch, g, ns, SLC_BLOCK, KEY_LANES - dh - SLOPE_PARTS), BF16)
        return jnp.concatenate([value_blocks(x), r, pad], axis=-1)

    s2 = slopes * LOG2E
    parts = []
    for _ in range(SLOPE_PARTS):
        part = s2.astype(BF16)
        parts.append(part)
        s2 = s2 - part.astype(F32)
    slope_rows = jnp.concatenate(parts + [jnp.zeros((g, KEY_LANES - dh - SLOPE_PARTS, QL), BF16)], axis=1)

    cmp = compress_tokens(jnp.stack([halves(kc), halves(vc)]), cmp_pos, cmp_w1, cmp_b1, cmp_w2, cmp_b2)
    cmp = cmp.astype(BF16).reshape(2, batch, g, nc, dh)
    r = (jnp.arange(nc) % CMP_CHUNK * CMP_STRIDE).astype(BF16).reshape(1, 1, nc, 1)
    k_cmp = jnp.concatenate([cmp[0], jnp.broadcast_to(r, (batch, g, nc, SLOPE_PARTS)),
                             jnp.zeros((batch, g, nc, KEY_LANES - dh - SLOPE_PARTS), BF16)], axis=-1)
    v_cmp_t = cmp[1].transpose(0, 1, 3, 2)

    oc, sel, flags = nsa_select(u, q_col, slope_rows, k_cmp, v_cmp_t, slopes)
    flag_words = flags[:, :, :, :, 0].astype(jnp.int32).reshape(-1)
    return nsa_attend(flag_words, u, q_col, slope_rows, key_blocks(ks), value_blocks(vs), key_blocks(kw),
                      value_blocks(vw), sel, oc, gates, slopes)


ROW_TILE = 1024
COL_TILE = 1024
OUT_ROW_TILE = 512
FFN_ROW_TILE = 1024
FFN_HID_TILE = 512
CONV_SEQ_TILE = 512
HGRN_SEQ_TILE = 1024


def _even_layer(x, nw, w_in, conv_w, conv_b, ln_w, ln_b, cmp_pos, cmp_w1, cmp_b1, cmp_w2, cmp_b2, w_out, *,
                batch, seq):
    c = conv_w.shape[-1]
    nq = NSA_HEADS * NSA_HEAD_DIM
    nkv = NSA_KV_GROUPS * NSA_HEAD_DIM
    n_in = w_in.shape[1]
    n_pad = -(-n_in // COL_TILE) * COL_TILE
    w_in = jnp.pad(w_in, ((0, 0), (0, n_pad - n_in))).astype(BF16)
    u = norm_matmul(x, nw, w_in, tm=ROW_TILE, tn=COL_TILE)
    a_out = conformer_conv(u, conv_w.reshape(CONV_WIDTH, c), conv_b, ln_w, ln_b, seq=seq, ts=CONV_SEQ_TILE)
    off = 2 * c
    q_col = off // (NSA_HPG * NSA_HEAD_DIM)
    off += nq
    kvs = [u[:, off + i * nkv:off + (i + 1) * nkv] for i in range(6)]
    off += 6 * nkv
    gate_logits = u[:, off:off + 3 * NSA_HEADS]
    b_out = nsa_mixer(u, q_col, *kvs, gate_logits, cmp_pos, cmp_w1, cmp_b1, cmp_w2, cmp_b2,
                      batch=batch, seq=seq)
    return matmul_residual(a_out, b_out, 0, 0, w_out.astype(BF16), x, tm=OUT_ROW_TILE, tn=w_out.shape[1])


def _odd_layer(x, nw, w_in, lb_gamma, gnorm_w, w_out, *, layer, batch, seq):
    d = w_in.shape[0]
    w_in = w_in.reshape(d, 4, HG_HEADS, HG_DK).transpose(0, 2, 1, 3).reshape(d, 4 * HG_HEADS * HG_DK)
    u = norm_matmul(x, nw, w_in.astype(BF16), tm=ROW_TILE, tn=COL_TILE)
    o = hgrn2(u, lb_gamma, gnorm_w, layer=layer, batch=batch, seq=seq, tt=HGRN_SEQ_TILE)
    return matmul_residual(o, o, 0, 1, w_out.astype(BF16), x, tm=OUT_ROW_TILE, tn=w_out.shape[1])


def kernel(x, norm_w, final_norm_w, ev_w_in, ev_conv_w, ev_conv_b, ev_conv_ln_w, ev_conv_ln_b, ev_cmp_pos,
           ev_cmp_w1, ev_cmp_b1, ev_cmp_w2, ev_cmp_b2, ev_w_out, od_w_in, od_lb_gamma, od_gnorm_w, od_w_out,
           ffn_w_gu, ffn_w_down):
    batch, seq, d = x.shape
    depth = norm_w.shape[0]
    xs = x.reshape(batch * seq, d)
    for layer in range(depth):
        i = layer // 2
        if layer % 2 == 0:
            xs = _even_layer(xs, norm_w[layer, 0], ev_w_in[i], ev_conv_w[i], ev_conv_b[i], ev_conv_ln_w[i],
                             ev_conv_ln_b[i], ev_cmp_pos[i], ev_cmp_w1[i], ev_cmp_b1[i], ev_cmp_w2[i],
                             ev_cmp_b2[i], ev_w_out[i], batch=batch, seq=seq)
        else:
            xs = _odd_layer(xs, norm_w[layer, 0], od_w_in[i], od_lb_gamma.astype(F32), od_gnorm_w[i],
                            od_w_out[i], layer=layer, batch=batch, seq=seq)
        xs = ffn_block(xs, norm_w[layer, 1], ffn_w_gu[layer].astype(BF16), ffn_w_down[layer].astype(BF16),
                       final_norm_w, tm=FFN_ROW_TILE, th=FFN_HID_TILE, final_norm=layer == depth - 1)
    return xs.reshape(batch, seq, d)
```

```python
import functools

import jax
import jax.numpy as jnp
import numpy as np
from jax import lax
from jax.experimental import pallas as pl
from jax.experimental.pallas import tpu as pltpu

F32 = jnp.float32
BF16 = jnp.bfloat16

EPS = 1e-6
TINY = 1e-30
NEG = -1e30

VMEM_LIMIT_BYTES = 56 * 1024 * 1024

CONV_WIDTH = 31
NSA_HEADS = 16
NSA_HEAD_DIM = 64
NSA_KV_GROUPS = 4
NSA_HPG = NSA_HEADS // NSA_KV_GROUPS
CMP_STRIDE = 16
CMP_BLOCK = 32
SLC_BLOCK = 64
SLC_TOPK = 16
WINDOW = 512
Q_BLOCK = 128
HG_HEADS = 16
HG_DK = 128
HG_DV = 128
HG_CHUNK = 64
HG_SUB = 16
HG_GROUP = 4


def _params(*sem):
    return pltpu.CompilerParams(dimension_semantics=sem, vmem_limit_bytes=VMEM_LIMIT_BYTES)


def _sigmoid(x):
    return 1.0 / (1.0 + jnp.exp(-x))


def _silu(x):
    return x * _sigmoid(x)


def _dot(a, b):
    return jnp.dot(a, b, preferred_element_type=F32)


def _dot_nt(a, b):
    return lax.dot_general(a, b, (((1,), (1,)), ((), ())), preferred_element_type=F32)


def _dot_tn(a, b):
    return lax.dot_general(a, b, (((0,), (0,)), ((), ())), preferred_element_type=F32)


def _norm_matmul_kernel(x_ref, nw_ref, w_ref, o_ref, h_ref):
    @pl.when(pl.program_id(1) == 0)
    def _():
        x = x_ref[...]
        ms = jnp.mean(x * x, axis=-1, keepdims=True)
        h_ref[...] = (x * lax.rsqrt(ms + EPS) * nw_ref[...]).astype(BF16)

    o_ref[...] = _dot(h_ref[...], w_ref[...]).astype(o_ref.dtype)


def norm_matmul(x, nw, w, *, tm, tn, out_dtype=BF16):
    m, k = x.shape
    n = w.shape[1]
    return pl.pallas_call(
        _norm_matmul_kernel,
        grid=(m // tm, n // tn),
        in_specs=[
            pl.BlockSpec((tm, k), lambda i, j: (i, 0)),
            pl.BlockSpec((1, k), lambda i, j: (0, 0)),
            pl.BlockSpec((k, tn), lambda i, j: (0, j)),
        ],
        out_specs=pl.BlockSpec((tm, tn), lambda i, j: (i, j)),
        out_shape=jax.ShapeDtypeStruct((m, n), out_dtype),
        scratch_shapes=[pltpu.VMEM((tm, k), BF16)],
        compiler_params=_params("parallel", "arbitrary"),
        name="norm_matmul",
    )(x, nw.reshape(1, k), w)


def _matmul_res_kernel(a1_ref, a2_ref, w1_ref, w2_ref, r_ref, o_ref):
    acc = _dot(a1_ref[...], w1_ref[...]) + _dot(a2_ref[...], w2_ref[...])
    o_ref[...] = r_ref[...] + acc


def matmul_residual(a1, a2, blk1, blk2, w, res, *, tm, tn):
    m = res.shape[0]
    k, n = w.shape
    kh = k // 2
    return pl.pallas_call(
        _matmul_res_kernel,
        grid=(m // tm, n // tn),
        in_specs=[
            pl.BlockSpec((tm, kh), lambda i, j: (i, blk1)),
            pl.BlockSpec((tm, kh), lambda i, j: (i, blk2)),
            pl.BlockSpec((kh, tn), lambda i, j: (0, j)),
            pl.BlockSpec((kh, tn), lambda i, j: (1, j)),
            pl.BlockSpec((tm, tn), lambda i, j: (i, j)),
        ],
        out_specs=pl.BlockSpec((tm, tn), lambda i, j: (i, j)),
        out_shape=jax.ShapeDtypeStruct((m, n), F32),
        compiler_params=_params("parallel", "arbitrary"),
        name="matmul_residual",
    )(a1, a2, w, w, res)


def _ffn_kernel(x_ref, nw_ref, wg_ref, wu_ref, wd_ref, fw_ref, o_ref, h_ref, *, final_norm):
    j = pl.program_id(1)

    @pl.when(j == 0)
    def _():
        x = x_ref[...]
        ms = jnp.mean(x * x, axis=-1, keepdims=True)
        h_ref[...] = (x * lax.rsqrt(ms + EPS) * nw_ref[...]).astype(BF16)
        o_ref[...] = x

    h = h_ref[...]
    a = _dot(h, wg_ref[...])
    b = _dot(h, wu_ref[...])
    z = (_silu(a) * b).astype(BF16)
    o_ref[...] += _dot(z, wd_ref[...])

    if final_norm:
        @pl.when(j == pl.num_programs(1) - 1)
        def _():
            y = o_ref[...]
            ms = jnp.mean(y * y, axis=-1, keepdims=True)
            o_ref[...] = y * lax.rsqrt(ms + EPS) * fw_ref[...]


def ffn_block(x, nw, w_gu, w_down, final_w, *, tm, th, final_norm):
    m, d = x.shape
    hid = w_down.shape[0]
    nh = hid // th
    return pl.pallas_call(
        functools.partial(_ffn_kernel, final_norm=final_norm),
        grid=(m // tm, nh),
        in_specs=[
            pl.BlockSpec((tm, d), lambda i, j: (i, 0)),
            pl.BlockSpec((1, d), lambda i, j: (0, 0)),
            pl.BlockSpec((d, th), lambda i, j: (0, j)),
            pl.BlockSpec((d, th), lambda i, j: (0, j + nh)),
            pl.BlockSpec((th, d), lambda i, j: (j, 0)),
            pl.BlockSpec((1, d), lambda i, j: (0, 0)),
        ],
        out_specs=pl.BlockSpec((tm, d), lambda i, j: (i, 0)),
        out_shape=jax.ShapeDtypeStruct((m, d), F32),
        scratch_shapes=[pltpu.VMEM((tm, d), BF16)],
        compiler_params=_params("parallel", "arbitrary"),
        name="ffn_block",
    )(x, nw.reshape(1, d), w_gu, w_gu, w_down, final_w.reshape(1, d))


CONV_HALO = 32
CONV_ROWS = 32


def _conv_kernel(a_ref, g_ref, ah_ref, gh_ref, cw_ref, cb_ref, lw_ref, lb_ref, o_ref, hs_ref, sh_ref, cv_ref, *,
                 tiles_per_seq):
    ts = a_ref.shape[0]
    first = (pl.program_id(0) % tiles_per_seq) == 0
    hprev = ah_ref[...].astype(F32) * _sigmoid(gh_ref[...].astype(F32))
    hs_ref[0:CONV_HALO, :] = jnp.where(first, 0.0, hprev)
    hs_ref[CONV_HALO:CONV_HALO + ts, :] = a_ref[...].astype(F32) * _sigmoid(g_ref[...].astype(F32))
    off = CONV_HALO - (CONV_WIDTH - 1)
    n_shifted = sh_ref.shape[1]
    for r in range(1, 8):
        sh_ref[r - 1] = hs_ref[r:r + n_shifted, :]

    def body(i, carry):
        base = pl.multiple_of(i * CONV_ROWS, CONV_ROWS)
        acc = jnp.zeros((CONV_ROWS // 8, 8, hs_ref.shape[1]), F32) + cb_ref[...]
        for w in range(CONV_WIDTH):
            r, a = (off + w) % 8, (off + w) // 8 * 8
            src = hs_ref if r == 0 else sh_ref.at[r - 1]
            rows = src[pl.ds(base + a, CONV_ROWS), :].reshape(CONV_ROWS // 8, 8, -1)
            acc = acc + rows * cw_ref[w * 8:(w + 1) * 8, :]
        cv_ref[pl.ds(base, CONV_ROWS), :] = acc.reshape(CONV_ROWS, -1)
        return carry

    lax.fori_loop(0, ts // CONV_ROWS, body, 0)

    y = cv_ref[...]
    mu = jnp.mean(y, axis=-1, keepdims=True)
    d = y - mu
    var = jnp.mean(d * d, axis=-1, keepdims=True)
    hn = d * lax.rsqrt(var + EPS) * lw_ref[...] + lb_ref[...]
    o_ref[...] = _silu(hn).astype(o_ref.dtype)


def conformer_conv(u, conv_w, conv_b, ln_w, ln_b, *, seq, ts):
    t = u.shape[0]
    c = conv_w.shape[1]
    hb = ts // CONV_HALO
    return pl.pallas_call(
        functools.partial(_conv_kernel, tiles_per_seq=seq // ts),
        grid=(t // ts,),
        in_specs=[
            pl.BlockSpec((ts, c), lambda i: (i, 0)),
            pl.BlockSpec((ts, c), lambda i: (i, 1)),
            pl.BlockSpec((CONV_HALO, c), lambda i: (jnp.maximum(i * hb - 1, 0), 0)),
            pl.BlockSpec((CONV_HALO, c), lambda i: (jnp.maximum(i * hb - 1, 0), 1)),
            pl.BlockSpec((CONV_WIDTH * 8, c), lambda i: (0, 0)),
            pl.BlockSpec((1, c), lambda i: (0, 0)),
            pl.BlockSpec((1, c), lambda i: (0, 0)),
            pl.BlockSpec((1, c), lambda i: (0, 0)),
        ],
        out_specs=pl.BlockSpec((ts, c), lambda i: (i, 0)),
        out_shape=jax.ShapeDtypeStruct((t, c), BF16),
        scratch_shapes=[pltpu.VMEM((CONV_HALO + ts, c), F32), pltpu.VMEM((7, CONV_HALO + ts - 8, c), F32),
                        pltpu.VMEM((ts, c), F32)],
        compiler_params=_params("parallel"),
        name="conformer_conv",
    )(u, u, u, u, jnp.repeat(conv_w, 8, axis=0), conv_b.reshape(1, c), ln_w.reshape(1, c), ln_b.reshape(1, c))


def _compress_kernel(x_ref, pos_ref, w1_ref, b1_ref, w2_ref, b2_ref, o_ref, sh_ref):
    n = x_ref.shape[0]
    hw = x_ref.shape[1]
    x = x_ref[...].astype(F32)
    xa = (x + pos_ref[:, 0:hw]).astype(BF16)
    xb = (x + pos_ref[:, hw:2 * hw]).astype(BF16)
    p1 = _dot(xa, w1_ref[0:hw, :])
    sh_ref[0:n, :] = _dot(xb, w1_ref[hw:2 * hw, :])
    sh_ref[n:n + 8, :] = jnp.zeros((8, sh_ref.shape[1]), F32)
    hid = p1 + sh_ref[1:n + 1, :] + b1_ref[...]
    o_ref[...] = _dot(_silu(hid).astype(BF16), w2_ref[...]) + b2_ref[...]


def compress_tokens(xh, pos, w1, b1, w2, b2):
    _, bg, n, hw = xh.shape
    hid = w1.shape[-1]
    dh = w2.shape[-1]
    return pl.pallas_call(
        _compress_kernel,
        grid=(2, bg),
        in_specs=[
            pl.BlockSpec((None, None, n, hw), lambda s, i: (s, i, 0, 0)),
            pl.BlockSpec((None, 1, 2 * hw), lambda s, i: (s, 0, 0)),
            pl.BlockSpec((None, 2 * hw, hid), lambda s, i: (s, 0, 0)),
            pl.BlockSpec((None, 1, hid), lambda s, i: (s, 0, 0)),
            pl.BlockSpec((None, hid, dh), lambda s, i: (s, 0, 0)),
            pl.BlockSpec((None, 1, dh), lambda s, i: (s, 0, 0)),
        ],
        out_specs=pl.BlockSpec((None, None, n, dh), lambda s, i: (s, i, 0, 0)),
        out_shape=jax.ShapeDtypeStruct((2, bg, n, dh), F32),
        scratch_shapes=[pltpu.VMEM((n + 8, hid), F32)],
        compiler_params=_params("parallel", "parallel"),
        name="compress_tokens",
    )(xh, pos.reshape(2, 1, 2 * hw), w1.astype(BF16), b1.reshape(2, 1, hid), w2.astype(BF16),
      b2.reshape(2, 1, dh))


def _split_bf16(x):
    hi = x.astype(BF16)
    lo = (x - hi.astype(F32)).astype(BF16)
    return hi, lo


def _hgrn_kernel(x_ref, gamma_ref, gn_ref, tril_ref, sum_ref, o_ref,
                 st_ref, gc_ref, k_ref, hk_ref, qg_ref, u_ref, dcat_ref, ps_ref, opart_ref, *, layer):
    c, sub = HG_CHUNK, HG_SUB
    nsub = c // sub
    tt, dk = o_ref.shape
    n_chunks = tt // c
    q_ref, f_ref, i_ref, g_ref = (x_ref.at[:, i * dk:(i + 1) * dk] for i in range(4))

    @pl.when(pl.program_id(2) == 0)
    def _():
        st_ref[...] = jnp.zeros_like(st_ref)

    gamma = gamma_ref[...]
    e = jnp.exp(gamma - jnp.max(gamma, axis=0, keepdims=True))
    sm = e / jnp.sum(e, axis=0, keepdims=True)
    lb = jnp.sum(sm[0:layer + 1, :], axis=0, keepdims=True) - sm[0:1, :]

    def gates(rows):
        f = lb + (1.0 - lb) * _sigmoid(f_ref[rows, :].astype(F32))
        k = 1.0 - f
        hi, lo = _split_bf16(jnp.log2(jnp.maximum(f, TINY)))
        gc = _dot(tril_ref[...], hi) + _dot(tril_ref[...], lo)
        gc_ref[rows, :] = gc
        k_ref[rows, :] = k
        hk_ref[rows, :] = gc - jnp.log2(k)
        qg_ref[rows, :] = (q_ref[rows, :].astype(F32) * jnp.exp2(gc)).astype(BF16)

    r8 = lax.broadcasted_iota(jnp.int32, (1, 8, 1), 1)
    srow = lax.broadcasted_iota(jnp.int32, (c, 1), 0)
    tcol = lax.broadcasted_iota(jnp.int32, (c, c), 1)

    def half_major(x):
        return [jnp.concatenate([x[(2 * i + h) * 8:(2 * i + h + 1) * 8, :] for i in range(nsub)], axis=0)
                .reshape(nsub, 8, -1) for h in range(2)]

    def state_free_part(ci):
        rows = slice(ci * c, (ci + 1) * c)
        q = q_ref[rows, :].astype(F32)
        gc = gc_ref[rows, :]
        k = k_ref[rows, :]
        ivb = i_ref[rows, :]
        glast = gc[c - 1:c, :]
        u_ref[ci] = _dot_tn((k * jnp.exp2(glast - gc)).astype(BF16), ivb)

        qhat = [jnp.zeros((sub, dk), F32)]
        khat = []
        for si in range(1, nsub):
            lo_r, hi_r = si * sub, (si + 1) * sub
            ref = gc[lo_r - 1:lo_r, :]
            qhat.append(q[lo_r:hi_r, :] * jnp.exp2(gc[lo_r:hi_r, :] - ref))
            khat.append(k * jnp.exp2(jnp.where(srow < lo_r, ref - gc, NEG)))
        at = _dot_nt(jnp.concatenate(khat, axis=0).astype(BF16), jnp.concatenate(qhat, axis=0).astype(BF16))
        at_off = jnp.zeros((c, c), F32)
        for si in range(1, nsub):
            at_off = jnp.where(tcol // sub == si, at[(si - 1) * c:si * c, :], at_off)
        opart_ref[rows, :] = _dot_tn(at_off.astype(BF16), ivb)

        lo_rows = slice(ci * c, ci * c + c // 2)
        hi_rows = slice(ci * c + c // 2, (ci + 1) * c)
        q0, q1 = half_major(q)
        g0, g1 = half_major(gc)
        h0, h1 = half_major(hk_ref[rows, :])
        for s in range(sub // 2):
            key = h0[:, s:s + 1, :]
            d0 = q0 * jnp.exp2(jnp.where(r8 >= s, g0 - key, NEG))
            d1 = q1 * jnp.exp2(g1 - key)
            dcat_ref[lo_rows, s * dk:(s + 1) * dk] = d0.reshape(c // 2, dk).astype(BF16)
            dcat_ref[hi_rows, s * dk:(s + 1) * dk] = d1.reshape(c // 2, dk).astype(BF16)
        dcat_ref[lo_rows, (sub // 2) * dk:] = jnp.zeros((c // 2, (sub // 2) * dk), BF16)
        for s in range(sub // 2):
            key = h1[:, s:s + 1, :]
            d1 = q1 * jnp.exp2(jnp.where(r8 >= s, g1 - key, NEG))
            col = (sub // 2 + s) * dk
            dcat_ref[hi_rows, col:col + dk] = d1.reshape(c // 2, dk).astype(BF16)

    group = tril_ref.shape[0]
    for g0 in range(0, tt, group):
        grows = slice(g0, g0 + group)
        gates(grows)
        for ci in range(g0 // c, (g0 + group) // c):
            state_free_part(ci)
        ps_ref[grows, :] = _dot(dcat_ref[grows, :], sum_ref[...])

    glast = [gc_ref[(ci + 1) * c - 1:(ci + 1) * c, :] for ci in range(n_chunks)]
    glast += [jnp.zeros((1, dk), F32)] * (-n_chunks % 8)
    decay_cols = jnp.exp2(jnp.concatenate(glast, axis=0)).T
    st = st_ref[...]
    for ci in range(n_chunks):
        rows = slice(ci * c, (ci + 1) * c)
        opart_ref[rows, :] += _dot(qg_ref[rows, :], st.astype(BF16))
        st = decay_cols[:, ci:ci + 1] * st + u_ref[ci]
    st_ref[...] = st

    own_block = (lax.broadcasted_iota(jnp.int32, (c, dk), 1) // sub
                 == (lax.broadcasted_iota(jnp.int32, (c, dk), 0) // 8) % nsub)
    for ci in range(n_chunks):
        rows = slice(ci * c, (ci + 1) * c)
        ivb = i_ref[rows, :]
        a = jnp.where(own_block, ps_ref[rows, :], 0.0)
        od = _dot(a.astype(BF16), jnp.concatenate([ivb, jnp.zeros((dk - c, ivb.shape[1]), BF16)], axis=0))
        o = opart_ref[rows, :] + jnp.concatenate([od[(h * nsub + i) * 8:(h * nsub + i + 1) * 8, :]
                                                  for i in range(nsub) for h in range(2)], axis=0)
        o = o * lax.rsqrt(jnp.mean(o * o, axis=-1, keepdims=True) + EPS) * gn_ref[...]
        o_ref[rows, :] = (o * _silu(g_ref[rows, :].astype(F32))).astype(o_ref.dtype)


def hgrn2(u, lb_gamma, gnorm_w, *, layer, batch, seq, tt):
    h, dk, dv = HG_HEADS, HG_DK, HG_DV
    nt = seq // tt
    depth = lb_gamma.shape[0]
    r = np.arange(HG_GROUP * HG_CHUNK)
    tril = jnp.asarray((r[:, None] >= r[None, :]) & (r[:, None] // HG_CHUNK == r[None, :] // HG_CHUNK), BF16)
    key_offset = np.arange(HG_SUB * dk) // dk
    summer = jnp.asarray(key_offset[:, None] == np.arange(dk)[None, :] % HG_SUB, BF16)
    return pl.pallas_call(
        functools.partial(_hgrn_kernel, layer=layer),
        grid=(batch, h, nt),
        in_specs=[
            pl.BlockSpec((tt, 4 * dk), lambda b, hh, t: (b * nt + t, hh)),
            pl.BlockSpec((depth, dk), lambda b, hh, t: (0, hh)),
            pl.BlockSpec((1, dv), lambda b, hh, t: (0, hh)),
            pl.BlockSpec((HG_GROUP * HG_CHUNK, HG_GROUP * HG_CHUNK), lambda b, hh, t: (0, 0)),
            pl.BlockSpec((HG_SUB * dk, dk), lambda b, hh, t: (0, 0)),
        ],
        out_specs=pl.BlockSpec((tt, dv), lambda b, hh, t: (b * nt + t, hh)),
        out_shape=jax.ShapeDtypeStruct((batch * seq, h * dv), BF16),
        scratch_shapes=[
            pltpu.VMEM((dk, dv), F32),
            pltpu.VMEM((tt, dk), F32), pltpu.VMEM((tt, dk), F32), pltpu.VMEM((tt, dk), F32),
            pltpu.VMEM((tt, dk), BF16), pltpu.VMEM((tt // HG_CHUNK, dk, dv), F32),
            pltpu.VMEM((tt, HG_SUB * dk), BF16), pltpu.VMEM((tt, dk), F32), pltpu.VMEM((tt, dv), F32),
        ],
        compiler_params=_params("parallel", "parallel", "arbitrary"),
        name="hgrn2",
    )(u, lb_gamma, gnorm_w.reshape(1, h * dv), tril, summer)


QL = NSA_HPG * Q_BLOCK
FLAG_BITS = 16


def _tile_positions(first_tile, n_tiles=1):
    lane = lax.broadcasted_iota(jnp.int32, (1, n_tiles * QL), 1)
    return (first_tile + lane // QL) * Q_BLOCK + (lane & (Q_BLOCK - 1))


CMP_CHUNK = 128
SEL_CLASSES = 8
SEL_TILES = 4
KEY_LANES = 128
SLOPE_PARTS = 3
LOG2E = 1.4426950408889634


def _load_query(qa_ref, q_ref, sa_ref):
    dh = NSA_HEAD_DIM
    n_tiles = q_ref.shape[0] // Q_BLOCK
    qt = (q_ref[...].astype(F32) * (dh ** -0.5 * LOG2E)).T
    qa_ref[0:dh, :] = jnp.concatenate(
        [qt[h * dh:(h + 1) * dh, i * Q_BLOCK:(i + 1) * Q_BLOCK] for i in range(n_tiles) for h in range(NSA_HPG)],
        axis=1).astype(BF16)
    qa_ref[dh:, :] = jnp.concatenate([sa_ref[...]] * n_tiles, axis=1)


def _nsa_select_body(nch, t_min, q_ref, sa_ref, kc_ref, vct_ref, slope_ref, oc_ref, sel_ref, flag_ref,
                     qa_ref, s_ref, pb_ref, p_ref):
    ck = CMP_CHUNK
    nk = nch * ck
    ns = nk * CMP_STRIDE // SLC_BLOCK
    n_t = SEL_TILES
    lanes = n_t * QL
    first_tile = pl.program_id(2) * n_t
    _load_query(qa_ref, q_ref, sa_ref)
    t = _tile_positions(first_tile, n_t)
    slope2 = jnp.concatenate([slope_ref[...] * LOG2E] * n_t, axis=1)

    rows = []
    mc = None
    for ch in range(nch):
        rs = slice(ch * ck, (ch + 1) * ck)
        s = _dot(kc_ref[rs, :], qa_ref[...])
        if ((ch + 1) * ck - 1) * CMP_STRIDE + (CMP_BLOCK - 1) > t_min:
            end = (ch * ck + lax.broadcasted_iota(jnp.int32, (ck, 1), 0)) * CMP_STRIDE + (CMP_BLOCK - 1)
            s = jnp.where(t >= end, s, NEG)
        s_ref[rs, :] = s
        brow = slope2 * (ch * ck * CMP_STRIDE + (CMP_BLOCK - 1) - first_tile * Q_BLOCK).astype(F32)
        rows.append(brow)
        cm = jnp.max(s.reshape(ck // 8, 8, lanes), axis=0) + brow
        mc = cm if mc is None else jnp.maximum(mc, cm)
    m = jnp.max(mc, axis=0, keepdims=True)

    lsum = None
    for ch in range(nch):
        rs = slice(ch * ck, (ch + 1) * ck)
        p = jnp.exp2(s_ref[rs, :] - (m - rows[ch]))
        ls = jnp.sum(p.reshape(ck // 8, 8, lanes), axis=0)
        lsum = ls if lsum is None else lsum + ls
        s_ref[rs, :] = p
        pb_ref[rs, :] = p.astype(BF16)
    l = jnp.sum(lsum, axis=0, keepdims=True)
    inv = jnp.where(t >= CMP_BLOCK - 1, 1.0 / jnp.maximum(l, TINY), 0.0)
    oc = _dot(vct_ref[:, 0:nk], pb_ref[0:nk, :]) * inv
    for i in range(n_t):
        oc_ref[i] = oc[:, i * QL:(i + 1) * QL]

    for i in range(n_t):
        p_ref[i, 0:8, :] = jnp.zeros((8, Q_BLOCK), F32)
        p_ref[i, 8 + nk:16 + nk, :] = jnp.zeros((8, Q_BLOCK), F32)
    for ch in range(nch):
        pn = s_ref[ch * ck:(ch + 1) * ck, :] * inv
        for i in range(n_t):
            acc = pn[:, i * QL:i * QL + Q_BLOCK]
            for h in range(1, NSA_HPG):
                acc = acc + pn[:, i * QL + h * Q_BLOCK:i * QL + (h + 1) * Q_BLOCK]
            p_ref[i, 8 + ch * ck:8 + (ch + 1) * ck, :] = acc

    ratio = SLC_BLOCK // CMP_STRIDE
    imp = []
    for i in range(n_t):
        acc = p_ref[i, pl.ds(7, ns, stride=ratio), :]
        for r in range(1, ratio + 1):
            acc = acc + p_ref[i, pl.ds(7 + r, ns, stride=ratio), :]
        imp.append(acc)
    imp = jnp.concatenate(imp, axis=1)

    j = lax.broadcasted_iota(jnp.int32, (ns, 1), 0)
    cur = jnp.concatenate([t[:, i * QL:i * QL + Q_BLOCK] for i in range(n_t)], axis=1) // SLC_BLOCK
    forced = (j == 0) | (j == cur) | (j == cur - 1)
    imp = jnp.where(forced, NEG, jnp.where(j > cur, -1.0, imp))
    for _ in range(min(SLC_TOPK, ns) - 3):
        mx = jnp.max(imp, axis=0, keepdims=True)
        idx = jnp.min(jnp.where(imp == mx, j, ns), axis=0, keepdims=True)
        imp = jnp.where(j == idx, NEG, imp)
    sel = jnp.where((imp == NEG) & (j <= cur), 1.0, 0.0)
    ns_all, nw, nw_all = sel_ref.shape[1], ns // FLAG_BITS, flag_ref.shape[1]
    bit = jnp.left_shift(1, j & (FLAG_BITS - 1)).astype(F32)
    for i in range(n_t):
        sel_i = sel[:, i * Q_BLOCK:(i + 1) * Q_BLOCK]
        sel_ref[i, 0:ns, :] = sel_i
        if ns < ns_all:
            sel_ref[i, ns:, :] = jnp.zeros((ns_all - ns, Q_BLOCK), F32)
        chosen = jnp.max(sel_i, axis=1, keepdims=True)
        words = jnp.sum((chosen * bit).reshape(nw, FLAG_BITS, 1), axis=1)
        flag_ref[i, 0:nw, :] = jnp.broadcast_to(words, (nw, flag_ref.shape[2]))
        if nw < nw_all:
            flag_ref[i, nw:, :] = jnp.zeros((nw_all - nw, flag_ref.shape[2]), F32)


def _nsa_select_kernel(q_ref, sa_ref, kc_ref, *rest):
    n_chunks = kc_ref.shape[0] // CMP_CHUNK
    n_steps = kc_ref.shape[0] * CMP_STRIDE // (SEL_TILES * Q_BLOCK)
    n_cls = min(SEL_CLASSES, n_chunks)
    cls = pl.program_id(2) * n_cls // n_steps
    for c in range(n_cls):
        @pl.when(cls == c)
        def _(c=c):
            first_step = -(-c * n_steps // n_cls)
            _nsa_select_body((c + 1) * n_chunks // n_cls, first_step * SEL_TILES * Q_BLOCK,
                             q_ref, sa_ref, kc_ref, *rest)


def nsa_select(u, q_col, slope_rows, kc, vct, slopes):
    b, g, nc = kc.shape[:3]
    dh, ql, n_t = NSA_HEAD_DIM, QL, SEL_TILES
    nqb = u.shape[0] // (b * Q_BLOCK)
    n_steps = nqb // n_t
    ns = nc * CMP_STRIDE // SLC_BLOCK
    tiles = lambda *shape: pl.BlockSpec((None, None, n_t) + shape, lambda bi, gi, qi: (bi, gi, qi, 0, 0))
    return pl.pallas_call(
        _nsa_select_kernel,
        grid=(b, g, n_steps),
        in_specs=[
            pl.BlockSpec((n_t * Q_BLOCK, NSA_HPG * dh), lambda bi, gi, qi: (bi * n_steps + qi, q_col + gi)),
            pl.BlockSpec((None, KEY_LANES - dh, ql), lambda bi, gi, qi: (gi, 0, 0)),
            pl.BlockSpec((None, None, nc, KEY_LANES), lambda bi, gi, qi: (bi, gi, 0, 0)),
            pl.BlockSpec((None, None, dh, nc), lambda bi, gi, qi: (bi, gi, 0, 0)),
            pl.BlockSpec((None, 1, ql), lambda bi, gi, qi: (gi, 0, 0)),
        ],
        out_specs=[tiles(dh, ql), tiles(ns, Q_BLOCK), tiles(ns // FLAG_BITS, 128)],
        out_shape=[
            jax.ShapeDtypeStruct((b, g, nqb, dh, ql), F32),
            jax.ShapeDtypeStruct((b, g, nqb, ns, Q_BLOCK), F32),
            jax.ShapeDtypeStruct((b, g, nqb, ns // FLAG_BITS, 128), F32),
        ],
        scratch_shapes=[
            pltpu.VMEM((KEY_LANES, n_t * ql), BF16), pltpu.VMEM((nc, n_t * ql), F32),
            pltpu.VMEM((nc, n_t * ql), BF16), pltpu.VMEM((n_t, nc + 16, Q_BLOCK), F32),
        ],
        compiler_params=_params("parallel", "parallel", "parallel"),
        name="nsa_select",
    )(u, slope_rows, kc, vct, slopes)


ATT_GROUP = 8
LIST_PAD = 3 * ATT_GROUP
DIAG_BLOCKS = Q_BLOCK // SLC_BLOCK
WIN_BLOCKS = WINDOW // SLC_BLOCK


def _score_blocks(k_src, blocks, qa_ref, kcat_ref, s_ref):
    kb = SLC_BLOCK
    for b, j in enumerate(blocks):
        kcat_ref[b * kb:(b + 1) * kb, :] = k_src[jnp.maximum(j, 0)]
    n = len(blocks) * kb
    s_ref[0:n, :] = _dot(kcat_ref[0:n, :], qa_ref[...])


def _softmax_step(v_src, blocks, masks, sel_rows, t, slope2, blk0, s_ref, vcat_ref, p_ref, m_ref, l_ref, acc_ref):
    nb = len(blocks)
    kb = SLC_BLOCK
    for b, j in enumerate(blocks):
        vcat_ref[b * kb:(b + 1) * kb, :] = v_src[jnp.maximum(j, 0)]
    rows = []
    mc = None
    for b, j in enumerate(blocks):
        brow = slope2 * ((j - blk0) * kb).astype(F32)
        if sel_rows[b] is not None:
            brow = jnp.where(sel_rows[b] > 0.0, brow, NEG)
        brow = jnp.where(j >= 0, brow, NEG)
        rows.append(brow)
        s = s_ref[b * kb:(b + 1) * kb, :]
        if masks[b] is not None:
            kpos = j * kb + lax.broadcasted_iota(jnp.int32, (kb, 1), 0)
            valid = (t >= kpos) if masks[b] == "causal" else (t - kpos < WINDOW)
            s = jnp.where(valid, s, NEG)
            s_ref[b * kb:(b + 1) * kb, :] = s
        cm = jnp.max(s.reshape(kb // 8, 8, QL), axis=0) + brow
        mc = cm if mc is None else jnp.maximum(mc, cm)
    m_old = m_ref[...]
    m_new = jnp.maximum(m_old, jnp.max(mc, axis=0, keepdims=True))
    alpha = jnp.exp2(m_old - m_new)
    m_ref[...] = m_new

    lsum = None
    for b in range(nb):
        p = jnp.exp2(s_ref[b * kb:(b + 1) * kb, :] - (m_new - rows[b]))
        ls = jnp.sum(p.reshape(kb // 8, 8, QL), axis=0)
        lsum = ls if lsum is None else lsum + ls
        p_ref[b * kb:(b + 1) * kb, :] = p.astype(BF16)
    l_ref[...] = alpha * l_ref[...] + lsum
    acc_ref[...] = alpha * acc_ref[...] + _dot_tn(vcat_ref[0:nb * kb, :], p_ref[0:nb * kb, :])


def _nsa_attend_kernel(fw_ref, q_ref, sa_ref, ks_ref, vs_ref, kw_ref, vw_ref, sel_ref, oc_ref, gate_ref, slope_ref,
                       o_ref, qa_ref, kcat_ref, s_ref, vcat_ref, p_ref, m_ref, l_ref, acc_ref,
                       kcat_w_ref, s_w_ref, vcat_w_ref, p_w_ref, m_w_ref, l_w_ref, acc_w_ref,
                       kcat_f_ref, s_f_ref, vcat_f_ref, p_f_ref, list_ref, *,
                       words_per_tile):
    bi, gi, qb = pl.program_id(0), pl.program_id(1), pl.program_id(2)
    tile_id = (bi * pl.num_programs(1) + gi) * pl.num_programs(2) + qb
    _load_query(qa_ref, q_ref, sa_ref)
    t = _tile_positions(qb)
    slope2 = slope_ref[...] * LOG2E
    blk0 = qb * DIAG_BLOCKS
    diag = [blk0 + i for i in range(DIAG_BLOCKS)]
    past = [blk0 - WIN_BLOCKS + i for i in range(WIN_BLOCKS)]

    def sel_row(j):
        row = sel_ref[pl.ds(jnp.maximum(j, 0), 1), :]
        return jnp.concatenate([row] * NSA_HPG, axis=1)

    def reset(m, l, acc):
        m[...] = jnp.full_like(m, NEG)
        l[...] = jnp.zeros_like(l)
        acc[...] = jnp.zeros_like(acc)

    def result(l, acc):
        return acc[...] * (1.0 / jnp.maximum(jnp.sum(l[...], axis=0, keepdims=True), TINY))

    def scan_flags(w, n):
        base = w * FLAG_BITS
        word = fw_ref[tile_id * words_per_tile + w] & ((1 << jnp.minimum(blk0 - base, FLAG_BITS)) - 1)
        for i in range(FLAG_BITS):
            list_ref[n] = base + i
            n = n + ((word >> i) & 1)
        return n

    n_sel = lax.fori_loop(0, (blk0 + FLAG_BITS - 1) // FLAG_BITS, scan_flags, 0)
    for i in range(LIST_PAD):
        list_ref[n_sel + i] = -1

    def listed(i):
        return [list_ref[i * ATT_GROUP + b] for b in range(ATT_GROUP)]

    first = diag + listed(0)
    _score_blocks(kw_ref, past + diag, qa_ref, kcat_w_ref, s_w_ref)
    _score_blocks(ks_ref, first, qa_ref, kcat_f_ref, s_f_ref)
    _score_blocks(ks_ref, listed(1), qa_ref, kcat_ref.at[1], s_ref.at[1])

    slc_state = (m_ref, l_ref, acc_ref)
    reset(*slc_state)
    _softmax_step(vs_ref, first, ["causal"] * DIAG_BLOCKS + [None] * ATT_GROUP, [sel_row(j) for j in first],
                  t, slope2, blk0, s_f_ref, vcat_f_ref, p_f_ref, *slc_state)

    win_state = (m_w_ref, l_w_ref, acc_w_ref)
    reset(*win_state)
    masks = ["window"] * DIAG_BLOCKS + [None] * (WIN_BLOCKS - DIAG_BLOCKS) + ["causal"] * DIAG_BLOCKS
    _softmax_step(vw_ref, past + diag, masks, [None] * len(masks), t, slope2, blk0,
                  s_w_ref, vcat_w_ref, p_w_ref, *win_state)

    n_steps = jnp.maximum((n_sel + ATT_GROUP - 1) // ATT_GROUP - 1, 0)

    def slc_step(i, slot):
        _score_blocks(ks_ref, listed(i + 2), qa_ref, kcat_ref.at[slot], s_ref.at[slot])
        blocks = listed(i + 1)
        _softmax_step(vs_ref, blocks, [None] * ATT_GROUP, [sel_row(j) for j in blocks], t, slope2, blk0,
                      s_ref.at[1 - slot], vcat_ref, p_ref, *slc_state)

    def slc_body(i2, carry):
        slc_step(2 * i2, 0)

        @pl.when(2 * i2 + 1 < n_steps)
        def _():
            slc_step(2 * i2 + 1, 1)

        return carry

    lax.fori_loop(0, (n_steps + 1) // 2, slc_body, 0)
    o_slc = result(l_ref, acc_ref)
    o_win = result(l_w_ref, acc_w_ref)

    gate = _sigmoid(gate_ref[...])
    o = gate[0:1, :] * oc_ref[...] + gate[1:2, :] * o_slc + gate[2:3, :] * o_win
    o = jnp.concatenate([o[:, h * Q_BLOCK:(h + 1) * Q_BLOCK] for h in range(NSA_HPG)], axis=0)
    o_ref[...] = o.T.astype(o_ref.dtype)


def nsa_attend(flag_words, u, q_col, slope_rows, ks, vs, kw, vw, sel, oc, gates, slopes):
    b, g, ns = ks.shape[:3]
    dh, ql = NSA_HEAD_DIM, QL
    nqb = u.shape[0] // (b * Q_BLOCK)
    gkeys = ATT_GROUP * SLC_BLOCK
    wkeys = (WIN_BLOCKS + DIAG_BLOCKS) * SLC_BLOCK
    fkeys = (DIAG_BLOCKS + ATT_GROUP) * SLC_BLOCK
    tile = lambda *shape: pl.BlockSpec((None, None, None) + shape, lambda bi, gi, qi, fw: (bi, gi, qi, 0, 0))
    seq = lambda *shape: pl.BlockSpec((None, None) + shape, lambda bi, gi, qi, fw: (bi, gi, 0, 0, 0))
    state = [pltpu.VMEM((1, ql), F32), pltpu.VMEM((8, ql), F32), pltpu.VMEM((dh, ql), F32)]
    grid_spec = pltpu.PrefetchScalarGridSpec(
        num_scalar_prefetch=1,
        grid=(b, g, nqb),
        in_specs=[
            pl.BlockSpec((Q_BLOCK, NSA_HPG * dh), lambda bi, gi, qi, fw: (bi * nqb + qi, q_col + gi)),
            pl.BlockSpec((None, KEY_LANES - dh, ql), lambda bi, gi, qi, fw: (gi, 0, 0)),
            seq(ns, SLC_BLOCK, KEY_LANES), seq(ns, SLC_BLOCK, dh),
            seq(ns, SLC_BLOCK, KEY_LANES), seq(ns, SLC_BLOCK, dh),
            tile(ns, Q_BLOCK), tile(dh, ql), tile(3, ql),
            pl.BlockSpec((None, 1, ql), lambda bi, gi, qi, fw: (gi, 0, 0)),
        ],
        out_specs=pl.BlockSpec((Q_BLOCK, NSA_HPG * dh), lambda bi, gi, qi, fw: (bi * nqb + qi, gi)),
        scratch_shapes=[
            pltpu.VMEM((KEY_LANES, ql), BF16),
            pltpu.VMEM((2, gkeys, KEY_LANES), BF16), pltpu.VMEM((2, gkeys, ql), F32),
            pltpu.VMEM((gkeys, dh), BF16), pltpu.VMEM((gkeys, ql), BF16), *state,
            pltpu.VMEM((wkeys, KEY_LANES), BF16), pltpu.VMEM((wkeys, ql), F32),
            pltpu.VMEM((wkeys, dh), BF16), pltpu.VMEM((wkeys, ql), BF16), *state,
            pltpu.VMEM((fkeys, KEY_LANES), BF16), pltpu.VMEM((fkeys, ql), F32),
            pltpu.VMEM((fkeys, dh), BF16), pltpu.VMEM((fkeys, ql), BF16),
            pltpu.SMEM((ns + LIST_PAD,), jnp.int32),
        ],
    )
    return pl.pallas_call(
        functools.partial(_nsa_attend_kernel, words_per_tile=ns // FLAG_BITS),
        grid_spec=grid_spec,
        out_shape=jax.ShapeDtypeStruct((b * nqb * Q_BLOCK, g * NSA_HPG * dh), BF16),
        compiler_params=_params("parallel", "parallel", "arbitrary"),
        name="nsa_attend",
    )(flag_words, u, slope_rows, ks, vs, kw, vw, sel, oc, gates, slopes)


def nsa_mixer(u, q_col, kc, vc, ks, vs, kw, vw, gate_logits, cmp_pos, cmp_w1, cmp_b1, cmp_w2, cmp_b2, *,
              batch, seq):
    g, hpg, dh = NSA_KV_GROUPS, NSA_HPG, NSA_HEAD_DIM
    nqb, ns, nc = seq // Q_BLOCK, seq // SLC_BLOCK, seq // CMP_STRIDE

    gates = gate_logits.astype(F32).reshape(batch, nqb, Q_BLOCK, 3, g, hpg).transpose(0, 4, 1, 3, 5, 2)
    gates = gates.reshape(batch, g, nqb, 3, QL)
    slopes = 2.0 ** (-8.0 * jnp.arange(1, NSA_HEADS + 1, dtype=F32) / NSA_HEADS)
    slopes = jnp.repeat(slopes.reshape(g, 1, hpg), Q_BLOCK, axis=2)

    def halves(x):
        x = x.reshape(batch, nc, CMP_STRIDE, g, dh).transpose(0, 3, 1, 2, 4)
        return x.reshape(batch * g, nc, CMP_STRIDE * dh)

    def value_blocks(x):
        return x.reshape(batch, ns, SLC_BLOCK, g, dh).transpose(0, 3, 1, 2, 4)

    def key_blocks(x):
        r = jnp.arange(SLC_BLOCK, dtype=F32).astype(BF16).reshape(1, 1, 1, SLC_BLOCK, 1)
        r = jnp.broadcast_to(r, (batch, g, ns, SLC_BLOCK, SLOPE_PARTS))
        pad = jnp.zeros((batch, g, ns, SLC_BLOCK, KEY_LANES - dh - SLOPE_PARTS), BF16)
        return jnp.concatenate([value_blocks(x), r, pad], axis=-1)

    s2 = slopes * LOG2E
    parts = []
    for _ in range(SLOPE_PARTS):
        part = s2.astype(BF16)
        parts.append(part)
        s2 = s2 - part.astype(F32)
    slope_rows = jnp.concatenate(parts + [jnp.zeros((g, KEY_LANES - dh - SLOPE_PARTS, QL), BF16)], axis=1)

    cmp = compress_tokens(jnp.stack([halves(kc), halves(vc)]), cmp_pos, cmp_w1, cmp_b1, cmp_w2, cmp_b2)
    cmp = cmp.astype(BF16).reshape(2, batch, g, nc, dh)
    r = (jnp.arange(nc) % CMP_CHUNK * CMP_STRIDE).astype(BF16).reshape(1, 1, nc, 1)
    k_cmp = jnp.concatenate([cmp[0], jnp.broadcast_to(r, (batch, g, nc, SLOPE_PARTS)),
                             jnp.zeros((batch, g, nc, KEY_LANES - dh - SLOPE_PARTS), BF16)], axis=-1)
    v_cmp_t = cmp[1].transpose(0, 1, 3, 2)

    oc, sel, flags = nsa_select(u, q_col, slope_rows, k_cmp, v_cmp_t, slopes)
    flag_words = flags[:, :, :, :, 0].astype(jnp.int32).reshape(-1)
    return nsa_attend(flag_words, u, q_col, slope_rows, key_blocks(ks), value_blocks(vs), key_blocks(kw),
                      value_blocks(vw), sel, oc, gates, slopes)


ROW_TILE = 1024
COL_TILE = 1024
OUT_ROW_TILE = 512
FFN_ROW_TILE = 1024
FFN_HID_TILE = 512
CONV_SEQ_TILE = 512
HGRN_SEQ_TILE = 1024


def _even_layer(x, nw, w_in, conv_w, conv_b, ln_w, ln_b, cmp_pos, cmp_w1, cmp_b1, cmp_w2, cmp_b2, w_out, *,
                batch, seq):
    c = conv_w.shape[-1]
    nq = NSA_HEADS * NSA_HEAD_DIM
    nkv = NSA_KV_GROUPS * NSA_HEAD_DIM
    n_in = w_in.shape[1]
    n_pad = -(-n_in // COL_TILE) * COL_TILE
    w_in = jnp.pad(w_in, ((0, 0), (0, n_pad - n_in))).astype(BF16)
    u = norm_matmul(x, nw, w_in, tm=ROW_TILE, tn=COL_TILE)
    a_out = conformer_conv(u, conv_w.reshape(CONV_WIDTH, c), conv_b, ln_w, ln_b, seq=seq, ts=CONV_SEQ_TILE)
    off = 2 * c
    q_col = off // (NSA_HPG * NSA_HEAD_DIM)
    off += nq
    kvs = [u[:, off + i * nkv:off + (i + 1) * nkv] for i in range(6)]
    off += 6 * nkv
    gate_logits = u[:, off:off + 3 * NSA_HEADS]
    b_out = nsa_mixer(u, q_col, *kvs, gate_logits, cmp_pos, cmp_w1, cmp_b1, cmp_w2, cmp_b2,
                      batch=batch, seq=seq)
    return matmul_residual(a_out, b_out, 0, 0, w_out.astype(BF16), x, tm=OUT_ROW_TILE, tn=w_out.shape[1])


def _odd_layer(x, nw, w_in, lb_gamma, gnorm_w, w_out, *, layer, batch, seq):
    d = w_in.shape[0]
    w_in = w_in.reshape(d, 4, HG_HEADS, HG_DK).transpose(0, 2, 1, 3).reshape(d, 4 * HG_HEADS * HG_DK)
    u = norm_matmul(x, nw, w_in.astype(BF16), tm=ROW_TILE, tn=COL_TILE)
    o = hgrn2(u, lb_gamma, gnorm_w, layer=layer, batch=batch, seq=seq, tt=HGRN_SEQ_TILE)
    return matmul_residual(o, o, 0, 1, w_out.astype(BF16), x, tm=OUT_ROW_TILE, tn=w_out.shape[1])


def kernel(x, norm_w, final_norm_w, ev_w_in, ev_conv_w, ev_conv_b, ev_conv_ln_w, ev_conv_ln_b, ev_cmp_pos,
           ev_cmp_w1, ev_cmp_b1, ev_cmp_w2, ev_cmp_b2, ev_w_out, od_w_in, od_lb_gamma, od_gnorm_w, od_w_out,
           ffn_w_gu, ffn_w_down):
    batch, seq, d = x.shape
    depth = norm_w.shape[0]
    xs = x.reshape(batch * seq, d)
    for layer in range(depth):
        i = layer // 2
        if layer % 2 == 0:
            xs = _even_layer(xs, norm_w[layer, 0], ev_w_in[i], ev_conv_w[i], ev_conv_b[i], ev_conv_ln_w[i],
                             ev_conv_ln_b[i], ev_cmp_pos[i], ev_cmp_w1[i], ev_cmp_b1[i], ev_cmp_w2[i],
                             ev_cmp_b2[i], ev_w_out[i], batch=batch, seq=seq)
        else:
            xs = _odd_layer(xs, norm_w[layer, 0], od_w_in[i], od_lb_gamma.astype(F32), od_gnorm_w[i],
                            od_w_out[i], layer=layer, batch=batch, seq=seq)
        xs = ffn_block(xs, norm_w[layer, 1], ffn_w_gu[layer].astype(BF16), ffn_w_down[layer].astype(BF16),
                       final_norm_w, tm=FFN_ROW_TILE, th=FFN_HID_TILE, final_norm=layer == depth - 1)
    return xs.reshape(batch, seq, d)
```

```python
import functools

import jax
import jax.numpy as jnp
import numpy as np
from jax import lax
from jax.experimental import pallas as pl
from jax.experimental.pallas import tpu as pltpu

F32 = jnp.float32
BF16 = jnp.bfloat16

EPS = 1e-6
TINY = 1e-30
NEG = -1e30

VMEM_LIMIT_BYTES = 56 * 1024 * 1024

CONV_WIDTH = 31
NSA_HEADS = 16
NSA_HEAD_DIM = 64
NSA_KV_GROUPS = 4
NSA_HPG = NSA_HEADS // NSA_KV_GROUPS
CMP_STRIDE = 16
CMP_BLOCK = 32
SLC_BLOCK = 64
SLC_TOPK = 16
WINDOW = 512
Q_BLOCK = 128
HG_HEADS = 16
HG_DK = 128
HG_DV = 128
HG_CHUNK = 64
HG_SUB = 16
HG_GROUP = 4


def _params(*sem):
    return pltpu.CompilerParams(dimension_semantics=sem, vmem_limit_bytes=VMEM_LIMIT_BYTES)


def _sigmoid(x):
    return 1.0 / (1.0 + jnp.exp(-x))


def _silu(x):
    return x * _sigmoid(x)


def _dot(a, b):
    return jnp.dot(a, b, preferred_element_type=F32)


def _dot_nt(a, b):
    return lax.dot_general(a, b, (((1,), (1,)), ((), ())), preferred_element_type=F32)


def _dot_tn(a, b):
    return lax.dot_general(a, b, (((0,), (0,)), ((), ())), preferred_element_type=F32)


def _norm_matmul_kernel(x_ref, nw_ref, w_ref, o_ref, h_ref):
    @pl.when(pl.program_id(1) == 0)
    def _():
        x = x_ref[...]
        ms = jnp.mean(x * x, axis=-1, keepdims=True)
        h_ref[...] = (x * lax.rsqrt(ms + EPS) * nw_ref[...]).astype(BF16)

    o_ref[...] = _dot(h_ref[...], w_ref[...]).astype(o_ref.dtype)


def norm_matmul(x, nw, w, *, tm, tn, out_dtype=BF16):
    m, k = x.shape
    n = w.shape[1]
    return pl.pallas_call(
        _norm_matmul_kernel,
        grid=(m // tm, n // tn),
        in_specs=[
            pl.BlockSpec((tm, k), lambda i, j: (i, 0)),
            pl.BlockSpec((1, k), lambda i, j: (0, 0)),
            pl.BlockSpec((k, tn), lambda i, j: (0, j)),
        ],
        out_specs=pl.BlockSpec((tm, tn), lambda i, j: (i, j)),
        out_shape=jax.ShapeDtypeStruct((m, n), out_dtype),
        scratch_shapes=[pltpu.VMEM((tm, k), BF16)],
        compiler_params=_params("parallel", "arbitrary"),
        name="norm_matmul",
    )(x, nw.reshape(1, k), w)


def _matmul_res_kernel(a1_ref, a2_ref, w1_ref, w2_ref, r_ref, o_ref):
    acc = _dot(a1_ref[...], w1_ref[...]) + _dot(a2_ref[...], w2_ref[...])
    o_ref[...] = r_ref[...] + acc


def matmul_residual(a1, a2, blk1, blk2, w, res, *, tm, tn):
    m = res.shape[0]
    k, n = w.shape
    kh = k // 2
    return pl.pallas_call(
        _matmul_res_kernel,
        grid=(m // tm, n // tn),
        in_specs=[
            pl.BlockSpec((tm, kh), lambda i, j: (i, blk1)),
            pl.BlockSpec((tm, kh), lambda i, j: (i, blk2)),
            pl.BlockSpec((kh, tn), lambda i, j: (0, j)),
            pl.BlockSpec((kh, tn), lambda i, j: (1, j)),
            pl.BlockSpec((tm, tn), lambda i, j: (i, j)),
        ],
        out_specs=pl.BlockSpec((tm, tn), lambda i, j: (i, j)),
        out_shape=jax.ShapeDtypeStruct((m, n), F32),
        compiler_params=_params("parallel", "arbitrary"),
        name="matmul_residual",
    )(a1, a2, w, w, res)


def _ffn_kernel(x_ref, nw_ref, wg_ref, wu_ref, wd_ref, fw_ref, o_ref, h_ref, *, final_norm):
    j = pl.program_id(1)

    @pl.when(j == 0)
    def _():
        x = x_ref[...]
        ms = jnp.mean(x * x, axis=-1, keepdims=True)
        h_ref[...] = (x * lax.rsqrt(ms + EPS) * nw_ref[...]).astype(BF16)
        o_ref[...] = x

    h = h_ref[...]
    a = _dot(h, wg_ref[...])
    b = _dot(h, wu_ref[...])
    z = (_silu(a) * b).astype(BF16)
    o_ref[...] += _dot(z, wd_ref[...])

    if final_norm:
        @pl.when(j == pl.num_programs(1) - 1)
        def _():
            y = o_ref[...]
            ms = jnp.mean(y * y, axis=-1, keepdims=True)
            o_ref[...] = y * lax.rsqrt(ms + EPS) * fw_ref[...]


def ffn_block(x, nw, w_gu, w_down, final_w, *, tm, th, final_norm):
    m, d = x.shape
    hid = w_down.shape[0]
    nh = hid // th
    return pl.pallas_call(
        functools.partial(_ffn_kernel, final_norm=final_norm),
        grid=(m // tm, nh),
        in_specs=[
            pl.BlockSpec((tm, d), lambda i, j: (i, 0)),
            pl.BlockSpec((1, d), lambda i, j: (0, 0)),
            pl.BlockSpec((d, th), lambda i, j: (0, j)),
            pl.BlockSpec((d, th), lambda i, j: (0, j + nh)),
            pl.BlockSpec((th, d), lambda i, j: (j, 0)),
            pl.BlockSpec((1, d), lambda i, j: (0, 0)),
        ],
        out_specs=pl.BlockSpec((tm, d), lambda i, j: (i, 0)),
        out_shape=jax.ShapeDtypeStruct((m, d), F32),
        scratch_shapes=[pltpu.VMEM((tm, d), BF16)],
        compiler_params=_params("parallel", "arbitrary"),
        name="ffn_block",
    )(x, nw.reshape(1, d), w_gu, w_gu, w_down, final_w.reshape(1, d))


CONV_HALO = 32
CONV_ROWS = 32


def _conv_kernel(a_ref, g_ref, ah_ref, gh_ref, cw_ref, cb_ref, lw_ref, lb_ref, o_ref, hs_ref, sh_ref, cv_ref, *,
                 tiles_per_seq):
    ts = a_ref.shape[0]
    first = (pl.program_id(0) % tiles_per_seq) == 0
    hprev = ah_ref[...].astype(F32) * _sigmoid(gh_ref[...].astype(F32))
    hs_ref[0:CONV_HALO, :] = jnp.where(first, 0.0, hprev)
    hs_ref[CONV_HALO:CONV_HALO + ts, :] = a_ref[...].astype(F32) * _sigmoid(g_ref[...].astype(F32))
    off = CONV_HALO - (CONV_WIDTH - 1)
    n_shifted = sh_ref.shape[1]
    for r in range(1, 8):
        sh_ref[r - 1] = hs_ref[r:r + n_shifted, :]

    def body(i, carry):
        base = pl.multiple_of(i * CONV_ROWS, CONV_ROWS)
        acc = jnp.zeros((CONV_ROWS // 8, 8, hs_ref.shape[1]), F32) + cb_ref[...]
        for w in range(CONV_WIDTH):
            r, a = (off + w) % 8, (off + w) // 8 * 8
            src = hs_ref if r == 0 else sh_ref.at[r - 1]
            rows = src[pl.ds(base + a, CONV_ROWS), :].reshape(CONV_ROWS // 8, 8, -1)
            acc = acc + rows * cw_ref[w * 8:(w + 1) * 8, :]
        cv_ref[pl.ds(base, CONV_ROWS), :] = acc.reshape(CONV_ROWS, -1)
        return carry

    lax.fori_loop(0, ts // CONV_ROWS, body, 0)

    y = cv_ref[...]
    mu = jnp.mean(y, axis=-1, keepdims=True)
    d = y - mu
    var = jnp.mean(d * d, axis=-1, keepdims=True)
    hn = d * lax.rsqrt(var + EPS) * lw_ref[...] + lb_ref[...]
    o_ref[...] = _silu(hn).astype(o_ref.dtype)


def conformer_conv(u, conv_w, conv_b, ln_w, ln_b, *, seq, ts):
    t = u.shape[0]
    c = conv_w.shape[1]
    hb = ts // CONV_HALO
    return pl.pallas_call(
        functools.partial(_conv_kernel, tiles_per_seq=seq // ts),
        grid=(t // ts,),
        in_specs=[
            pl.BlockSpec((ts, c), lambda i: (i, 0)),
            pl.BlockSpec((ts, c), lambda i: (i, 1)),
            pl.BlockSpec((CONV_HALO, c), lambda i: (jnp.maximum(i * hb - 1, 0), 0)),
            pl.BlockSpec((CONV_HALO, c), lambda i: (jnp.maximum(i * hb - 1, 0), 1)),
            pl.BlockSpec((CONV_WIDTH * 8, c), lambda i: (0, 0)),
            pl.BlockSpec((1, c), lambda i: (0, 0)),
            pl.BlockSpec((1, c), lambda i: (0, 0)),
            pl.BlockSpec((1, c), lambda i: (0, 0)),
        ],
        out_specs=pl.BlockSpec((ts, c), lambda i: (i, 0)),
        out_shape=jax.ShapeDtypeStruct((t, c), BF16),
        scratch_shapes=[pltpu.VMEM((CONV_HALO + ts, c), F32), pltpu.VMEM((7, CONV_HALO + ts - 8, c), F32),
                        pltpu.VMEM((ts, c), F32)],
        compiler_params=_params("parallel"),
        name="conformer_conv",
    )(u, u, u, u, jnp.repeat(conv_w, 8, axis=0), conv_b.reshape(1, c), ln_w.reshape(1, c), ln_b.reshape(1, c))


def _compress_kernel(x_ref, pos_ref, w1_ref, b1_ref, w2_ref, b2_ref, o_ref, sh_ref):
    n = x_ref.shape[0]
    hw = x_ref.shape[1]
    x = x_ref[...].astype(F32)
    xa = (x + pos_ref[:, 0:hw]).astype(BF16)
    xb = (x + pos_ref[:, hw:2 * hw]).astype(BF16)
    p1 = _dot(xa, w1_ref[0:hw, :])
    sh_ref[0:n, :] = _dot(xb, w1_ref[hw:2 * hw, :])
    sh_ref[n:n + 8, :] = jnp.zeros((8, sh_ref.shape[1]), F32)
    hid = p1 + sh_ref[1:n + 1, :] + b1_ref[...]
    o_ref[...] = _dot(_silu(hid).astype(BF16), w2_ref[...]) + b2_ref[...]


def compress_tokens(xh, pos, w1, b1, w2, b2):
    _, bg, n, hw = xh.shape
    hid = w1.shape[-1]
    dh = w2.shape[-1]
    return pl.pallas_call(
        _compress_kernel,
        grid=(2, bg),
        in_specs=[
            pl.BlockSpec((None, None, n, hw), lambda s, i: (s, i, 0, 0)),
            pl.BlockSpec((None, 1, 2 * hw), lambda s, i: (s, 0, 0)),
            pl.BlockSpec((None, 2 * hw, hid), lambda s, i: (s, 0, 0)),
            pl.BlockSpec((None, 1, hid), lambda s, i: (s, 0, 0)),
            pl.BlockSpec((None, hid, dh), lambda s, i: (s, 0, 0)),
            pl.BlockSpec((None, 1, dh), lambda s, i: (s, 0, 0)),
        ],
        out_specs=pl.BlockSpec((None, None, n, dh), lambda s, i: (s, i, 0, 0)),
        out_shape=jax.ShapeDtypeStruct((2, bg, n, dh), F32),
        scratch_shapes=[pltpu.VMEM((n + 8, hid), F32)],
        compiler_params=_params("parallel", "parallel"),
        name="compress_tokens",
    )(xh, pos.reshape(2, 1, 2 * hw), w1.astype(BF16), b1.reshape(2, 1, hid), w2.astype(BF16),
      b2.reshape(2, 1, dh))


def _split_bf16(x):
    hi = x.astype(BF16)
    lo = (x - hi.astype(F32)).astype(BF16)
    return hi, lo


def _hgrn_kernel(x_ref, gamma_ref, gn_ref, tril_ref, sum_ref, o_ref,
                 st_ref, gc_ref, k_ref, hk_ref, qg_ref, u_ref, dcat_ref, ps_ref, opart_ref, *, layer):
    c, sub = HG_CHUNK, HG_SUB
    nsub = c // sub
    tt, dk = o_ref.shape
    n_chunks = tt // c
    q_ref, f_ref, i_ref, g_ref = (x_ref.at[:, i * dk:(i + 1) * dk] for i in range(4))

    @pl.when(pl.program_id(2) == 0)
    def _():
        st_ref[...] = jnp.zeros_like(st_ref)

    gamma = gamma_ref[...]
    e = jnp.exp(gamma - jnp.max(gamma, axis=0, keepdims=True))
    sm = e / jnp.sum(e, axis=0, keepdims=True)
    lb = jnp.sum(sm[0:layer + 1, :], axis=0, keepdims=True) - sm[0:1, :]

    def gates(rows):
        f = lb + (1.0 - lb) * _sigmoid(f_ref[rows, :].astype(F32))
        k = 1.0 - f
        hi, lo = _split_bf16(jnp.log2(jnp.maximum(f, TINY)))
        gc = _dot(tril_ref[...], hi) + _dot(tril_ref[...], lo)
        gc_ref[rows, :] = gc
        k_ref[rows, :] = k
        hk_ref[rows, :] = gc - jnp.log2(k)
        qg_ref[rows, :] = (q_ref[rows, :].astype(F32) * jnp.exp2(gc)).astype(BF16)

    r8 = lax.broadcasted_iota(jnp.int32, (1, 8, 1), 1)
    srow = lax.broadcasted_iota(jnp.int32, (c, 1), 0)
    tcol = lax.broadcasted_iota(jnp.int32, (c, c), 1)

    def half_major(x):
        return [jnp.concatenate([x[(2 * i + h) * 8:(2 * i + h + 1) * 8, :] for i in range(nsub)], axis=0)
                .reshape(nsub, 8, -1) for h in range(2)]

    def state_free_part(ci):
        rows = slice(ci * c, (ci + 1) * c)
        q = q_ref[rows, :].astype(F32)
        gc = gc_ref[rows, :]
        k = k_ref[rows, :]
        ivb = i_ref[rows, :]
        glast = gc[c - 1:c, :]
        u_ref[ci] = _dot_tn((k * jnp.exp2(glast - gc)).astype(BF16), ivb)

        qhat = [jnp.zeros((sub, dk), F32)]
        khat = []
        for si in range(1, nsub):
            lo_r, hi_r = si * sub, (si + 1) * sub
            ref = gc[lo_r - 1:lo_r, :]
            qhat.append(q[lo_r:hi_r, :] * jnp.exp2(gc[lo_r:hi_r, :] - ref))
            khat.append(k * jnp.exp2(jnp.where(srow < lo_r, ref - gc, NEG)))
        at = _dot_nt(jnp.concatenate(khat, axis=0).astype(BF16), jnp.concatenate(qhat, axis=0).astype(BF16))
        at_off = jnp.zeros((c, c), F32)
        for si in range(1, nsub):
            at_off = jnp.where(tcol // sub == si, at[(si - 1) * c:si * c, :], at_off)
        opart_ref[rows, :] = _dot_tn(at_off.astype(BF16), ivb)

        lo_rows = slice(ci * c, ci * c + c // 2)
        hi_rows = slice(ci * c + c // 2, (ci + 1) * c)
        q0, q1 = half_major(q)
        g0, g1 = half_major(gc)
        h0, h1 = half_major(hk_ref[rows, :])
        for s in range(sub // 2):
            key = h0[:, s:s + 1, :]
            d0 = q0 * jnp.exp2(jnp.where(r8 >= s, g0 - key, NEG))
            d1 = q1 * jnp.exp2(g1 - key)
            dcat_ref[lo_rows, s * dk:(s + 1) * dk] = d0.reshape(c // 2, dk).astype(BF16)
            dcat_ref[hi_rows, s * dk:(s + 1) * dk] = d1.reshape(c // 2, dk).astype(BF16)
        dcat_ref[lo_rows, (sub // 2) * dk:] = jnp.zeros((c // 2, (sub // 2) * dk), BF16)
        for s in range(sub // 2):
            key = h1[:, s:s + 1, :]
            d1 = q1 * jnp.exp2(jnp.where(r8 >= s, g1 - key, NEG))
            col = (sub // 2 + s) * dk
            dcat_ref[hi_rows, col:col + dk] = d1.reshape(c // 2, dk).astype(BF16)

    group = tril_ref.shape[0]
    for g0 in range(0, tt, group):
        grows = slice(g0, g0 + group)
        gates(grows)
        for ci in range(g0 // c, (g0 + group) // c):
            state_free_part(ci)
        ps_ref[grows, :] = _dot(dcat_ref[grows, :], sum_ref[...])

    glast = [gc_ref[(ci + 1) * c - 1:(ci + 1) * c, :] for ci in range(n_chunks)]
    glast += [jnp.zeros((1, dk), F32)] * (-n_chunks % 8)
    decay_cols = jnp.exp2(jnp.concatenate(glast, axis=0)).T
    st = st_ref[...]
    for ci in range(n_chunks):
        rows = slice(ci * c, (ci + 1) * c)
        opart_ref[rows, :] += _dot(qg_ref[rows, :], st.astype(BF16))
        st = decay_cols[:, ci:ci + 1] * st + u_ref[ci]
    st_ref[...] = st

    own_block = (lax.broadcasted_iota(jnp.int32, (c, dk), 1) // sub
                 == (lax.broadcasted_iota(jnp.int32, (c, dk), 0) // 8) % nsub)
    for ci in range(n_chunks):
        rows = slice(ci * c, (ci + 1) * c)
        ivb = i_ref[rows, :]
        a = jnp.where(own_block, ps_ref[rows, :], 0.0)
        od = _dot(a.astype(BF16), jnp.concatenate([ivb, jnp.zeros((dk - c, ivb.shape[1]), BF16)], axis=0))
        o = opart_ref[rows, :] + jnp.concatenate([od[(h * nsub + i) * 8:(h * nsub + i + 1) * 8, :]
                                                  for i in range(nsub) for h in range(2)], axis=0)
        o = o * lax.rsqrt(jnp.mean(o * o, axis=-1, keepdims=True) + EPS) * gn_ref[...]
        o_ref[rows, :] = (o * _silu(g_ref[rows, :].astype(F32))).astype(o_ref.dtype)


def hgrn2(u, lb_gamma, gnorm_w, *, layer, batch, seq, tt):
    h, dk, dv = HG_HEADS, HG_DK, HG_DV
    nt = seq // tt
    depth = lb_gamma.shape[0]
    r = np.arange(HG_GROUP * HG_CHUNK)
    tril = jnp.asarray((r[:, None] >= r[None, :]) & (r[:, None] // HG_CHUNK == r[None, :] // HG_CHUNK), BF16)
    key_offset = np.arange(HG_SUB * dk) // dk
    summer = jnp.asarray(key_offset[:, None] == np.arange(dk)[None, :] % HG_SUB, BF16)
    return pl.pallas_call(
        functools.partial(_hgrn_kernel, layer=layer),
        grid=(batch, h, nt),
        in_specs=[
            pl.BlockSpec((tt, 4 * dk), lambda b, hh, t: (b * nt + t, hh)),
            pl.BlockSpec((depth, dk), lambda b, hh, t: (0, hh)),
            pl.BlockSpec((1, dv), lambda b, hh, t: (0, hh)),
            pl.BlockSpec((HG_GROUP * HG_CHUNK, HG_GROUP * HG_CHUNK), lambda b, hh, t: (0, 0)),
            pl.BlockSpec((HG_SUB * dk, dk), lambda b, hh, t: (0, 0)),
        ],
        out_specs=pl.BlockSpec((tt, dv), lambda b, hh, t: (b * nt + t, hh)),
        out_shape=jax.ShapeDtypeStruct((batch * seq, h * dv), BF16),
        scratch_shapes=[
            pltpu.VMEM((dk, dv), F32),
            pltpu.VMEM((tt, dk), F32), pltpu.VMEM((tt, dk), F32), pltpu.VMEM((tt, dk), F32),
            pltpu.VMEM((tt, dk), BF16), pltpu.VMEM((tt // HG_CHUNK, dk, dv), F32),
            pltpu.VMEM((tt, HG_SUB * dk), BF16), pltpu.VMEM((tt, dk), F32), pltpu.VMEM((tt, dv), F32),
        ],
        compiler_params=_params("parallel", "parallel", "arbitrary"),
        name="hgrn2",
    )(u, lb_gamma, gnorm_w.reshape(1, h * dv), tril, summer)


QL = NSA_HPG * Q_BLOCK
FLAG_BITS = 16


def _tile_positions(first_tile, n_tiles=1):
    lane = lax.broadcasted_iota(jnp.int32, (1, n_tiles * QL), 1)
    return (first_tile + lane // QL) * Q_BLOCK + (lane & (Q_BLOCK - 1))


CMP_CHUNK = 128
SEL_CLASSES = 8
SEL_TILES = 4
KEY_LANES = 128
SLOPE_PARTS = 3
LOG2E = 1.4426950408889634


def _load_query(qa_ref, q_ref, sa_ref):
    dh = NSA_HEAD_DIM
    n_tiles = q_ref.shape[0] // Q_BLOCK
    qt = (q_ref[...].astype(F32) * (dh ** -0.5 * LOG2E)).T
    qa_ref[0:dh, :] = jnp.concatenate(
        [qt[h * dh:(h + 1) * dh, i * Q_BLOCK:(i + 1) * Q_BLOCK] for i in range(n_tiles) for h in range(NSA_HPG)],
        axis=1).astype(BF16)
    qa_ref[dh:, :] = jnp.concatenate([sa_ref[...]] * n_tiles, axis=1)


def _nsa_select_body(nch, t_min, q_ref, sa_ref, kc_ref, vct_ref, slope_ref, oc_ref, sel_ref, flag_ref,
                     qa_ref, s_ref, pb_ref, p_ref):
    ck = CMP_CHUNK
    nk = nch * ck
    ns = nk * CMP_STRIDE // SLC_BLOCK
    n_t = SEL_TILES
    lanes = n_t * QL
    first_tile = pl.program_id(2) * n_t
    _load_query(qa_ref, q_ref, sa_ref)
    t = _tile_positions(first_tile, n_t)
    slope2 = jnp.concatenate([slope_ref[...] * LOG2E] * n_t, axis=1)

    rows = []
    mc = None
    for ch in range(nch):
        rs = slice(ch * ck, (ch + 1) * ck)
        s = _dot(kc_ref[rs, :], qa_ref[...])
        if ((ch + 1) * ck - 1) * CMP_STRIDE + (CMP_BLOCK - 1) > t_min:
            end = (ch * ck + lax.broadcasted_iota(jnp.int32, (ck, 1), 0)) * CMP_STRIDE + (CMP_BLOCK - 1)
            s = jnp.where(t >= end, s, NEG)
        s_ref[rs, :] = s
        brow = slope2 * (ch * ck * CMP_STRIDE + (CMP_BLOCK - 1) - first_tile * Q_BLOCK).astype(F32)
        rows.append(brow)
        cm = jnp.max(s.reshape(ck // 8, 8, lanes), axis=0) + brow
        mc = cm if mc is None else jnp.maximum(mc, cm)
    m = jnp.max(mc, axis=0, keepdims=True)

    lsum = None
    for ch in range(nch):
        rs = slice(ch * ck, (ch + 1) * ck)
        p = jnp.exp2(s_ref[rs, :] - (m - rows[ch]))
        ls = jnp.sum(p.reshape(ck // 8, 8, lanes), axis=0)
        lsum = ls if lsum is None else lsum + ls
        s_ref[rs, :] = p
        pb_ref[rs, :] = p.astype(BF16)
    l = jnp.sum(lsum, axis=0, keepdims=True)
    inv = jnp.where(t >= CMP_BLOCK - 1, 1.0 / jnp.maximum(l, TINY), 0.0)
    oc = _dot(vct_ref[:, 0:nk], pb_ref[0:nk, :]) * inv
    for i in range(n_t):
        oc_ref[i] = oc[:, i * QL:(i + 1) * QL]

    for i in range(n_t):
        p_ref[i, 0:8, :] = jnp.zeros((8, Q_BLOCK), F32)
        p_ref[i, 8 + nk:16 + nk, :] = jnp.zeros((8, Q_BLOCK), F32)
    for ch in range(nch):
        pn = s_ref[ch * ck:(ch + 1) * ck, :] * inv
        for i in range(n_t):
            acc = pn[:, i * QL:i * QL + Q_BLOCK]
            for h in range(1, NSA_HPG):
                acc = acc + pn[:, i * QL + h * Q_BLOCK:i * QL + (h + 1) * Q_BLOCK]
            p_ref[i, 8 + ch * ck:8 + (ch + 1) * ck, :] = acc

    ratio = SLC_BLOCK // CMP_STRIDE
    imp = []
    for i in range(n_t):
        acc = p_ref[i, pl.ds(7, ns, stride=ratio), :]
        for r in range(1, ratio + 1):
            acc = acc + p_ref[i, pl.ds(7 + r, ns, stride=ratio), :]
        imp.append(acc)
    imp = jnp.concatenate(imp, axis=1)

    j = lax.broadcasted_iota(jnp.int32, (ns, 1), 0)
    cur = jnp.concatenate([t[:, i * QL:i * QL + Q_BLOCK] for i in range(n_t)], axis=1) // SLC_BLOCK
    forced = (j == 0) | (j == cur) | (j == cur - 1)
    imp = jnp.where(forced, NEG, jnp.where(j > cur, -1.0, imp))
    for _ in range(min(SLC_TOPK, ns) - 3):
        mx = jnp.max(imp, axis=0, keepdims=True)
        idx = jnp.min(jnp.where(imp == mx, j, ns), axis=0, keepdims=True)
        imp = jnp.where(j == idx, NEG, imp)
    sel = jnp.where((imp == NEG) & (j <= cur), 1.0, 0.0)
    ns_all, nw, nw_all = sel_ref.shape[1], ns // FLAG_BITS, flag_ref.shape[1]
    bit = jnp.left_shift(1, j & (FLAG_BITS - 1)).astype(F32)
    for i in range(n_t):
        sel_i = sel[:, i * Q_BLOCK:(i + 1) * Q_BLOCK]
        sel_ref[i, 0:ns, :] = sel_i
        if ns < ns_all:
            sel_ref[i, ns:, :] = jnp.zeros((ns_all - ns, Q_BLOCK), F32)
        chosen = jnp.max(sel_i, axis=1, keepdims=True)
        words = jnp.sum((chosen * bit).reshape(nw, FLAG_BITS, 1), axis=1)
        flag_ref[i, 0:nw, :] = jnp.broadcast_to(words, (nw, flag_ref.shape[2]))
        if nw < nw_all:
            flag_ref[i, nw:, :] = jnp.zeros((nw_all - nw, flag_ref.shape[2]), F32)


def _nsa_select_kernel(q_ref, sa_ref, kc_ref, *rest):
    n_chunks = kc_ref.shape[0] // CMP_CHUNK
    n_steps = kc_ref.shape[0] * CMP_STRIDE // (SEL_TILES * Q_BLOCK)
    n_cls = min(SEL_CLASSES, n_chunks)
    cls = pl.program_id(2) * n_cls // n_steps
    for c in range(n_cls):
        @pl.when(cls == c)
        def _(c=c):
            first_step = -(-c * n_steps // n_cls)
            _nsa_select_body((c + 1) * n_chunks // n_cls, first_step * SEL_TILES * Q_BLOCK,
                             q_ref, sa_ref, kc_ref, *rest)


def nsa_select(u, q_col, slope_rows, kc, vct, slopes):
    b, g, nc = kc.shape[:3]
    dh, ql, n_t = NSA_HEAD_DIM, QL, SEL_TILES
    nqb = u.shape[0] // (b * Q_BLOCK)
    n_steps = nqb // n_t
    ns = nc * CMP_STRIDE // SLC_BLOCK
    tiles = lambda *shape: pl.BlockSpec((None, None, n_t) + shape, lambda bi, gi, qi: (bi, gi, qi, 0, 0))
    return pl.pallas_call(
        _nsa_select_kernel,
        grid=(b, g, n_steps),
        in_specs=[
            pl.BlockSpec((n_t * Q_BLOCK, NSA_HPG * dh), lambda bi, gi, qi: (bi * n_steps + qi, q_col + gi)),
            pl.BlockSpec((None, KEY_LANES - dh, ql), lambda bi, gi, qi: (gi, 0, 0)),
            pl.BlockSpec((None, None, nc, KEY_LANES), lambda bi, gi, qi: (bi, gi, 0, 0)),
            pl.BlockSpec((None, None, dh, nc), lambda bi, gi, qi: (bi, gi, 0, 0)),
            pl.BlockSpec((None, 1, ql), lambda bi, gi, qi: (gi, 0, 0)),
        ],
        out_specs=[tiles(dh, ql), tiles(ns, Q_BLOCK), tiles(ns // FLAG_BITS, 128)],
        out_shape=[
            jax.ShapeDtypeStruct((b, g, nqb, dh, ql), F32),
            jax.ShapeDtypeStruct((b, g, nqb, ns, Q_BLOCK), F32),
            jax.ShapeDtypeStruct((b, g, nqb, ns // FLAG_BITS, 128), F32),
        ],
        scratch_shapes=[
            pltpu.VMEM((KEY_LANES, n_t * ql), BF16), pltpu.VMEM((nc, n_t * ql), F32),
            pltpu.VMEM((nc, n_t * ql), BF16), pltpu.VMEM((n_t, nc + 16, Q_BLOCK), F32),
        ],
        compiler_params=_params("parallel", "parallel", "parallel"),
        name="nsa_select",
    )(u, slope_rows, kc, vct, slopes)


ATT_GROUP = 8
LIST_PAD = 3 * ATT_GROUP
DIAG_BLOCKS = Q_BLOCK // SLC_BLOCK
WIN_BLOCKS = WINDOW // SLC_BLOCK


def _score_blocks(k_src, blocks, qa_ref, kcat_ref, s_ref):
    kb = SLC_BLOCK
    dh = k_src.shape[-1]
    for b, j in enumerate(blocks):
        kcat_ref[b * kb:(b + 1) * kb, 0:dh] = k_src[jnp.maximum(j, 0)]
    n = len(blocks) * kb
    s_ref[0:n, :] = _dot(kcat_ref[0:n, :], qa_ref[...])


def _write_position_columns(kcat_ref):
    n, dh = kcat_ref.shape[-2], NSA_HEAD_DIM
    offset = lax.broadcasted_iota(jnp.int32, (n, KEY_LANES - dh), 0) % SLC_BLOCK
    col = lax.broadcasted_iota(jnp.int32, (n, KEY_LANES - dh), 1)
    kcat_ref[:, dh:] = jnp.where(col < SLOPE_PARTS, offset, 0).astype(F32).astype(BF16)


def _softmax_step(v_src, blocks, masks, sel_rows, t, slope2, blk0, s_ref, vcat_ref, p_ref, m_ref, l_ref, acc_ref):
    nb = len(blocks)
    kb = SLC_BLOCK
    for b, j in enumerate(blocks):
        vcat_ref[b * kb:(b + 1) * kb, :] = v_src[jnp.maximum(j, 0)]
    rows = []
    mc = None
    for b, j in enumerate(blocks):
        brow = slope2 * ((j - blk0) * kb).astype(F32)
        if sel_rows[b] is not None:
            brow = jnp.where(sel_rows[b] > 0.0, brow, NEG)
        brow = jnp.where(j >= 0, brow, NEG)
        rows.append(brow)
        s = s_ref[b * kb:(b + 1) * kb, :]
        if masks[b] is not None:
            kpos = j * kb + lax.broadcasted_iota(jnp.int32, (kb, 1), 0)
            valid = (t >= kpos) if masks[b] == "causal" else (t - kpos < WINDOW)
            s = jnp.where(valid, s, NEG)
            s_ref[b * kb:(b + 1) * kb, :] = s
        cm = jnp.max(s.reshape(kb // 8, 8, QL), axis=0) + brow
        mc = cm if mc is None else jnp.maximum(mc, cm)
    m_old = m_ref[...]
    m_new = jnp.maximum(m_old, jnp.max(mc, axis=0, keepdims=True))
    alpha = jnp.exp2(m_old - m_new)
    m_ref[...] = m_new

    lsum = None
    for b in range(nb):
        p = jnp.exp2(s_ref[b * kb:(b + 1) * kb, :] - (m_new - rows[b]))
        ls = jnp.sum(p.reshape(kb // 8, 8, QL), axis=0)
        lsum = ls if lsum is None else lsum + ls
        p_ref[b * kb:(b + 1) * kb, :] = p.astype(BF16)
    l_ref[...] = alpha * l_ref[...] + lsum
    acc_ref[...] = alpha * acc_ref[...] + _dot_tn(vcat_ref[0:nb * kb, :], p_ref[0:nb * kb, :])


def _nsa_attend_kernel(fw_ref, q_ref, sa_ref, ks_ref, vs_ref, kw_ref, vw_ref, sel_ref, oc_ref, gate_ref, slope_ref,
                       o_ref, qa_ref, kcat_ref, s_ref, vcat_ref, p_ref, m_ref, l_ref, acc_ref,
                       kcat_w_ref, s_w_ref, vcat_w_ref, p_w_ref, m_w_ref, l_w_ref, acc_w_ref,
                       kcat_f_ref, s_f_ref, vcat_f_ref, p_f_ref, list_ref, *,
                       words_per_tile):
    bi, gi, qb = pl.program_id(0), pl.program_id(1), pl.program_id(2)
    tile_id = (bi * pl.num_programs(1) + gi) * pl.num_programs(2) + qb

    @pl.when(qb == 0)
    def _():
        for slot in range(2):
            _write_position_columns(kcat_ref.at[slot])
        _write_position_columns(kcat_w_ref)
        _write_position_columns(kcat_f_ref)

    _load_query(qa_ref, q_ref, sa_ref)
    t = _tile_positions(qb)
    slope2 = slope_ref[...] * LOG2E
    blk0 = qb * DIAG_BLOCKS
    diag = [blk0 + i for i in range(DIAG_BLOCKS)]
    past = [blk0 - WIN_BLOCKS + i for i in range(WIN_BLOCKS)]

    def sel_row(j):
        row = sel_ref[pl.ds(jnp.maximum(j, 0), 1), :]
        return jnp.concatenate([row] * NSA_HPG, axis=1)

    def reset(m, l, acc):
        m[...] = jnp.full_like(m, NEG)
        l[...] = jnp.zeros_like(l)
        acc[...] = jnp.zeros_like(acc)

    def result(l, acc):
        return acc[...] * (1.0 / jnp.maximum(jnp.sum(l[...], axis=0, keepdims=True), TINY))

    def scan_flags(w, n):
        base = w * FLAG_BITS
        word = fw_ref[tile_id * words_per_tile + w] & ((1 << jnp.minimum(blk0 - base, FLAG_BITS)) - 1)
        for i in range(FLAG_BITS):
            list_ref[n] = base + i
            n = n + ((word >> i) & 1)
        return n

    n_sel = lax.fori_loop(0, (blk0 + FLAG_BITS - 1) // FLAG_BITS, scan_flags, 0)
    for i in range(LIST_PAD):
        list_ref[n_sel + i] = -1

    def listed(i):
        return [list_ref[i * ATT_GROUP + b] for b in range(ATT_GROUP)]

    first = diag + listed(0)
    _score_blocks(kw_ref, past + diag, qa_ref, kcat_w_ref, s_w_ref)
    _score_blocks(ks_ref, first, qa_ref, kcat_f_ref, s_f_ref)
    _score_blocks(ks_ref, listed(1), qa_ref, kcat_ref.at[1], s_ref.at[1])

    slc_state = (m_ref, l_ref, acc_ref)
    reset(*slc_state)
    _softmax_step(vs_ref, first, ["causal"] * DIAG_BLOCKS + [None] * ATT_GROUP, [sel_row(j) for j in first],
                  t, slope2, blk0, s_f_ref, vcat_f_ref, p_f_ref, *slc_state)

    win_state = (m_w_ref, l_w_ref, acc_w_ref)
    reset(*win_state)
    masks = ["window"] * DIAG_BLOCKS + [None] * (WIN_BLOCKS - DIAG_BLOCKS) + ["causal"] * DIAG_BLOCKS
    _softmax_step(vw_ref, past + diag, masks, [None] * len(masks), t, slope2, blk0,
                  s_w_ref, vcat_w_ref, p_w_ref, *win_state)

    n_steps = jnp.maximum((n_sel + ATT_GROUP - 1) // ATT_GROUP - 1, 0)

    def slc_step(i, slot):
        _score_blocks(ks_ref, listed(i + 2), qa_ref, kcat_ref.at[slot], s_ref.at[slot])
        blocks = listed(i + 1)
        _softmax_step(vs_ref, blocks, [None] * ATT_GROUP, [sel_row(j) for j in blocks], t, slope2, blk0,
                      s_ref.at[1 - slot], vcat_ref, p_ref, *slc_state)

    def slc_body(i2, carry):
        slc_step(2 * i2, 0)

        @pl.when(2 * i2 + 1 < n_steps)
        def _():
            slc_step(2 * i2 + 1, 1)

        return carry

    lax.fori_loop(0, (n_steps + 1) // 2, slc_body, 0)
    o_slc = result(l_ref, acc_ref)
    o_win = result(l_w_ref, acc_w_ref)

    gate = _sigmoid(gate_ref[...])
    o = gate[0:1, :] * oc_ref[...] + gate[1:2, :] * o_slc + gate[2:3, :] * o_win
    o = jnp.concatenate([o[:, h * Q_BLOCK:(h + 1) * Q_BLOCK] for h in range(NSA_HPG)], axis=0)
    o_ref[...] = o.T.astype(o_ref.dtype)


def nsa_attend(flag_words, u, q_col, slope_rows, ks, vs, kw, vw, sel, oc, gates, slopes):
    b, g, ns = ks.shape[:3]
    dh, ql = NSA_HEAD_DIM, QL
    nqb = u.shape[0] // (b * Q_BLOCK)
    gkeys = ATT_GROUP * SLC_BLOCK
    wkeys = (WIN_BLOCKS + DIAG_BLOCKS) * SLC_BLOCK
    fkeys = (DIAG_BLOCKS + ATT_GROUP) * SLC_BLOCK
    tile = lambda *shape: pl.BlockSpec((None, None, None) + shape, lambda bi, gi, qi, fw: (bi, gi, qi, 0, 0))
    seq = lambda *shape: pl.BlockSpec((None, None) + shape, lambda bi, gi, qi, fw: (bi, gi, 0, 0, 0))
    state = [pltpu.VMEM((1, ql), F32), pltpu.VMEM((8, ql), F32), pltpu.VMEM((dh, ql), F32)]
    grid_spec = pltpu.PrefetchScalarGridSpec(
        num_scalar_prefetch=1,
        grid=(b, g, nqb),
        in_specs=[
            pl.BlockSpec((Q_BLOCK, NSA_HPG * dh), lambda bi, gi, qi, fw: (bi * nqb + qi, q_col + gi)),
            pl.BlockSpec((None, KEY_LANES - dh, ql), lambda bi, gi, qi, fw: (gi, 0, 0)),
            seq(ns, SLC_BLOCK, dh), seq(ns, SLC_BLOCK, dh),
            seq(ns, SLC_BLOCK, dh), seq(ns, SLC_BLOCK, dh),
            tile(ns, Q_BLOCK), tile(dh, ql), tile(3, ql),
            pl.BlockSpec((None, 1, ql), lambda bi, gi, qi, fw: (gi, 0, 0)),
        ],
        out_specs=pl.BlockSpec((Q_BLOCK, NSA_HPG * dh), lambda bi, gi, qi, fw: (bi * nqb + qi, gi)),
        scratch_shapes=[
            pltpu.VMEM((KEY_LANES, ql), BF16),
            pltpu.VMEM((2, gkeys, KEY_LANES), BF16), pltpu.VMEM((2, gkeys, ql), F32),
            pltpu.VMEM((gkeys, dh), BF16), pltpu.VMEM((gkeys, ql), BF16), *state,
            pltpu.VMEM((wkeys, KEY_LANES), BF16), pltpu.VMEM((wkeys, ql), F32),
            pltpu.VMEM((wkeys, dh), BF16), pltpu.VMEM((wkeys, ql), BF16), *state,
            pltpu.VMEM((fkeys, KEY_LANES), BF16), pltpu.VMEM((fkeys, ql), F32),
            pltpu.VMEM((fkeys, dh), BF16), pltpu.VMEM((fkeys, ql), BF16),
            pltpu.SMEM((ns + LIST_PAD,), jnp.int32),
        ],
    )
    return pl.pallas_call(
        functools.partial(_nsa_attend_kernel, words_per_tile=ns // FLAG_BITS),
        grid_spec=grid_spec,
        out_shape=jax.ShapeDtypeStruct((b * nqb * Q_BLOCK, g * NSA_HPG * dh), BF16),
        compiler_params=_params("parallel", "parallel", "arbitrary"),
        name="nsa_attend",
    )(flag_words, u, slope_rows, ks, vs, kw, vw, sel, oc, gates, slopes)


def nsa_mixer(u, q_col, kc, vc, ks, vs, kw, vw, gate_logits, cmp_pos, cmp_w1, cmp_b1, cmp_w2, cmp_b2, *,
              batch, seq):
    g, hpg, dh = NSA_KV_GROUPS, NSA_HPG, NSA_HEAD_DIM
    nqb, ns, nc = seq // Q_BLOCK, seq // SLC_BLOCK, seq // CMP_STRIDE

    gates = gate_logits.astype(F32).reshape(batch, nqb, Q_BLOCK, 3, g, hpg).transpose(0, 4, 1, 3, 5, 2)
    gates = gates.reshape(batch, g, nqb, 3, QL)
    slopes = 2.0 ** (-8.0 * jnp.arange(1, NSA_HEADS + 1, dtype=F32) / NSA_HEADS)
    slopes = jnp.repeat(slopes.reshape(g, 1, hpg), Q_BLOCK, axis=2)

    def halves(x):
        x = x.reshape(batch, nc, CMP_STRIDE, g, dh).transpose(0, 3, 1, 2, 4)
        return x.reshape(batch * g, nc, CMP_STRIDE * dh)

    def blocks(x):
        return x.reshape(batch, ns, SLC_BLOCK, g, dh).transpose(0, 3, 1, 2, 4)

    s2 = slopes * LOG2E
    parts = []
    for _ in range(SLOPE_PARTS):
        part = s2.astype(BF16)
        parts.append(part)
        s2 = s2 - part.astype(F32)
    slope_rows = jnp.concatenate(parts + [jnp.zeros((g, KEY_LANES - dh - SLOPE_PARTS, QL), BF16)], axis=1)

    cmp = compress_tokens(jnp.stack([halves(kc), halves(vc)]), cmp_pos, cmp_w1, cmp_b1, cmp_w2, cmp_b2)
    cmp = cmp.astype(BF16).reshape(2, batch, g, nc, dh)
    r = (jnp.arange(nc) % CMP_CHUNK * CMP_STRIDE).astype(BF16).reshape(1, 1, nc, 1)
    k_cmp = jnp.concatenate([cmp[0], jnp.broadcast_to(r, (batch, g, nc, SLOPE_PARTS)),
                             jnp.zeros((batch, g, nc, KEY_LANES - dh - SLOPE_PARTS), BF16)], axis=-1)
    v_cmp_t = cmp[1].transpose(0, 1, 3, 2)

    oc, sel, flags = nsa_select(u, q_col, slope_rows, k_cmp, v_cmp_t, slopes)
    flag_words = flags[:, :, :, :, 0].astype(jnp.int32).reshape(-1)
    return nsa_attend(flag_words, u, q_col, slope_rows, blocks(ks), blocks(vs), blocks(kw), blocks(vw),
                      sel, oc, gates, slopes)


ROW_TILE = 1024
COL_TILE = 1024
OUT_ROW_TILE = 512
FFN_ROW_TILE = 1024
FFN_HID_TILE = 512
CONV_SEQ_TILE = 512
HGRN_SEQ_TILE = 1024


def _even_layer(x, nw, w_in, conv_w, conv_b, ln_w, ln_b, cmp_pos, cmp_w1, cmp_b1, cmp_w2, cmp_b2, w_out, *,
                batch, seq):
    c = conv_w.shape[-1]
    nq = NSA_HEADS * NSA_HEAD_DIM
    nkv = NSA_KV_GROUPS * NSA_HEAD_DIM
    n_in = w_in.shape[1]
    n_pad = -(-n_in // COL_TILE) * COL_TILE
    w_in = jnp.pad(w_in, ((0, 0), (0, n_pad - n_in))).astype(BF16)
    u = norm_matmul(x, nw, w_in, tm=ROW_TILE, tn=COL_TILE)
    a_out = conformer_conv(u, conv_w.reshape(CONV_WIDTH, c), conv_b, ln_w, ln_b, seq=seq, ts=CONV_SEQ_TILE)
    off = 2 * c
    q_col = off // (NSA_HPG * NSA_HEAD_DIM)
    off += nq
    kvs = [u[:, off + i * nkv:off + (i + 1) * nkv] for i in range(6)]
    off += 6 * nkv
    gate_logits = u[:, off:off + 3 * NSA_HEADS]
    b_out = nsa_mixer(u, q_col, *kvs, gate_logits, cmp_pos, cmp_w1, cmp_b1, cmp_w2, cmp_b2,
                      batch=batch, seq=seq)
    return matmul_residual(a_out, b_out, 0, 0, w_out.astype(BF16), x, tm=OUT_ROW_TILE, tn=w_out.shape[1])


def _odd_layer(x, nw, w_in, lb_gamma, gnorm_w, w_out, *, layer, batch, seq):
    d = w_in.shape[0]
    w_in = w_in.reshape(d, 4, HG_HEADS, HG_DK).transpose(0, 2, 1, 3).reshape(d, 4 * HG_HEADS * HG_DK)
    u = norm_matmul(x, nw, w_in.astype(BF16), tm=ROW_TILE, tn=COL_TILE)
    o = hgrn2(u, lb_gamma, gnorm_w, layer=layer, batch=batch, seq=seq, tt=HGRN_SEQ_TILE)
    return matmul_residual(o, o, 0, 1, w_out.astype(BF16), x, tm=OUT_ROW_TILE, tn=w_out.shape[1])


def kernel(x, norm_w, final_norm_w, ev_w_in, ev_conv_w, ev_conv_b, ev_conv_ln_w, ev_conv_ln_b, ev_cmp_pos,
           ev_cmp_w1, ev_cmp_b1, ev_cmp_w2, ev_cmp_b2, ev_w_out, od_w_in, od_lb_gamma, od_gnorm_w, od_w_out,
           ffn_w_gu, ffn_w_down):
    batch, seq, d = x.shape
    depth = norm_w.shape[0]
    xs = x.reshape(batch * seq, d)
    for layer in range(depth):
        i = layer // 2
        if layer % 2 == 0:
            xs = _even_layer(xs, norm_w[layer, 0], ev_w_in[i], ev_conv_w[i], ev_conv_b[i], ev_conv_ln_w[i],
                             ev_conv_ln_b[i], ev_cmp_pos[i], ev_cmp_w1[i], ev_cmp_b1[i], ev_cmp_w2[i],
                             ev_cmp_b2[i], ev_w_out[i], batch=batch, seq=seq)
        else:
            xs = _odd_layer(xs, norm_w[layer, 0], od_w_in[i], od_lb_gamma.astype(F32), od_gnorm_w[i],
                            od_w_out[i], layer=layer, batch=batch, seq=seq)
        xs = ffn_block(xs, norm_w[layer, 1], ffn_w_gu[layer].astype(BF16), ffn_w_down[layer].astype(BF16),
                       final_norm_w, tm=FFN_ROW_TILE, th=FFN_HID_TILE, final_norm=layer == depth - 1)
    return xs.reshape(batch, seq, d)
```

```python
import functools

import jax
import jax.numpy as jnp
import numpy as np
from jax import lax
from jax.experimental import pallas as pl
from jax.experimental.pallas import tpu as pltpu

F32 = jnp.float32
BF16 = jnp.bfloat16

EPS = 1e-6
TINY = 1e-30
NEG = -1e30

VMEM_LIMIT_BYTES = 56 * 1024 * 1024

CONV_WIDTH = 31
NSA_HEADS = 16
NSA_HEAD_DIM = 64
NSA_KV_GROUPS = 4
NSA_HPG = NSA_HEADS // NSA_KV_GROUPS
CMP_STRIDE = 16
CMP_BLOCK = 32
SLC_BLOCK = 64
SLC_TOPK = 16
WINDOW = 512
Q_BLOCK = 128
HG_HEADS = 16
HG_DK = 128
HG_DV = 128
HG_CHUNK = 64
HG_SUB = 16
HG_GROUP = 4


def _params(*sem):
    return pltpu.CompilerParams(dimension_semantics=sem, vmem_limit_bytes=VMEM_LIMIT_BYTES)


def _sigmoid(x):
    return 1.0 / (1.0 + jnp.exp(-x))


def _silu(x):
    return x * _sigmoid(x)


def _dot(a, b):
    return jnp.dot(a, b, preferred_element_type=F32)


def _dot_nt(a, b):
    return lax.dot_general(a, b, (((1,), (1,)), ((), ())), preferred_element_type=F32)


def _dot_tn(a, b):
    return lax.dot_general(a, b, (((0,), (0,)), ((), ())), preferred_element_type=F32)


def _norm_matmul_kernel(x_ref, nw_ref, w_ref, o_ref, h_ref):
    @pl.when(pl.program_id(1) == 0)
    def _():
        x = x_ref[...]
        ms = jnp.mean(x * x, axis=-1, keepdims=True)
        h_ref[...] = (x * lax.rsqrt(ms + EPS) * nw_ref[...]).astype(BF16)

    o_ref[...] = _dot(h_ref[...], w_ref[...]).astype(o_ref.dtype)


def norm_matmul(x, nw, w, *, tm, tn, out_dtype=BF16):
    m, k = x.shape
    n = w.shape[1]
    return pl.pallas_call(
        _norm_matmul_kernel,
        grid=(m // tm, n // tn),
        in_specs=[
            pl.BlockSpec((tm, k), lambda i, j: (i, 0)),
            pl.BlockSpec((1, k), lambda i, j: (0, 0)),
            pl.BlockSpec((k, tn), lambda i, j: (0, j)),
        ],
        out_specs=pl.BlockSpec((tm, tn), lambda i, j: (i, j)),
        out_shape=jax.ShapeDtypeStruct((m, n), out_dtype),
        scratch_shapes=[pltpu.VMEM((tm, k), BF16)],
        compiler_params=_params("parallel", "arbitrary"),
        name="norm_matmul",
    )(x, nw.reshape(1, k), w)


def _norm_matmul_resident_kernel(x_ref, nw_ref, w_ref, o_ref):
    x = x_ref[...]
    ms = jnp.mean(x * x, axis=-1, keepdims=True)
    h = (x * lax.rsqrt(ms + EPS) * nw_ref[...]).astype(BF16)
    o_ref[...] = _dot(h, w_ref[...]).astype(o_ref.dtype)


def norm_matmul_resident(x, nw, w, *, tm, out_dtype=BF16):
    m, k = x.shape
    n = w.shape[1]
    return pl.pallas_call(
        _norm_matmul_resident_kernel,
        grid=(m // tm,),
        in_specs=[
            pl.BlockSpec((tm, k), lambda i: (i, 0)),
            pl.BlockSpec((1, k), lambda i: (0, 0)),
            pl.BlockSpec((k, n), lambda i: (0, 0), pipeline_mode=pl.Buffered(1)),
        ],
        out_specs=pl.BlockSpec((tm, n), lambda i: (i, 0)),
        out_shape=jax.ShapeDtypeStruct((m, n), out_dtype),
        compiler_params=_params("parallel"),
        name="norm_matmul_resident",
    )(x, nw.reshape(1, k), w)


def _matmul_res_kernel(a1_ref, a2_ref, w1_ref, w2_ref, r_ref, o_ref):
    acc = _dot(a1_ref[...], w1_ref[...]) + _dot(a2_ref[...], w2_ref[...])
    o_ref[...] = r_ref[...] + acc


def matmul_residual(a1, a2, blk1, blk2, w, res, *, tm, tn):
    m = res.shape[0]
    k, n = w.shape
    kh = k // 2
    return pl.pallas_call(
        _matmul_res_kernel,
        grid=(m // tm, n // tn),
        in_specs=[
            pl.BlockSpec((tm, kh), lambda i, j: (i, blk1)),
            pl.BlockSpec((tm, kh), lambda i, j: (i, blk2)),
            pl.BlockSpec((kh, tn), lambda i, j: (0, j)),
            pl.BlockSpec((kh, tn), lambda i, j: (1, j)),
            pl.BlockSpec((tm, tn), lambda i, j: (i, j)),
        ],
        out_specs=pl.BlockSpec((tm, tn), lambda i, j: (i, j)),
        out_shape=jax.ShapeDtypeStruct((m, n), F32),
        compiler_params=_params("parallel", "arbitrary"),
        name="matmul_residual",
    )(a1, a2, w, w, res)


def _ffn_kernel(x_ref, nw_ref, wg_ref, wu_ref, wd_ref, fw_ref, o_ref, h_ref, *, final_norm):
    j = pl.program_id(1)

    @pl.when(j == 0)
    def _():
        x = x_ref[...]
        ms = jnp.mean(x * x, axis=-1, keepdims=True)
        h_ref[...] = (x * lax.rsqrt(ms + EPS) * nw_ref[...]).astype(BF16)
        o_ref[...] = x

    h = h_ref[...]
    a = _dot(h, wg_ref[...])
    b = _dot(h, wu_ref[...])
    z = (_silu(a) * b).astype(BF16)
    o_ref[...] += _dot(z, wd_ref[...])

    if final_norm:
        @pl.when(j == pl.num_programs(1) - 1)
        def _():
            y = o_ref[...]
            ms = jnp.mean(y * y, axis=-1, keepdims=True)
            o_ref[...] = y * lax.rsqrt(ms + EPS) * fw_ref[...]


def ffn_block(x, nw, w_gu, w_down, final_w, *, tm, th, final_norm):
    m, d = x.shape
    hid = w_down.shape[0]
    nh = hid // th
    return pl.pallas_call(
        functools.partial(_ffn_kernel, final_norm=final_norm),
        grid=(m // tm, nh),
        in_specs=[
            pl.BlockSpec((tm, d), lambda i, j: (i, 0)),
            pl.BlockSpec((1, d), lambda i, j: (0, 0)),
            pl.BlockSpec((d, th), lambda i, j: (0, j)),
            pl.BlockSpec((d, th), lambda i, j: (0, j + nh)),
            pl.BlockSpec((th, d), lambda i, j: (j, 0)),
            pl.BlockSpec((1, d), lambda i, j: (0, 0)),
        ],
        out_specs=pl.BlockSpec((tm, d), lambda i, j: (i, 0)),
        out_shape=jax.ShapeDtypeStruct((m, d), F32),
        scratch_shapes=[pltpu.VMEM((tm, d), BF16)],
        compiler_params=_params("parallel", "arbitrary"),
        name="ffn_block",
    )(x, nw.reshape(1, d), w_gu, w_gu, w_down, final_w.reshape(1, d))


CONV_HALO = 32
CONV_ROWS = 32


def _conv_kernel(a_ref, g_ref, ah_ref, gh_ref, cw_ref, cb_ref, lw_ref, lb_ref, o_ref, hs_ref, sh_ref, cv_ref, *,
                 tiles_per_seq):
    ts = a_ref.shape[0]
    first = (pl.program_id(0) % tiles_per_seq) == 0
    hprev = ah_ref[...].astype(F32) * _sigmoid(gh_ref[...].astype(F32))
    hs_ref[0:CONV_HALO, :] = jnp.where(first, 0.0, hprev)
    hs_ref[CONV_HALO:CONV_HALO + ts, :] = a_ref[...].astype(F32) * _sigmoid(g_ref[...].astype(F32))
    off = CONV_HALO - (CONV_WIDTH - 1)
    n_shifted = sh_ref.shape[1]
    for r in range(1, 8):
        sh_ref[r - 1] = hs_ref[r:r + n_shifted, :]

    def body(i, carry):
        base = pl.multiple_of(i * CONV_ROWS, CONV_ROWS)
        acc = jnp.zeros((CONV_ROWS // 8, 8, hs_ref.shape[1]), F32) + cb_ref[...]
        for w in range(CONV_WIDTH):
            r, a = (off + w) % 8, (off + w) // 8 * 8
            src = hs_ref if r == 0 else sh_ref.at[r - 1]
            rows = src[pl.ds(base + a, CONV_ROWS), :].reshape(CONV_ROWS // 8, 8, -1)
            acc = acc + rows * cw_ref[w * 8:(w + 1) * 8, :]
        cv_ref[pl.ds(base, CONV_ROWS), :] = acc.reshape(CONV_ROWS, -1)
        return carry

    lax.fori_loop(0, ts // CONV_ROWS, body, 0)

    y = cv_ref[...]
    mu = jnp.mean(y, axis=-1, keepdims=True)
    d = y - mu
    var = jnp.mean(d * d, axis=-1, keepdims=True)
    hn = d * lax.rsqrt(var + EPS) * lw_ref[...] + lb_ref[...]
    o_ref[...] = _silu(hn).astype(o_ref.dtype)


def conformer_conv(u, conv_w, conv_b, ln_w, ln_b, *, seq, ts):
    t = u.shape[0]
    c = conv_w.shape[1]
    hb = ts // CONV_HALO
    return pl.pallas_call(
        functools.partial(_conv_kernel, tiles_per_seq=seq // ts),
        grid=(t // ts,),
        in_specs=[
            pl.BlockSpec((ts, c), lambda i: (i, 0)),
            pl.BlockSpec((ts, c), lambda i: (i, 1)),
            pl.BlockSpec((CONV_HALO, c), lambda i: (jnp.maximum(i * hb - 1, 0), 0)),
            pl.BlockSpec((CONV_HALO, c), lambda i: (jnp.maximum(i * hb - 1, 0), 1)),
            pl.BlockSpec((CONV_WIDTH * 8, c), lambda i: (0, 0)),
            pl.BlockSpec((1, c), lambda i: (0, 0)),
            pl.BlockSpec((1, c), lambda i: (0, 0)),
            pl.BlockSpec((1, c), lambda i: (0, 0)),
        ],
        out_specs=pl.BlockSpec((ts, c), lambda i: (i, 0)),
        out_shape=jax.ShapeDtypeStruct((t, c), BF16),
        scratch_shapes=[pltpu.VMEM((CONV_HALO + ts, c), F32), pltpu.VMEM((7, CONV_HALO + ts - 8, c), F32),
                        pltpu.VMEM((ts, c), F32)],
        compiler_params=_params("parallel"),
        name="conformer_conv",
    )(u, u, u, u, jnp.repeat(conv_w, 8, axis=0), conv_b.reshape(1, c), ln_w.reshape(1, c), ln_b.reshape(1, c))


def _compress_kernel(x_ref, pos_ref, w1_ref, b1_ref, w2_ref, b2_ref, o_ref, sh_ref):
    n = x_ref.shape[0]
    hw = x_ref.shape[1]
    x = x_ref[...].astype(F32)
    xa = (x + pos_ref[:, 0:hw]).astype(BF16)
    xb = (x + pos_ref[:, hw:2 * hw]).astype(BF16)
    p1 = _dot(xa, w1_ref[0:hw, :])
    sh_ref[0:n, :] = _dot(xb, w1_ref[hw:2 * hw, :])
    sh_ref[n:n + 8, :] = jnp.zeros((8, sh_ref.shape[1]), F32)
    hid = p1 + sh_ref[1:n + 1, :] + b1_ref[...]
    o_ref[...] = _dot(_silu(hid).astype(BF16), w2_ref[...]) + b2_ref[...]


def compress_tokens(xh, pos, w1, b1, w2, b2):
    _, bg, n, hw = xh.shape
    hid = w1.shape[-1]
    dh = w2.shape[-1]
    return pl.pallas_call(
        _compress_kernel,
        grid=(2, bg),
        in_specs=[
            pl.BlockSpec((None, None, n, hw), lambda s, i: (s, i, 0, 0)),
            pl.BlockSpec((None, 1, 2 * hw), lambda s, i: (s, 0, 0)),
            pl.BlockSpec((None, 2 * hw, hid), lambda s, i: (s, 0, 0)),
            pl.BlockSpec((None, 1, hid), lambda s, i: (s, 0, 0)),
            pl.BlockSpec((None, hid, dh), lambda s, i: (s, 0, 0)),
            pl.BlockSpec((None, 1, dh), lambda s, i: (s, 0, 0)),
        ],
        out_specs=pl.BlockSpec((None, None, n, dh), lambda s, i: (s, i, 0, 0)),
        out_shape=jax.ShapeDtypeStruct((2, bg, n, dh), F32),
        scratch_shapes=[pltpu.VMEM((n + 8, hid), F32)],
        compiler_params=_params("parallel", "parallel"),
        name="compress_tokens",
    )(xh, pos.reshape(2, 1, 2 * hw), w1.astype(BF16), b1.reshape(2, 1, hid), w2.astype(BF16),
      b2.reshape(2, 1, dh))


def _split_bf16(x):
    hi = x.astype(BF16)
    lo = (x - hi.astype(F32)).astype(BF16)
    return hi, lo


def _hgrn_kernel(x_ref, gamma_ref, gn_ref, tril_ref, sum_ref, o_ref,
                 st_ref, gc_ref, k_ref, hk_ref, qg_ref, u_ref, dcat_ref, ps_ref, opart_ref, *, layer):
    c, sub = HG_CHUNK, HG_SUB
    nsub = c // sub
    tt, dk = o_ref.shape
    n_chunks = tt // c
    q_ref, f_ref, i_ref, g_ref = (x_ref.at[:, i * dk:(i + 1) * dk] for i in range(4))

    @pl.when(pl.program_id(2) == 0)
    def _():
        st_ref[...] = jnp.zeros_like(st_ref)

    gamma = gamma_ref[...]
    e = jnp.exp(gamma - jnp.max(gamma, axis=0, keepdims=True))
    sm = e / jnp.sum(e, axis=0, keepdims=True)
    lb = jnp.sum(sm[0:layer + 1, :], axis=0, keepdims=True) - sm[0:1, :]

    def gates(rows):
        f = lb + (1.0 - lb) * _sigmoid(f_ref[rows, :].astype(F32))
        k = 1.0 - f
        hi, lo = _split_bf16(jnp.log2(jnp.maximum(f, TINY)))
        gc = _dot(tril_ref[...], hi) + _dot(tril_ref[...], lo)
        gc_ref[rows, :] = gc
        k_ref[rows, :] = k
        hk_ref[rows, :] = gc - jnp.log2(k)
        qg_ref[rows, :] = (q_ref[rows, :].astype(F32) * jnp.exp2(gc)).astype(BF16)

    r8 = lax.broadcasted_iota(jnp.int32, (1, 8, 1), 1)
    srow = lax.broadcasted_iota(jnp.int32, (c, 1), 0)
    tcol = lax.broadcasted_iota(jnp.int32, (c, c), 1)

    def half_major(x):
        return [jnp.concatenate([x[(2 * i + h) * 8:(2 * i + h + 1) * 8, :] for i in range(nsub)], axis=0)
                .reshape(nsub, 8, -1) for h in range(2)]

    def state_free_part(ci):
        rows = slice(ci * c, (ci + 1) * c)
        q = q_ref[rows, :].astype(F32)
        gc = gc_ref[rows, :]
        k = k_ref[rows, :]
        ivb = i_ref[rows, :]
        glast = gc[c - 1:c, :]
        u_ref[ci] = _dot_tn((k * jnp.exp2(glast - gc)).astype(BF16), ivb)

        qhat = [jnp.zeros((sub, dk), F32)]
        khat = []
        for si in range(1, nsub):
            lo_r, hi_r = si * sub, (si + 1) * sub
            ref = gc[lo_r - 1:lo_r, :]
            qhat.append(q[lo_r:hi_r, :] * jnp.exp2(gc[lo_r:hi_r, :] - ref))
            khat.append(k * jnp.exp2(jnp.where(srow < lo_r, ref - gc, NEG)))
        at = _dot_nt(jnp.concatenate(khat, axis=0).astype(BF16), jnp.concatenate(qhat, axis=0).astype(BF16))
        at_off = jnp.zeros((c, c), F32)
        for si in range(1, nsub):
            at_off = jnp.where(tcol // sub == si, at[(si - 1) * c:si * c, :], at_off)
        opart_ref[rows, :] = _dot_tn(at_off.astype(BF16), ivb)

        lo_rows = slice(ci * c, ci * c + c // 2)
        hi_rows = slice(ci * c + c // 2, (ci + 1) * c)
        q0, q1 = half_major(q)
        g0, g1 = half_major(gc)
        h0, h1 = half_major(hk_ref[rows, :])
        for s in range(sub // 2):
            key = h0[:, s:s + 1, :]
            d0 = q0 * jnp.exp2(jnp.where(r8 >= s, g0 - key, NEG))
            d1 = q1 * jnp.exp2(g1 - key)
            dcat_ref[lo_rows, s * dk:(s + 1) * dk] = d0.reshape(c // 2, dk).astype(BF16)
            dcat_ref[hi_rows, s * dk:(s + 1) * dk] = d1.reshape(c // 2, dk).astype(BF16)
        dcat_ref[lo_rows, (sub // 2) * dk:] = jnp.zeros((c // 2, (sub // 2) * dk), BF16)
        for s in range(sub // 2):
            key = h1[:, s:s + 1, :]
            d1 = q1 * jnp.exp2(jnp.where(r8 >= s, g1 - key, NEG))
            col = (sub // 2 + s) * dk
            dcat_ref[hi_rows, col:col + dk] = d1.reshape(c // 2, dk).astype(BF16)

    group = tril_ref.shape[0]
    for g0 in range(0, tt, group):
        grows = slice(g0, g0 + group)
        gates(grows)
        for ci in range(g0 // c, (g0 + group) // c):
            state_free_part(ci)
        ps_ref[grows, :] = _dot(dcat_ref[grows, :], sum_ref[...])

    glast = [gc_ref[(ci + 1) * c - 1:(ci + 1) * c, :] for ci in range(n_chunks)]
    glast += [jnp.zeros((1, dk), F32)] * (-n_chunks % 8)
    decay_cols = jnp.exp2(jnp.concatenate(glast, axis=0)).T
    st = st_ref[...]
    for ci in range(n_chunks):
        rows = slice(ci * c, (ci + 1) * c)
        opart_ref[rows, :] += _dot(qg_ref[rows, :], st.astype(BF16))
        st = decay_cols[:, ci:ci + 1] * st + u_ref[ci]
    st_ref[...] = st

    own_block = (lax.broadcasted_iota(jnp.int32, (c, dk), 1) // sub
                 == (lax.broadcasted_iota(jnp.int32, (c, dk), 0) // 8) % nsub)
    for ci in range(n_chunks):
        rows = slice(ci * c, (ci + 1) * c)
        ivb = i_ref[rows, :]
        a = jnp.where(own_block, ps_ref[rows, :], 0.0)
        od = _dot(a.astype(BF16), jnp.concatenate([ivb, jnp.zeros((dk - c, ivb.shape[1]), BF16)], axis=0))
        o = opart_ref[rows, :] + jnp.concatenate([od[(h * nsub + i) * 8:(h * nsub + i + 1) * 8, :]
                                                  for i in range(nsub) for h in range(2)], axis=0)
        o = o * lax.rsqrt(jnp.mean(o * o, axis=-1, keepdims=True) + EPS) * gn_ref[...]
        o_ref[rows, :] = (o * _silu(g_ref[rows, :].astype(F32))).astype(o_ref.dtype)


def hgrn2(u, lb_gamma, gnorm_w, *, layer, batch, seq, tt):
    h, dk, dv = HG_HEADS, HG_DK, HG_DV
    nt = seq // tt
    depth = lb_gamma.shape[0]
    r = np.arange(HG_GROUP * HG_CHUNK)
    tril = jnp.asarray((r[:, None] >= r[None, :]) & (r[:, None] // HG_CHUNK == r[None, :] // HG_CHUNK), BF16)
    key_offset = np.arange(HG_SUB * dk) // dk
    summer = jnp.asarray(key_offset[:, None] == np.arange(dk)[None, :] % HG_SUB, BF16)
    return pl.pallas_call(
        functools.partial(_hgrn_kernel, layer=layer),
        grid=(batch, h, nt),
        in_specs=[
            pl.BlockSpec((tt, 4 * dk), lambda b, hh, t: (b * nt + t, hh)),
            pl.BlockSpec((depth, dk), lambda b, hh, t: (0, hh)),
            pl.BlockSpec((1, dv), lambda b, hh, t: (0, hh)),
            pl.BlockSpec((HG_GROUP * HG_CHUNK, HG_GROUP * HG_CHUNK), lambda b, hh, t: (0, 0)),
            pl.BlockSpec((HG_SUB * dk, dk), lambda b, hh, t: (0, 0)),
        ],
        out_specs=pl.BlockSpec((tt, dv), lambda b, hh, t: (b * nt + t, hh)),
        out_shape=jax.ShapeDtypeStruct((batch * seq, h * dv), BF16),
        scratch_shapes=[
            pltpu.VMEM((dk, dv), F32),
            pltpu.VMEM((tt, dk), F32), pltpu.VMEM((tt, dk), F32), pltpu.VMEM((tt, dk), F32),
            pltpu.VMEM((tt, dk), BF16), pltpu.VMEM((tt // HG_CHUNK, dk, dv), F32),
            pltpu.VMEM((tt, HG_SUB * dk), BF16), pltpu.VMEM((tt, dk), F32), pltpu.VMEM((tt, dv), F32),
        ],
        compiler_params=_params("parallel", "parallel", "arbitrary"),
        name="hgrn2",
    )(u, lb_gamma, gnorm_w.reshape(1, h * dv), tril, summer)


QL = NSA_HPG * Q_BLOCK
FLAG_BITS = 16


def _tile_positions(first_tile, n_tiles=1):
    lane = lax.broadcasted_iota(jnp.int32, (1, n_tiles * QL), 1)
    return (first_tile + lane // QL) * Q_BLOCK + (lane & (Q_BLOCK - 1))


CMP_CHUNK = 128
SEL_CLASSES = 8
SEL_TILES = 4
KEY_LANES = 128
SLOPE_PARTS = 3
LOG2E = 1.4426950408889634


def _load_query(qa_ref, q_ref, sa_ref):
    dh = NSA_HEAD_DIM
    n_tiles = q_ref.shape[0] // Q_BLOCK
    qt = (q_ref[...].astype(F32) * (dh ** -0.5 * LOG2E)).T
    qa_ref[0:dh, :] = jnp.concatenate(
        [qt[h * dh:(h + 1) * dh, i * Q_BLOCK:(i + 1) * Q_BLOCK] for i in range(n_tiles) for h in range(NSA_HPG)],
        axis=1).astype(BF16)
    qa_ref[dh:, :] = jnp.concatenate([sa_ref[...]] * n_tiles, axis=1)


def _nsa_select_body(nch, t_min, q_ref, sa_ref, kc_ref, vct_ref, slope_ref, oc_ref, sel_ref, flag_ref,
                     qa_ref, s_ref, pb_ref, p_ref):
    ck = CMP_CHUNK
    nk = nch * ck
    ns = nk * CMP_STRIDE // SLC_BLOCK
    n_t = SEL_TILES
    lanes = n_t * QL
    first_tile = pl.program_id(2) * n_t
    _load_query(qa_ref, q_ref, sa_ref)
    t = _tile_positions(first_tile, n_t)
    slope2 = jnp.concatenate([slope_ref[...] * LOG2E] * n_t, axis=1)

    rows = []
    mc = None
    for ch in range(nch):
        rs = slice(ch * ck, (ch + 1) * ck)
        s = _dot(kc_ref[rs, :], qa_ref[...])
        if ((ch + 1) * ck - 1) * CMP_STRIDE + (CMP_BLOCK - 1) > t_min:
            end = (ch * ck + lax.broadcasted_iota(jnp.int32, (ck, 1), 0)) * CMP_STRIDE + (CMP_BLOCK - 1)
            s = jnp.where(t >= end, s, NEG)
        s_ref[rs, :] = s
        brow = slope2 * (ch * ck * CMP_STRIDE + (CMP_BLOCK - 1) - first_tile * Q_BLOCK).astype(F32)
        rows.append(brow)
        cm = jnp.max(s.reshape(ck // 8, 8, lanes), axis=0) + brow
        mc = cm if mc is None else jnp.maximum(mc, cm)
    m = jnp.max(mc, axis=0, keepdims=True)

    lsum = None
    for ch in range(nch):
        rs = slice(ch * ck, (ch + 1) * ck)
        p = jnp.exp2(s_ref[rs, :] - (m - rows[ch]))
        ls = jnp.sum(p.reshape(ck // 8, 8, lanes), axis=0)
        lsum = ls if lsum is None else lsum + ls
        s_ref[rs, :] = p
        pb_ref[rs, :] = p.astype(BF16)
    l = jnp.sum(lsum, axis=0, keepdims=True)
    inv = jnp.where(t >= CMP_BLOCK - 1, 1.0 / jnp.maximum(l, TINY), 0.0)
    oc = _dot(vct_ref[:, 0:nk], pb_ref[0:nk, :]) * inv
    for i in range(n_t):
        oc_ref[i] = oc[:, i * QL:(i + 1) * QL]

    for i in range(n_t):
        p_ref[i, 0:8, :] = jnp.zeros((8, Q_BLOCK), F32)
        p_ref[i, 8 + nk:16 + nk, :] = jnp.zeros((8, Q_BLOCK), F32)
    for ch in range(nch):
        pn = s_ref[ch * ck:(ch + 1) * ck, :] * inv
        for i in range(n_t):
            acc = pn[:, i * QL:i * QL + Q_BLOCK]
            for h in range(1, NSA_HPG):
                acc = acc + pn[:, i * QL + h * Q_BLOCK:i * QL + (h + 1) * Q_BLOCK]
            p_ref[i, 8 + ch * ck:8 + (ch + 1) * ck, :] = acc

    ratio = SLC_BLOCK // CMP_STRIDE
    imp = []
    for i in range(n_t):
        acc = p_ref[i, pl.ds(7, ns, stride=ratio), :]
        for r in range(1, ratio + 1):
            acc = acc + p_ref[i, pl.ds(7 + r, ns, stride=ratio), :]
        imp.append(acc)
    imp = jnp.concatenate(imp, axis=1)

    j = lax.broadcasted_iota(jnp.int32, (ns, 1), 0)
    cur = jnp.concatenate([t[:, i * QL:i * QL + Q_BLOCK] for i in range(n_t)], axis=1) // SLC_BLOCK
    forced = (j == 0) | (j == cur) | (j == cur - 1)
    imp = jnp.where(forced, NEG, jnp.where(j > cur, -1.0, imp))
    for _ in range(min(SLC_TOPK, ns) - 3):
        mx = jnp.max(imp, axis=0, keepdims=True)
        idx = jnp.min(jnp.where(imp == mx, j, ns), axis=0, keepdims=True)
        imp = jnp.where(j == idx, NEG, imp)
    sel = jnp.where((imp == NEG) & (j <= cur), 1.0, 0.0)
    ns_all, nw, nw_all = sel_ref.shape[1], ns // FLAG_BITS, flag_ref.shape[1]
    bit = jnp.left_shift(1, j & (FLAG_BITS - 1)).astype(F32)
    for i in range(n_t):
        sel_i = sel[:, i * Q_BLOCK:(i + 1) * Q_BLOCK]
        sel_ref[i, 0:ns, :] = sel_i
        if ns < ns_all:
            sel_ref[i, ns:, :] = jnp.zeros((ns_all - ns, Q_BLOCK), F32)
        chosen = jnp.max(sel_i, axis=1, keepdims=True)
        words = jnp.sum((chosen * bit).reshape(nw, FLAG_BITS, 1), axis=1)
        flag_ref[i, 0:nw, :] = jnp.broadcast_to(words, (nw, flag_ref.shape[2]))
        if nw < nw_all:
            flag_ref[i, nw:, :] = jnp.zeros((nw_all - nw, flag_ref.shape[2]), F32)


def _nsa_select_kernel(q_ref, sa_ref, kc_ref, *rest):
    n_chunks = kc_ref.shape[0] // CMP_CHUNK
    n_steps = kc_ref.shape[0] * CMP_STRIDE // (SEL_TILES * Q_BLOCK)
    n_cls = min(SEL_CLASSES, n_chunks)
    cls = pl.program_id(2) * n_cls // n_steps
    for c in range(n_cls):
        @pl.when(cls == c)
        def _(c=c):
            first_step = -(-c * n_steps // n_cls)
            _nsa_select_body((c + 1) * n_chunks // n_cls, first_step * SEL_TILES * Q_BLOCK,
                             q_ref, sa_ref, kc_ref, *rest)


def nsa_select(u, q_col, slope_rows, kc, vct, slopes):
    b, g, nc = kc.shape[:3]
    dh, ql, n_t = NSA_HEAD_DIM, QL, SEL_TILES
    nqb = u.shape[0] // (b * Q_BLOCK)
    n_steps = nqb // n_t
    ns = nc * CMP_STRIDE // SLC_BLOCK
    tiles = lambda *shape: pl.BlockSpec((None, None, n_t) + shape, lambda bi, gi, qi: (bi, gi, qi, 0, 0))
    return pl.pallas_call(
        _nsa_select_kernel,
        grid=(b, g, n_steps),
        in_specs=[
            pl.BlockSpec((n_t * Q_BLOCK, NSA_HPG * dh), lambda bi, gi, qi: (bi * n_steps + qi, q_col + gi)),
            pl.BlockSpec((None, KEY_LANES - dh, ql), lambda bi, gi, qi: (gi, 0, 0)),
            pl.BlockSpec((None, None, nc, KEY_LANES), lambda bi, gi, qi: (bi, gi, 0, 0)),
            pl.BlockSpec((None, None, dh, nc), lambda bi, gi, qi: (bi, gi, 0, 0)),
            pl.BlockSpec((None, 1, ql), lambda bi, gi, qi: (gi, 0, 0)),
        ],
        out_specs=[tiles(dh, ql), tiles(ns, Q_BLOCK), tiles(ns // FLAG_BITS, 128)],
        out_shape=[
            jax.ShapeDtypeStruct((b, g, nqb, dh, ql), F32),
            jax.ShapeDtypeStruct((b, g, nqb, ns, Q_BLOCK), F32),
            jax.ShapeDtypeStruct((b, g, nqb, ns // FLAG_BITS, 128), F32),
        ],
        scratch_shapes=[
            pltpu.VMEM((KEY_LANES, n_t * ql), BF16), pltpu.VMEM((nc, n_t * ql), F32),
            pltpu.VMEM((nc, n_t * ql), BF16), pltpu.VMEM((n_t, nc + 16, Q_BLOCK), F32),
        ],
        compiler_params=_params("parallel", "parallel", "parallel"),
        name="nsa_select",
    )(u, slope_rows, kc, vct, slopes)


ATT_GROUP = 8
LIST_PAD = 3 * ATT_GROUP
DIAG_BLOCKS = Q_BLOCK // SLC_BLOCK
WIN_BLOCKS = WINDOW // SLC_BLOCK


def _score_blocks(k_src, blocks, qa_ref, kcat_ref, s_ref):
    kb = SLC_BLOCK
    dh = k_src.shape[-1]
    for b, j in enumerate(blocks):
        kcat_ref[b * kb:(b + 1) * kb, 0:dh] = k_src[jnp.maximum(j, 0)]
    n = len(blocks) * kb
    s_ref[0:n, :] = _dot(kcat_ref[0:n, :], qa_ref[...])


def _write_position_columns(kcat_ref):
    n, dh = kcat_ref.shape[-2], NSA_HEAD_DIM
    offset = lax.broadcasted_iota(jnp.int32, (n, KEY_LANES - dh), 0) % SLC_BLOCK
    col = lax.broadcasted_iota(jnp.int32, (n, KEY_LANES - dh), 1)
    kcat_ref[:, dh:] = jnp.where(col < SLOPE_PARTS, offset, 0).astype(F32).astype(BF16)


def _softmax_step(v_src, blocks, masks, sel_rows, t, slope2, blk0, s_ref, vcat_ref, p_ref, m_ref, l_ref, acc_ref):
    nb = len(blocks)
    kb = SLC_BLOCK
    for b, j in enumerate(blocks):
        vcat_ref[b * kb:(b + 1) * kb, :] = v_src[jnp.maximum(j, 0)]
    rows = []
    mc = None
    for b, j in enumerate(blocks):
        brow = slope2 * ((j - blk0) * kb).astype(F32)
        if sel_rows[b] is not None:
            brow = jnp.where(sel_rows[b] > 0.0, brow, NEG)
        brow = jnp.where(j >= 0, brow, NEG)
        rows.append(brow)
        s = s_ref[b * kb:(b + 1) * kb, :]
        if masks[b] is not None:
            kpos = j * kb + lax.broadcasted_iota(jnp.int32, (kb, 1), 0)
            valid = (t >= kpos) if masks[b] == "causal" else (t - kpos < WINDOW)
            s = jnp.where(valid, s, NEG)
            s_ref[b * kb:(b + 1) * kb, :] = s
        cm = jnp.max(s.reshape(kb // 8, 8, QL), axis=0) + brow
        mc = cm if mc is None else jnp.maximum(mc, cm)
    m_old = m_ref[...]
    m_new = jnp.maximum(m_old, jnp.max(mc, axis=0, keepdims=True))
    alpha = jnp.exp2(m_old - m_new)
    m_ref[...] = m_new

    lsum = None
    for b in range(nb):
        p = jnp.exp2(s_ref[b * kb:(b + 1) * kb, :] - (m_new - rows[b]))
        ls = jnp.sum(p.reshape(kb // 8, 8, QL), axis=0)
        lsum = ls if lsum is None else lsum + ls
        p_ref[b * kb:(b + 1) * kb, :] = p.astype(BF16)
    l_ref[...] = alpha * l_ref[...] + lsum
    acc_ref[...] = alpha * acc_ref[...] + _dot_tn(vcat_ref[0:nb * kb, :], p_ref[0:nb * kb, :])


def _nsa_attend_kernel(fw_ref, q_ref, sa_ref, ks_ref, vs_ref, kw_ref, vw_ref, sel_ref, oc_ref, gate_ref, slope_ref,
                       o_ref, qa_ref, kcat_ref, s_ref, vcat_ref, p_ref, m_ref, l_ref, acc_ref,
                       kcat_w_ref, s_w_ref, vcat_w_ref, p_w_ref, m_w_ref, l_w_ref, acc_w_ref,
                       kcat_f_ref, s_f_ref, vcat_f_ref, p_f_ref, list_ref, *,
                       words_per_tile):
    bi, gi, qb = pl.program_id(0), pl.program_id(1), pl.program_id(2)
    tile_id = (bi * pl.num_programs(1) + gi) * pl.num_programs(2) + qb

    @pl.when(qb == 0)
    def _():
        for slot in range(2):
            _write_position_columns(kcat_ref.at[slot])
        _write_position_columns(kcat_w_ref)
        _write_position_columns(kcat_f_ref)

    _load_query(qa_ref, q_ref, sa_ref)
    t = _tile_positions(qb)
    slope2 = slope_ref[...] * LOG2E
    blk0 = qb * DIAG_BLOCKS
    diag = [blk0 + i for i in range(DIAG_BLOCKS)]
    past = [blk0 - WIN_BLOCKS + i for i in range(WIN_BLOCKS)]

    def sel_row(j):
        row = sel_ref[pl.ds(jnp.maximum(j, 0), 1), :]
        return jnp.concatenate([row] * NSA_HPG, axis=1)

    def reset(m, l, acc):
        m[...] = jnp.full_like(m, NEG)
        l[...] = jnp.zeros_like(l)
        acc[...] = jnp.zeros_like(acc)

    def result(l, acc):
        return acc[...] * (1.0 / jnp.maximum(jnp.sum(l[...], axis=0, keepdims=True), TINY))

    def scan_flags(w, n):
        base = w * FLAG_BITS
        word = fw_ref[tile_id * words_per_tile + w] & ((1 << jnp.minimum(blk0 - base, FLAG_BITS)) - 1)
        for i in range(FLAG_BITS):
            list_ref[n] = base + i
            n = n + ((word >> i) & 1)
        return n

    n_sel = lax.fori_loop(0, (blk0 + FLAG_BITS - 1) // FLAG_BITS, scan_flags, 0)
    for i in range(LIST_PAD):
        list_ref[n_sel + i] = -1

    def listed(i):
        return [list_ref[i * ATT_GROUP + b] for b in range(ATT_GROUP)]

    first = diag + listed(0)
    _score_blocks(kw_ref, past + diag, qa_ref, kcat_w_ref, s_w_ref)
    _score_blocks(ks_ref, first, qa_ref, kcat_f_ref, s_f_ref)
    _score_blocks(ks_ref, listed(1), qa_ref, kcat_ref.at[1], s_ref.at[1])

    slc_state = (m_ref, l_ref, acc_ref)
    reset(*slc_state)
    _softmax_step(vs_ref, first, ["causal"] * DIAG_BLOCKS + [None] * ATT_GROUP, [sel_row(j) for j in first],
                  t, slope2, blk0, s_f_ref, vcat_f_ref, p_f_ref, *slc_state)

    win_state = (m_w_ref, l_w_ref, acc_w_ref)
    reset(*win_state)
    masks = ["window"] * DIAG_BLOCKS + [None] * (WIN_BLOCKS - DIAG_BLOCKS) + ["causal"] * DIAG_BLOCKS
    _softmax_step(vw_ref, past + diag, masks, [None] * len(masks), t, slope2, blk0,
                  s_w_ref, vcat_w_ref, p_w_ref, *win_state)

    n_steps = jnp.maximum((n_sel + ATT_GROUP - 1) // ATT_GROUP - 1, 0)

    def slc_step(i, slot):
        _score_blocks(ks_ref, listed(i + 2), qa_ref, kcat_ref.at[slot], s_ref.at[slot])
        blocks = listed(i + 1)
        _softmax_step(vs_ref, blocks, [None] * ATT_GROUP, [sel_row(j) for j in blocks], t, slope2, blk0,
                      s_ref.at[1 - slot], vcat_ref, p_ref, *slc_state)

    def slc_body(i2, carry):
        slc_step(2 * i2, 0)

        @pl.when(2 * i2 + 1 < n_steps)
        def _():
            slc_step(2 * i2 + 1, 1)

        return carry

    lax.fori_loop(0, (n_steps + 1) // 2, slc_body, 0)
    o_slc = result(l_ref, acc_ref)
    o_win = result(l_w_ref, acc_w_ref)

    gate = _sigmoid(gate_ref[...])
    o = gate[0:1, :] * oc_ref[...] + gate[1:2, :] * o_slc + gate[2:3, :] * o_win
    o = jnp.concatenate([o[:, h * Q_BLOCK:(h + 1) * Q_BLOCK] for h in range(NSA_HPG)], axis=0)
    o_ref[...] = o.T.astype(o_ref.dtype)


def nsa_attend(flag_words, u, q_col, slope_rows, ks, vs, kw, vw, sel, oc, gates, slopes):
    b, g, ns = ks.shape[:3]
    dh, ql = NSA_HEAD_DIM, QL
    nqb = u.shape[0] // (b * Q_BLOCK)
    gkeys = ATT_GROUP * SLC_BLOCK
    wkeys = (WIN_BLOCKS + DIAG_BLOCKS) * SLC_BLOCK
    fkeys = (DIAG_BLOCKS + ATT_GROUP) * SLC_BLOCK
    tile = lambda *shape: pl.BlockSpec((None, None, None) + shape, lambda bi, gi, qi, fw: (bi, gi, qi, 0, 0))
    seq = lambda *shape: pl.BlockSpec((None, None) + shape, lambda bi, gi, qi, fw: (bi, gi, 0, 0, 0))
    state = [pltpu.VMEM((1, ql), F32), pltpu.VMEM((8, ql), F32), pltpu.VMEM((dh, ql), F32)]
    grid_spec = pltpu.PrefetchScalarGridSpec(
        num_scalar_prefetch=1,
        grid=(b, g, nqb),
        in_specs=[
            pl.BlockSpec((Q_BLOCK, NSA_HPG * dh), lambda bi, gi, qi, fw: (bi * nqb + qi, q_col + gi)),
            pl.BlockSpec((None, KEY_LANES - dh, ql), lambda bi, gi, qi, fw: (gi, 0, 0)),
            seq(ns, SLC_BLOCK, dh), seq(ns, SLC_BLOCK, dh),
            seq(ns, SLC_BLOCK, dh), seq(ns, SLC_BLOCK, dh),
            tile(ns, Q_BLOCK), tile(dh, ql), tile(3, ql),
            pl.BlockSpec((None, 1, ql), lambda bi, gi, qi, fw: (gi, 0, 0)),
        ],
        out_specs=pl.BlockSpec((Q_BLOCK, NSA_HPG * dh), lambda bi, gi, qi, fw: (bi * nqb + qi, gi)),
        scratch_shapes=[
            pltpu.VMEM((KEY_LANES, ql), BF16),
            pltpu.VMEM((2, gkeys, KEY_LANES), BF16), pltpu.VMEM((2, gkeys, ql), F32),
            pltpu.VMEM((gkeys, dh), BF16), pltpu.VMEM((gkeys, ql), BF16), *state,
            pltpu.VMEM((wkeys, KEY_LANES), BF16), pltpu.VMEM((wkeys, ql), F32),
            pltpu.VMEM((wkeys, dh), BF16), pltpu.VMEM((wkeys, ql), BF16), *state,
            pltpu.VMEM((fkeys, KEY_LANES), BF16), pltpu.VMEM((fkeys, ql), F32),
            pltpu.VMEM((fkeys, dh), BF16), pltpu.VMEM((fkeys, ql), BF16),
            pltpu.SMEM((ns + LIST_PAD,), jnp.int32),
        ],
    )
    return pl.pallas_call(
        functools.partial(_nsa_attend_kernel, words_per_tile=ns // FLAG_BITS),
        grid_spec=grid_spec,
        out_shape=jax.ShapeDtypeStruct((b * nqb * Q_BLOCK, g * NSA_HPG * dh), BF16),
        compiler_params=_params("parallel", "parallel", "arbitrary"),
        name="nsa_attend",
    )(flag_words, u, slope_rows, ks, vs, kw, vw, sel, oc, gates, slopes)


def nsa_mixer(u, q_col, kc, vc, ks, vs, kw, vw, gate_logits, cmp_pos, cmp_w1, cmp_b1, cmp_w2, cmp_b2, *,
              batch, seq):
    g, hpg, dh = NSA_KV_GROUPS, NSA_HPG, NSA_HEAD_DIM
    nqb, ns, nc = seq // Q_BLOCK, seq // SLC_BLOCK, seq // CMP_STRIDE

    gates = gate_logits.astype(F32).reshape(batch, nqb, Q_BLOCK, 3, g, hpg).transpose(0, 4, 1, 3, 5, 2)
    gates = gates.reshape(batch, g, nqb, 3, QL)
    slopes = 2.0 ** (-8.0 * jnp.arange(1, NSA_HEADS + 1, dtype=F32) / NSA_HEADS)
    slopes = jnp.repeat(slopes.reshape(g, 1, hpg), Q_BLOCK, axis=2)

    def halves(x):
        x = x.reshape(batch, nc, CMP_STRIDE, g, dh).transpose(0, 3, 1, 2, 4)
        return x.reshape(batch * g, nc, CMP_STRIDE * dh)

    def blocks(x):
        return x.reshape(batch, ns, SLC_BLOCK, g, dh).transpose(0, 3, 1, 2, 4)

    s2 = slopes * LOG2E
    parts = []
    for _ in range(SLOPE_PARTS):
        part = s2.astype(BF16)
        parts.append(part)
        s2 = s2 - part.astype(F32)
    slope_rows = jnp.concatenate(parts + [jnp.zeros((g, KEY_LANES - dh - SLOPE_PARTS, QL), BF16)], axis=1)

    cmp = compress_tokens(jnp.stack([halves(kc), halves(vc)]), cmp_pos, cmp_w1, cmp_b1, cmp_w2, cmp_b2)
    cmp = cmp.astype(BF16).reshape(2, batch, g, nc, dh)
    r = (jnp.arange(nc) % CMP_CHUNK * CMP_STRIDE).astype(BF16).reshape(1, 1, nc, 1)
    k_cmp = jnp.concatenate([cmp[0], jnp.broadcast_to(r, (batch, g, nc, SLOPE_PARTS)),
                             jnp.zeros((batch, g, nc, KEY_LANES - dh - SLOPE_PARTS), BF16)], axis=-1)
    v_cmp_t = cmp[1].transpose(0, 1, 3, 2)

    oc, sel, flags = nsa_select(u, q_col, slope_rows, k_cmp, v_cmp_t, slopes)
    flag_words = flags[:, :, :, :, 0].astype(jnp.int32).reshape(-1)
    return nsa_attend(flag_words, u, q_col, slope_rows, blocks(ks), blocks(vs), blocks(kw), blocks(vw),
                      sel, oc, gates, slopes)


ROW_TILE = 1024
COL_TILE = 1024
OUT_ROW_TILE = 512
FFN_ROW_TILE = 1024
FFN_HID_TILE = 512
CONV_SEQ_TILE = 512
HGRN_SEQ_TILE = 1024


def _even_layer(x, nw, w_in, conv_w, conv_b, ln_w, ln_b, cmp_pos, cmp_w1, cmp_b1, cmp_w2, cmp_b2, w_out, *,
                batch, seq):
    c = conv_w.shape[-1]
    nq = NSA_HEADS * NSA_HEAD_DIM
    nkv = NSA_KV_GROUPS * NSA_HEAD_DIM
    n_in = w_in.shape[1]
    n_pad = -(-n_in // 128) * 128
    w_in = jnp.pad(w_in, ((0, 0), (0, n_pad - n_in))).astype(BF16)
    u = norm_matmul_resident(x, nw, w_in, tm=OUT_ROW_TILE)
    a_out = conformer_conv(u, conv_w.reshape(CONV_WIDTH, c), conv_b, ln_w, ln_b, seq=seq, ts=CONV_SEQ_TILE)
    off = 2 * c
    q_col = off // (NSA_HPG * NSA_HEAD_DIM)
    off += nq
    kvs = [u[:, off + i * nkv:off + (i + 1) * nkv] for i in range(6)]
    off += 6 * nkv
    gate_logits = u[:, off:off + 3 * NSA_HEADS]
    b_out = nsa_mixer(u, q_col, *kvs, gate_logits, cmp_pos, cmp_w1, cmp_b1, cmp_w2, cmp_b2,
                      batch=batch, seq=seq)
    return matmul_residual(a_out, b_out, 0, 0, w_out.astype(BF16), x, tm=OUT_ROW_TILE, tn=w_out.shape[1])


def _odd_layer(x, nw, w_in, lb_gamma, gnorm_w, w_out, *, layer, batch, seq):
    d = w_in.shape[0]
    w_in = w_in.reshape(d, 4, HG_HEADS, HG_DK).transpose(0, 2, 1, 3).reshape(d, 4 * HG_HEADS * HG_DK)
    u = norm_matmul(x, nw, w_in.astype(BF16), tm=ROW_TILE, tn=COL_TILE)
    o = hgrn2(u, lb_gamma, gnorm_w, layer=layer, batch=batch, seq=seq, tt=HGRN_SEQ_TILE)
    return matmul_residual(o, o, 0, 1, w_out.astype(BF16), x, tm=OUT_ROW_TILE, tn=w_out.shape[1])


def kernel(x, norm_w, final_norm_w, ev_w_in, ev_conv_w, ev_conv_b, ev_conv_ln_w, ev_conv_ln_b, ev_cmp_pos,
           ev_cmp_w1, ev_cmp_b1, ev_cmp_w2, ev_cmp_b2, ev_w_out, od_w_in, od_lb_gamma, od_gnorm_w, od_w_out,
           ffn_w_gu, ffn_w_down):
    batch, seq, d = x.shape
    depth = norm_w.shape[0]
    xs = x.reshape(batch * seq, d)
    for layer in range(depth):
        i = layer // 2
        if layer % 2 == 0:
            xs = _even_layer(xs, norm_w[layer, 0], ev_w_in[i], ev_conv_w[i], ev_conv_b[i], ev_conv_ln_w[i],
                             ev_conv_ln_b[i], ev_cmp_pos[i], ev_cmp_w1[i], ev_cmp_b1[i], ev_cmp_w2[i],
                             ev_cmp_b2[i], ev_w_out[i], batch=batch, seq=seq)
        else:
            xs = _odd_layer(xs, norm_w[layer, 0], od_w_in[i], od_lb_gamma.astype(F32), od_gnorm_w[i],
                            od_w_out[i], layer=layer, batch=batch, seq=seq)
        xs = ffn_block(xs, norm_w[layer, 1], ffn_w_gu[layer].astype(BF16), ffn_w_down[layer].astype(BF16),
                       final_norm_w, tm=FFN_ROW_TILE, th=FFN_HID_TILE, final_norm=layer == depth - 1)
    return xs.reshape(batch, seq, d)
```
